```python
import math
import jax, jax.numpy as jnp
from jax import lax
import numpy as np

D_MODEL = 1024
BATCH = 16
SEQ = 2048
DEPTH = 2

N_MEM = 256
EPS = 1e-6

D_MIX = D_MODEL
D_CONV = D_MIX // 4
CONV_WIDTH = 31
SSD_HEAD_DIM = 64
D_SSD = D_MIX // 2
SSD_HEADS = D_SSD // SSD_HEAD_DIM
SSD_GROUPS = 2
SSD_STATE = 128
SSD_CONV = 4
SSD_CHUNK = 128
D_XBC = D_SSD + 2 * SSD_GROUPS * SSD_STATE
ATT_HEAD_DIM = 64
D_ATT = D_MIX // 4
ATT_HEADS = D_ATT // ATT_HEAD_DIM
IDX_HEADS = 8
IDX_DIM = 64
TOPK_MAX = 256
Q_BLOCK = 128
ROPE_THETA = 500000.0
ROT_DIM = ATT_HEAD_DIM // 4
D_IN = 2 * D_CONV + D_SSD + D_XBC + SSD_HEADS + 3 * D_ATT + IDX_HEADS * IDX_DIM + IDX_DIM + IDX_HEADS
MEM_HEADS = 4
MEM_HEAD_DIM = 64
D_MEMATT = MEM_HEADS * MEM_HEAD_DIM
D_FF = 2816
N_EXPERTS = 8
TOP_K = 2
MOE_BLOCK = 256
N_DENSE = (DEPTH + 1) // 2
N_MOE = DEPTH // 2

kernel_name = 'hybrid_conv_ssd_dsa_moe_block'


def rms_norm(x, g):
    xf = x.astype(jnp.float32)
    y = xf * lax.rsqrt(jnp.mean(xf * xf, axis=-1, keepdims=True) + EPS)
    return (y * g.astype(jnp.float32)).astype(x.dtype)


def layer_norm(x, g, b):
    xf = x.astype(jnp.float32)
    mu = jnp.mean(xf, axis=-1, keepdims=True)
    xc = xf - mu
    y = xc * lax.rsqrt(jnp.mean(xc * xc, axis=-1, keepdims=True) + EPS)
    return (y * g.astype(jnp.float32) + b.astype(jnp.float32)).astype(x.dtype)


def rope_tables(positions):
    inv = ROPE_THETA ** (-jnp.arange(0, ROT_DIM, 2, dtype=jnp.float32) / ROT_DIM)
    ang = positions.astype(jnp.float32)[..., None] * inv
    return jnp.cos(ang), jnp.sin(ang)


def partial_rope(x, cos, sin):
    half = ROT_DIM // 2
    c = cos[:, :, None, :]
    s = sin[:, :, None, :]
    xf = x.astype(jnp.float32)
    x1 = xf[..., :half]
    x2 = xf[..., half:ROT_DIM]
    out = jnp.concatenate([x1 * c - x2 * s, x2 * c + x1 * s, xf[..., ROT_DIM:]], axis=-1)
    return out.astype(x.dtype)


def causal_dwconv(x, w):
    k = w.shape[0]
    return lax.conv_general_dilated(
        x, w[:, None, :].astype(x.dtype), window_strides=(1,), padding=[(k - 1, 0)],
        dimension_numbers=('NWC', 'WIO', 'NWC'), feature_group_count=x.shape[-1])


def conformer_conv(u, dw_w, ln_g, ln_b, pw_w):
    h = u[..., :D_CONV] * jax.nn.sigmoid(u[..., D_CONV:])
    h = causal_dwconv(h, dw_w)
    h = jax.nn.silu(layer_norm(h, ln_g, ln_b))
    return h @ pw_w.astype(h.dtype)


def _segsum(a):
    cs = jnp.cumsum(a, axis=-1)
    seg = cs[..., :, None] - cs[..., None, :]
    t = a.shape[-1]
    mask = jnp.tril(jnp.ones((t, t), dtype=bool))
    return jnp.where(mask, seg, -jnp.inf)


def ssd_chunked(xd, a, bm, cm):
    b, s, h, p = xd.shape
    n = bm.shape[-1]
    c, l = s // SSD_CHUNK, SSD_CHUNK
    xd = xd.reshape(b, c, l, h, p)
    bm = bm.reshape(b, c, l, h, n)
    cm = cm.reshape(b, c, l, h, n)
    a = a.reshape(b, c, l, h).transpose(0, 3, 1, 2)
    a_cs = jnp.cumsum(a, axis=-1)
    decay_in = jnp.exp(_segsum(a))
    cb = jnp.einsum('bclhn,bcshn->bhcls', cm, bm) * decay_in
    y_diag = jnp.einsum('bhcls,bcshp->bclhp', cb, xd)
    decay_states = jnp.exp(a_cs[..., -1:] - a_cs).transpose(0, 2, 3, 1)
    states = jnp.einsum('bclhn,bclhp->bchpn', bm * decay_states[..., None], xd)
    states = jnp.concatenate([jnp.zeros_like(states[:, :1]), states], axis=1)
    decay_chunk = jnp.exp(_segsum(jnp.pad(a_cs[..., -1], ((0, 0), (0, 0), (1, 0)))))
    states = jnp.einsum('bhzc,bchpn->bzhpn', decay_chunk, states)[:, :-1]
    y_off = jnp.einsum('bclhn,bchpn->bclhp', cm, states) * jnp.exp(a_cs).transpose(0, 2, 3, 1)[..., None]
    return (y_diag + y_off).reshape(b, s, h, p)


def ssd_mixer(z, xbc, dt_raw, conv_w, conv_b, dt_bias, a_log, d_skip, norm_g):
    b, s, _ = xbc.shape
    xbc = jax.nn.silu(causal_dwconv(xbc, conv_w) + conv_b.astype(xbc.dtype))
    xs = xbc[..., :D_SSD].reshape(b, s, SSD_HEADS, SSD_HEAD_DIM).astype(jnp.float32)
    bc = xbc[..., D_SSD:].reshape(b, s, 2, SSD_GROUPS, SSD_STATE).astype(jnp.float32)
    rep = SSD_HEADS // SSD_GROUPS
    bm = jnp.repeat(bc[:, :, 0], rep, axis=2)
    cm = jnp.repeat(bc[:, :, 1], rep, axis=2)
    dt = jax.nn.softplus(dt_raw.astype(jnp.float32) + dt_bias.astype(jnp.float32))
    a = -jnp.exp(a_log.astype(jnp.float32))
    y = ssd_chunked(xs * dt[..., None], dt * a, bm, cm)
    y = y + xs * d_skip.astype(jnp.float32)[:, None]
    y = y.reshape(b, s, D_SSD) * jax.nn.silu(z.astype(jnp.float32))
    y = rms_norm(y.reshape(b, s, SSD_GROUPS, D_SSD // SSD_GROUPS),
                 norm_g.reshape(SSD_GROUPS, D_SSD // SSD_GROUPS)).reshape(b, s, D_SSD)
    return y.astype(z.dtype)


def dsa_attention(q, k, v, q_idx, k_idx, w_idx, cos, sin, q_g, k_g):
    b, s, _ = q.shape
    q = partial_rope(rms_norm(q.reshape(b, s, ATT_HEADS, ATT_HEAD_DIM), q_g), cos, sin)
    k = partial_rope(rms_norm(k.reshape(b, s, ATT_HEADS, ATT_HEAD_DIM), k_g), cos, sin)
    v = v.reshape(b, s, ATT_HEADS, ATT_HEAD_DIM)
    q_idx = partial_rope(q_idx.reshape(b, s, IDX_HEADS, IDX_DIM), cos, sin).astype(jnp.float32)
    k_idx = partial_rope(k_idx[:, :, None, :], cos, sin)[:, :, 0].astype(jnp.float32)
    w = w_idx.astype(jnp.float32) * (IDX_HEADS ** -0.5 * IDX_DIM ** -0.5)
    kv = jnp.concatenate([k, v], axis=-1)
    k_sel = min(TOPK_MAX, s // 4)
    nb = s // Q_BLOCK
    key_pos = jnp.arange(s)

    def to_blocks(t):
        return jnp.moveaxis(t.reshape((b, nb, Q_BLOCK) + t.shape[2:]), 1, 0)

    def block(args):
        qb, qib, wb, start = args
        qpos = start + jnp.arange(Q_BLOCK)
        logits = jnp.einsum('bthd,bsd->bths', qib, k_idx)
        score = jnp.einsum('bth,bths->bts', wb, jax.nn.relu(logits))
        score = jnp.where((key_pos[None, :] <= qpos[:, None])[None], score, -jnp.inf)
        _, idx = lax.top_k(score, k_sel)
        valid = idx <= qpos[None, :, None]
        kv_sel = jax.vmap(lambda kvb, ib: kvb[ib])(kv, idx)
        k_s = kv_sel[..., :ATT_HEAD_DIM]
        v_s = kv_sel[..., ATT_HEAD_DIM:]
        att = jnp.einsum('bthd,btkhd->bhtk', qb.astype(jnp.float32), k_s.astype(jnp.float32)) * (ATT_HEAD_DIM ** -0.5)
        att = jnp.where(valid[:, None], att, -jnp.inf)
        p = jax.nn.softmax(att, axis=-1)
        return jnp.einsum('bhtk,btkhd->bthd', p.astype(v_s.dtype), v_s)

    starts = jnp.arange(nb, dtype=jnp.int32) * Q_BLOCK
    out = lax.map(block, (to_blocks(q), to_blocks(q_idx), to_blocks(w), starts))
    return jnp.moveaxis(out, 0, 1).reshape(b, s, D_ATT)


def memory_cross_attention(h, m, w_q, w_kv, q_g, k_g, w_o):
    b, s, _ = h.shape
    nm = m.shape[1]
    q = rms_norm((h @ w_q).reshape(b, s, MEM_HEADS, MEM_HEAD_DIM), q_g)
    kv = (m @ w_kv).reshape(b, nm, 2, MEM_HEADS, MEM_HEAD_DIM)
    k = rms_norm(kv[:, :, 0], k_g)
    v = kv[:, :, 1]
    att = jnp.einsum('bshd,bmhd->bhsm', q.astype(jnp.float32), k.astype(jnp.float32)) * (MEM_HEAD_DIM ** -0.5)
    p = jax.nn.softmax(att, axis=-1)
    o = jnp.einsum('bhsm,bmhd->bshd', p.astype(v.dtype), v).reshape(b, s, D_MEMATT)
    return o @ w_o


def swiglu(h, wg, wu, wd):
    return (jax.nn.silu(h @ wg) * (h @ wu)) @ wd


def moe_swiglu(h, w_router, w_gate, w_up, w_down):
    b, s, d = h.shape
    hf = h.reshape(b * s, d)
    logits = (hf @ w_router).astype(jnp.float32)
    top_v, top_i = lax.top_k(logits, TOP_K)
    gates = jax.nn.softmax(top_v, axis=-1)
    n_assign = hf.shape[0] * TOP_K
    expert_flat = top_i.reshape(-1)
    token_flat = jnp.arange(n_assign, dtype=jnp.int32) // TOP_K
    gate_flat = gates.reshape(-1)
    order = jnp.argsort(expert_flat)
    sorted_exp = expert_flat[order]
    counts = jnp.bincount(expert_flat, length=N_EXPERTS)
    starts = jnp.cumsum(counts) - counts
    padded = (counts + MOE_BLOCK - 1) // MOE_BLOCK * MOE_BLOCK
    padded_ends = jnp.cumsum(padded)
    padded_starts = padded_ends - padded
    dest = padded_starts[sorted_exp] + jnp.arange(n_assign, dtype=jnp.int32) - starts[sorted_exp]
    n_blocks = -(-n_assign // MOE_BLOCK) + N_EXPERTS
    n_slots = n_blocks * MOE_BLOCK
    slot_tok = jnp.zeros((n_slots,), jnp.int32).at[dest].set(token_flat[order])
    slot_gate = jnp.zeros((n_slots,), jnp.float32).at[dest].set(gate_flat[order])
    block_exp = jnp.minimum(
        jnp.searchsorted(padded_ends, jnp.arange(n_blocks, dtype=jnp.int32) * MOE_BLOCK, side='right'),
        N_EXPERTS - 1)

    def block(args):
        tok, g, e = args
        y = swiglu(hf[tok], w_gate[e], w_up[e], w_down[e])
        return y * g[:, None].astype(y.dtype)

    ys = lax.map(block, (slot_tok.reshape(n_blocks, MOE_BLOCK), slot_gate.reshape(n_blocks, MOE_BLOCK), block_exp))
    out = jnp.zeros_like(hf).at[slot_tok].add(ys.reshape(n_slots, d))
    return out.reshape(b, s, d)


def setup_inputs(seed: int = 0) -> dict:
    key = jax.random.key(seed)
    ks = iter(jax.random.split(key, 40))
    f32 = jnp.float32

    def nrm(shape, scale):
        return jax.random.normal(next(ks), shape, f32) * scale

    def gain(shape):
        return 1.0 + 0.02 * jax.random.normal(next(ks), shape, f32)

    x = nrm((BATCH, SEQ, D_MODEL), 1.0)
    mem = nrm((BATCH, N_MEM, D_MODEL), 1.0)
    positions = (jax.random.randint(next(ks), (BATCH, 1), 0, 4096, dtype=jnp.int32)
                 + jnp.arange(SEQ, dtype=jnp.int32)[None, :])
    mix_norm_g = gain((DEPTH, D_MODEL))
    w_in = nrm((DEPTH, D_MODEL, D_IN), D_MODEL ** -0.5)
    conv_dw_w = nrm((DEPTH, CONV_WIDTH, D_CONV), CONV_WIDTH ** -0.5)
    conv_ln_g = gain((DEPTH, D_CONV))
    conv_ln_b = nrm((DEPTH, D_CONV), 0.02)
    conv_pw_w = nrm((DEPTH, D_CONV, D_CONV), D_CONV ** -0.5)
    ssd_conv_w = nrm((DEPTH, SSD_CONV, D_XBC), SSD_CONV ** -0.5)
    ssd_conv_b = nrm((DEPTH, D_XBC), 0.02)
    u = jax.random.uniform(next(ks), (DEPTH, SSD_HEADS), f32)
    dt0 = jnp.exp(u * (math.log(0.1) - math.log(0.001)) + math.log(0.001))
    ssd_dt_bias = dt0 + jnp.log(-jnp.expm1(-dt0))
    ssd_a_log = jnp.log(jax.random.uniform(next(ks), (DEPTH, SSD_HEADS), f32, 1.0, 16.0))
    ssd_d = gain((DEPTH, SSD_HEADS))
    ssd_norm_g = gain((DEPTH, D_SSD))
    att_q_norm_g = gain((DEPTH, ATT_HEAD_DIM))
    att_k_norm_g = gain((DEPTH, ATT_HEAD_DIM))
    w_out = nrm((DEPTH, D_MIX, D_MODEL), D_MIX ** -0.5)
    memx_norm_g = gain((DEPTH, D_MODEL))
    mem_norm_g = gain((DEPTH, D_MODEL))
    memx_w_q = nrm((DEPTH, D_MODEL, D_MEMATT), D_MODEL ** -0.5)
    memx_w_kv = nrm((DEPTH, D_MODEL, 2 * D_MEMATT), D_MODEL ** -0.5)
    memx_q_norm_g = gain((DEPTH, MEM_HEAD_DIM))
    memx_k_norm_g = gain((DEPTH, MEM_HEAD_DIM))
    memx_w_o = nrm((DEPTH, D_MEMATT, D_MODEL), D_MEMATT ** -0.5)
    ffn_norm_g = gain((DEPTH, D_MODEL))
    ffn_w_gate = nrm((N_DENSE, D_MODEL, D_FF), D_MODEL ** -0.5)
    ffn_w_up = nrm((N_DENSE, D_MODEL, D_FF), D_MODEL ** -0.5)
    ffn_w_down = nrm((N_DENSE, D_FF, D_MODEL), D_FF ** -0.5)
    moe_w_router = nrm((N_MOE, D_MODEL, N_EXPERTS), D_MODEL ** -0.5)
    moe_w_gate = nrm((N_MOE, N_EXPERTS, D_MODEL, D_FF), D_MODEL ** -0.5)
    moe_w_up = nrm((N_MOE, N_EXPERTS, D_MODEL, D_FF), D_MODEL ** -0.5)
    moe_w_down = nrm((N_MOE, N_EXPERTS, D_FF, D_MODEL), D_FF ** -0.5)
    return {'x': x, 'mem': mem, 'positions': positions, 'mix_norm_g': mix_norm_g, 'w_in': w_in,
            'conv_dw_w': conv_dw_w, 'conv_ln_g': conv_ln_g, 'conv_ln_b': conv_ln_b, 'conv_pw_w': conv_pw_w,
            'ssd_conv_w': ssd_conv_w, 'ssd_conv_b': ssd_conv_b, 'ssd_dt_bias': ssd_dt_bias,
            'ssd_a_log': ssd_a_log, 'ssd_d': ssd_d, 'ssd_norm_g': ssd_norm_g,
            'att_q_norm_g': att_q_norm_g, 'att_k_norm_g': att_k_norm_g, 'w_out': w_out,
            'memx_norm_g': memx_norm_g, 'mem_norm_g': mem_norm_g, 'memx_w_q': memx_w_q,
            'memx_w_kv': memx_w_kv, 'memx_q_norm_g': memx_q_norm_g, 'memx_k_norm_g': memx_k_norm_g,
            'memx_w_o': memx_w_o, 'ffn_norm_g': ffn_norm_g, 'ffn_w_gate': ffn_w_gate,
            'ffn_w_up': ffn_w_up, 'ffn_w_down': ffn_w_down, 'moe_w_router': moe_w_router,
            'moe_w_gate': moe_w_gate, 'moe_w_up': moe_w_up, 'moe_w_down': moe_w_down}


def reference(x, mem, positions, mix_norm_g, w_in, conv_dw_w, conv_ln_g, conv_ln_b, conv_pw_w,
              ssd_conv_w, ssd_conv_b, ssd_dt_bias, ssd_a_log, ssd_d, ssd_norm_g,
              att_q_norm_g, att_k_norm_g, w_out, memx_norm_g, mem_norm_g, memx_w_q, memx_w_kv,
              memx_q_norm_g, memx_k_norm_g, memx_w_o, ffn_norm_g, ffn_w_gate, ffn_w_up, ffn_w_down,
              moe_w_router, moe_w_gate, moe_w_up, moe_w_down):
    cos, sin = rope_tables(positions)
    sizes = [2 * D_CONV, D_SSD, D_XBC, SSD_HEADS, D_ATT, D_ATT, D_ATT, IDX_HEADS * IDX_DIM, IDX_DIM, IDX_HEADS]
    for i in range(DEPTH):
        h = rms_norm(x, mix_norm_g[i])
        u = h @ w_in[i].astype(h.dtype)
        parts = []
        off = 0
        for sz in sizes:
            parts.append(u[..., off:off + sz])
            off += sz
        u_conv, z, xbc, dt_raw, q, k, v, q_idx, k_idx, w_idx = parts
        y_conv = conformer_conv(u_conv, conv_dw_w[i], conv_ln_g[i], conv_ln_b[i], conv_pw_w[i])
        y_ssd = ssd_mixer(z, xbc, dt_raw, ssd_conv_w[i], ssd_conv_b[i], ssd_dt_bias[i],
                          ssd_a_log[i], ssd_d[i], ssd_norm_g[i])
        y_att = dsa_attention(q, k, v, q_idx, k_idx, w_idx, cos, sin, att_q_norm_g[i], att_k_norm_g[i])
        y_mix = jnp.concatenate([y_conv.astype(x.dtype), y_ssd.astype(x.dtype), y_att.astype(x.dtype)], axis=-1)
        x = x + (y_mix @ w_out[i].astype(x.dtype)).astype(x.dtype)
        h = rms_norm(x, memx_norm_g[i])
        m = rms_norm(mem, mem_norm_g[i])
        x = x + memory_cross_attention(h, m, memx_w_q[i], memx_w_kv[i], memx_q_norm_g[i],
                                       memx_k_norm_g[i], memx_w_o[i]).astype(x.dtype)
        h = rms_norm(x, ffn_norm_g[i])
        j = i // 2
        if i % 2 == 0:
            y = swiglu(h, ffn_w_gate[j], ffn_w_up[j], ffn_w_down[j])
        else:
            y = moe_swiglu(h, moe_w_router[j], moe_w_gate[j], moe_w_up[j], moe_w_down[j])
        x = x + y.astype(x.dtype)
    return x
```

```python
import functools
import math

import numpy as np
import jax
import jax.numpy as jnp
from jax import lax
from jax.experimental import pallas as pl
from jax.experimental.pallas import tpu as pltpu

F32 = jnp.float32
BF16 = jnp.bfloat16
I32 = jnp.int32

EPS = 1e-6
D_MODEL = 1024
D_CONV = 256
CONV_WIDTH = 31
SSD_HEAD_DIM = 64
D_SSD = 512
SSD_HEADS = 8
SSD_GROUPS = 2
SSD_STATE = 128
SSD_CONV = 4
SSD_CHUNK = 128
D_XBC = D_SSD + 2 * SSD_GROUPS * SSD_STATE
ATT_HEAD_DIM = 64
D_ATT = 256
ATT_HEADS = 4
IDX_HEADS = 8
IDX_DIM = 64
TOPK_MAX = 256
ROPE_THETA = 500000.0
ROT_DIM = ATT_HEAD_DIM // 4
MEM_HEADS = 4
MEM_HEAD_DIM = 64
D_MEMATT = 256
D_FF = 2816
N_EXPERTS = 8

LANES = 128
SUBLANES = 8
VMEM_LIMIT_BYTES = 56 * 1024 * 1024

TOKEN_TILE = 512
SEQ_TILE = 512
Q_TILE = 128
KEY_TILE = 256
FF_CHUNK = 1408
CONV_ROWS = 64
CONV_HALO = 32
SSD_HALO = 8
INT_MIN = -2 ** 31
NEG_INF = float("-inf")

_SZ = [2 * D_CONV, D_SSD, D_XBC, SSD_HEADS, D_ATT, D_ATT, D_ATT, IDX_HEADS * IDX_DIM, IDX_DIM, IDX_HEADS]
_OFF = np.concatenate([[0], np.cumsum(_SZ)]).tolist()
N_STD = 2 * D_CONV + D_SSD + D_XBC + D_ATT + LANES
N_TRN = IDX_HEADS * IDX_DIM + D_ATT + D_ATT + 16


def _mm(a, b):
    return jnp.dot(a, b, preferred_element_type=F32)


def _mm_nt(a, b):
    return lax.dot_general(a, b, (((1,), (1,)), ((), ())), preferred_element_type=F32)


def _split3(a):
    hi = a.astype(BF16)
    r = a - hi.astype(F32)
    mid = r.astype(BF16)
    lo = (r - mid.astype(F32)).astype(BF16)
    return hi, mid, lo


def _mm_f32_lhs(a, b_exact):
    hi, mid, lo = _split3(a)
    return (_mm(hi, b_exact) + _mm(mid, b_exact)) + _mm(lo, b_exact)


def _mm_f32_rhs(a_exact, b):
    hi, mid, lo = _split3(b)
    return (_mm(a_exact, hi) + _mm(a_exact, mid)) + _mm(a_exact, lo)


def _mm_nt_f32_rhs(a_exact, b):
    hi, mid, lo = _split3(b)
    return (_mm_nt(a_exact, hi) + _mm_nt(a_exact, mid)) + _mm_nt(a_exact, lo)


def _mm_f32_f32(a, b):
    a1, a2, a3 = _split3(a)
    b1, b2, b3 = _split3(b)
    return (_mm(a1, b1) + (_mm(a1, b2) + _mm(a2, b1))) + ((_mm(a1, b3) + _mm(a2, b2)) + _mm(a3, b1))


def _silu(x):
    return x * jax.nn.sigmoid(x)


def _rms_rows(x, g):
    ms = jnp.mean(x * x, axis=-1, keepdims=True)
    return (x * lax.rsqrt(ms + EPS)) * g


def _params(n_axes):
    return pltpu.CompilerParams(dimension_semantics=("arbitrary",) * n_axes,
                                vmem_limit_bytes=VMEM_LIMIT_BYTES)


def _full(shape):
    n = len(shape)
    return pl.BlockSpec(shape, lambda *_: (0,) * n)


def _in_proj_kernel(x_ref, g_ref, wstd_ref, wtrn_ref, c128_ref, s128_ref, ct_ref, st_ref, qg_ref, kg_ref,
                    gmat_ref, pmat_ref,
                    uconv_ref, z_ref, xbc_ref, kn_ref, small_ref, kib_ref, qit_ref, qnt_ref, vt_ref, wt_ref):
    tm = x_ref.shape[0]
    hb = _rms_rows(x_ref[...], g_ref[...]).astype(BF16)
    uconv_ref[...] = _mm(hb, wstd_ref[:, 0:512])
    z_ref[...] = _mm(hb, wstd_ref[:, 512:1024])
    xbc_ref[...] = _mm(hb, wstd_ref[:, 1024:2048])

    c128 = c128_ref[...]
    s128 = s128_ref[...]
    kraw = _mm(hb, wstd_ref[:, 2048:2304])
    kn = (kraw * lax.rsqrt(_mm_f32_lhs(kraw * kraw, gmat_ref[...]) + EPS)) * kg_ref[...]
    kn = kn * jnp.concatenate([c128, c128], axis=1) + \
        _mm_f32_lhs(kn, pmat_ref[...]) * jnp.concatenate([s128, s128], axis=1)
    kn_ref[...] = kn.astype(BF16)

    sm = _mm(hb, wstd_ref[:, 2304:2432])
    lane = lax.broadcasted_iota(I32, (tm, LANES), 1)
    is_idx = lane < IDX_DIM
    sm = sm * jnp.where(is_idx, c128, 1.0) + _mm_f32_lhs(sm, pmat_ref[0:LANES, 0:LANES]) * jnp.where(is_idx, s128, 0.0)
    small_ref[...] = sm
    kib_ref[...] = jnp.where(is_idx, sm, 0.0).astype(BF16)

    out_t = _mm_nt(wtrn_ref[...], hb)
    ct = ct_ref[...]
    st = st_ref[...]

    def rope_t(blk):
        x1 = blk[0:8]
        x2 = blk[8:16]
        return jnp.concatenate([x1 * ct - x2 * st, x2 * ct + x1 * st, blk[16:64]], axis=0)

    qi = [rope_t(out_t[h * 64:(h + 1) * 64]) for h in range(IDX_HEADS)]
    qit_ref[...] = jnp.concatenate(qi, axis=0).astype(BF16)

    qn = []
    for h in range(ATT_HEADS):
        blk = out_t[512 + h * 64:512 + (h + 1) * 64]
        ms = jnp.mean(blk * blk, axis=0, keepdims=True)
        blk = (blk * lax.rsqrt(ms + EPS)) * qg_ref[...]
        qn.append(rope_t(blk) * (ATT_HEAD_DIM ** -0.5))
    qnt_ref[...] = jnp.concatenate(qn, axis=0).astype(BF16)

    vt = out_t[768:1024].astype(BF16)
    for c in range(tm // KEY_TILE):
        vt_ref[c] = vt[:, c * KEY_TILE:(c + 1) * KEY_TILE]
    wt_ref[...] = out_t[1024:1032] * (IDX_HEADS ** -0.5 * IDX_DIM ** -0.5)


def _in_proj(x2, g, wstd, wtrn, c128, s128, ct, st, qg, kg, gmat, pmat):
    t = x2.shape[0]
    tm = TOKEN_TILE
    row = lambda w: pl.BlockSpec((tm, w), lambda i: (i, 0))
    col = lambda r: pl.BlockSpec((r, tm), lambda i: (0, i))
    out_shape = (
        jax.ShapeDtypeStruct((t, 512), F32), jax.ShapeDtypeStruct((t, 512), F32),
        jax.ShapeDtypeStruct((t, 1024), F32), jax.ShapeDtypeStruct((t, 256), BF16),
        jax.ShapeDtypeStruct((t, LANES), F32), jax.ShapeDtypeStruct((t, LANES), BF16),
        jax.ShapeDtypeStruct((512, t), BF16), jax.ShapeDtypeStruct((256, t), BF16),
        jax.ShapeDtypeStruct((t // KEY_TILE, 256, KEY_TILE), BF16), jax.ShapeDtypeStruct((8, t), F32))
    out_specs = (row(512), row(512), row(1024), row(256), row(LANES), row(LANES), col(512), col(256),
                 pl.BlockSpec((tm // KEY_TILE, 256, KEY_TILE), lambda i: (i, 0, 0)), col(8))
    in_specs = [row(D_MODEL), _full((1, D_MODEL)), _full(wstd.shape), _full(wtrn.shape), row(LANES), row(LANES),
                col(8), col(8), _full((64, 1)), _full((1, 256)), _full((256, 256)), _full((256, 256))]
    return pl.pallas_call(_in_proj_kernel, grid=(t // tm,), in_specs=in_specs, out_specs=out_specs,
                          out_shape=out_shape, compiler_params=_params(1), name="in_proj")(
        x2, g, wstd, wtrn, c128, s128, ct, st, qg, kg, gmat, pmat)


def _conv_kernel(u_ref, dw_ref, lg_ref, lb_ref, pw_ref, out_ref, hp_ref):
    ts = u_ref.shape[0]

    @pl.when(pl.program_id(1) == 0)
    def _():
        hp_ref[0:CONV_HALO, :] = jnp.zeros((CONV_HALO, D_CONV), F32)

    u = u_ref[...]
    hp_ref[CONV_HALO:CONV_HALO + ts, :] = u[:, :D_CONV] * jax.nn.sigmoid(u[:, D_CONV:])
    base = CONV_HALO - (CONV_WIDTH - 1)
    for r0 in range(0, ts, CONV_ROWS):
        acc = dw_ref[0:1, :] * hp_ref[pl.ds(r0 + base, CONV_ROWS), :]
        for k in range(1, CONV_WIDTH):
            acc = acc + dw_ref[k:k + 1, :] * hp_ref[pl.ds(r0 + base + k, CONV_ROWS), :]
        mu = jnp.mean(acc, axis=-1, keepdims=True)
        xc = acc - mu
        y = xc * lax.rsqrt(jnp.mean(xc * xc, axis=-1, keepdims=True) + EPS)
        y = _silu(y * lg_ref[...] + lb_ref[...])
        out_ref[pl.ds(r0, CONV_ROWS), :] = _mm(y.astype(BF16), pw_ref[...])
    hp_ref[0:CONV_HALO, :] = hp_ref[ts:ts + CONV_HALO, :]


def _conformer_conv(uconv, dw, lg, lb, pw, batch, seq):
    ts = SEQ_TILE
    ns = seq // ts
    return pl.pallas_call(
        _conv_kernel, grid=(batch, ns),
        in_specs=[pl.BlockSpec((ts, 512), lambda b, s: (b * ns + s, 0)), _full((32, D_CONV)), _full((1, D_CONV)),
                  _full((1, D_CONV)), _full((D_CONV, D_CONV))],
        out_specs=pl.BlockSpec((ts, D_CONV), lambda b, s: (b * ns + s, 0)),
        out_shape=jax.ShapeDtypeStruct((batch * seq, D_CONV), F32),
        scratch_shapes=[pltpu.VMEM((CONV_HALO + ts, D_CONV), F32)],
        compiler_params=_params(2), name="conformer_conv")(uconv, dw, lg, lb, pw)


def _ssd_kernel(z_ref, xbc_ref, small_ref, cw_ref, cb_ref, dtb_ref, alog_ref, dskip_ref, ng_ref,
                tri_ref, rsel_ref, rep_ref, sel8_ref, out_ref, xp_ref, state_ref):
    ts = z_ref.shape[0]
    L = SSD_CHUNK

    @pl.when(pl.program_id(1) == 0)
    def _():
        xp_ref[0:SSD_HALO, :] = jnp.zeros((SSD_HALO, D_XBC), F32)
        state_ref[...] = jnp.zeros(state_ref.shape, F32)

    xp_ref[SSD_HALO:SSD_HALO + ts, :] = xbc_ref[...]
    a_neg = -jnp.exp(alog_ref[...])
    ii = lax.broadcasted_iota(I32, (L, L), 0)
    jj = lax.broadcasted_iota(I32, (L, L), 1)
    causal = ii >= jj
    first_half = jj < SSD_HEAD_DIM
    gw = D_SSD // SSD_GROUPS

    for c in range(ts // L):
        r0 = c * L
        base = r0 + SSD_HALO - (SSD_CONV - 1)
        conv = cb_ref[...] + cw_ref[0:1, :] * xp_ref[pl.ds(base, L), :]
        for k in range(1, SSD_CONV):
            conv = conv + cw_ref[k:k + 1, :] * xp_ref[pl.ds(base + k, L), :]
        xa = _silu(conv)
        xs = xa[:, 0:D_SSD]
        bm = xa[:, D_SSD:D_SSD + 256]
        cm = xa[:, D_SSD + 256:D_SSD + 512]

        dt_arg = _mm_f32_lhs(small_ref[pl.ds(r0, L), :], rsel_ref[...]) + dtb_ref[...]
        dt = jnp.maximum(dt_arg, 0.0) + jnp.log1p(jnp.exp(-jnp.abs(dt_arg)))
        acs = _mm_f32_rhs(tri_ref[...], dt * a_neg)
        acs_wide = _mm_f32_lhs(acs, rep_ref[...])
        acs_t = _mm_nt_f32_rhs(sel8_ref[...], acs)
        last = acs[L - 1:L, :]
        decay_end = jnp.exp(last - acs)
        decay_in = jnp.exp(acs)
        xd = xs * dt
        xdb = xd.astype(BF16)

        y_parts = []
        for g in range(SSD_GROUPS):
            bg = bm[:, g * SSD_STATE:(g + 1) * SSD_STATE]
            cgb = cm[:, g * SSD_STATE:(g + 1) * SSD_STATE].astype(BF16)
            cb = _mm_nt(cgb, bg.astype(BF16))
            pair_out = []
            for pr in range(2):
                h0 = g * 4 + pr * 2
                xpair = xdb[:, h0 * 64:(h0 + 2) * 64]
                outs = []
                for h in (h0, h0 + 1):
                    seg = acs_wide[:, h * L:(h + 1) * L] - acs_t[h:h + 1, :]
                    m_h = (cb * jnp.exp(jnp.where(causal, seg, NEG_INF))).astype(BF16)
                    outs.append(_mm(m_h, xpair))
                pair_out.append(jnp.where(first_half, outs[0], outs[1]))
            y_diag = jnp.concatenate(pair_out, axis=1)
            gs = slice(g * gw, (g + 1) * gw)
            st = state_ref[:, gs]
            y_off = _mm(cgb, st.astype(BF16)) * decay_in[:, gs]
            contrib = _mm(bg.T.astype(BF16), (xd[:, gs] * decay_end[:, gs]).astype(BF16))
            state_ref[:, gs] = jnp.exp(last[:, gs]) * st + contrib
            y_parts.append(y_diag + y_off)
        y = jnp.concatenate(y_parts, axis=1) + xs * dskip_ref[...]
        y = y * _silu(z_ref[pl.ds(r0, L), :])
        outs = []
        for g in range(SSD_GROUPS):
            gs = slice(g * gw, (g + 1) * gw)
            outs.append(_rms_rows(y[:, gs], ng_ref[:, gs]))
        out_ref[pl.ds(r0, L), :] = jnp.concatenate(outs, axis=1)
    xp_ref[0:SSD_HALO, :] = xp_ref[ts:ts + SSD_HALO, :]


def _ssd(z, xbc, small, cw, cb, dtb, alog, dskip, ng, consts, batch, seq):
    ts = SEQ_TILE
    ns = seq // ts
    tri, rsel, rep, sel8 = consts
    row = lambda w: pl.BlockSpec((ts, w), lambda b, s: (b * ns + s, 0))
    return pl.pallas_call(
        _ssd_kernel, grid=(batch, ns),
        in_specs=[row(512), row(1024), row(LANES), _full((8, D_XBC)), _full((1, D_XBC)), _full((1, 512)),
                  _full((1, 512)), _full((1, 512)), _full((1, 512)), _full(tri.shape), _full(rsel.shape),
                  _full(rep.shape), _full(sel8.shape)],
        out_specs=row(512),
        out_shape=jax.ShapeDtypeStruct((batch * seq, 512), F32),
        scratch_shapes=[pltpu.VMEM((SSD_HALO + ts, D_XBC), F32), pltpu.VMEM((SSD_STATE, D_SSD), F32)],
        compiler_params=_params(2), name="ssd_mixer")(z, xbc, small, cw, cb, dtb, alog, dskip, ng, tri, rsel, rep, sel8)


def _dsa_kernel(qit_ref, qnt_ref, wt_ref, kib_ref, kn_ref, vt_ref, tril_ref, out_ref, keys_ref, s_ref, acc_ref,
                *, ksel):
    qb = Q_TILE
    kb = KEY_TILE
    i = pl.program_id(1)
    nsb = (i * qb + qb + kb - 1) // kb
    qpos = i * qb + lax.broadcasted_iota(I32, (kb, qb), 1)
    krow = lax.broadcasted_iota(I32, (kb, qb), 0)

    rhs_idx = jnp.concatenate([qit_ref[h * IDX_DIM:(h + 1) * IDX_DIM, :] for h in range(IDX_HEADS)], axis=1)
    rhs_idx = jnp.concatenate([rhs_idx, jnp.zeros((LANES - IDX_DIM, IDX_HEADS * qb), BF16)], axis=0)
    wt = wt_ref[...]

    def score_tile(j, carry):
        r = pl.multiple_of(j * kb, kb)
        lg = _mm(kib_ref[pl.ds(r, kb), :], rhs_idx)
        sc = wt[0:1, :] * jnp.maximum(lg[:, 0:qb], 0.0)
        for h in range(1, IDX_HEADS):
            sc = sc + wt[h:h + 1, :] * jnp.maximum(lg[:, h * qb:(h + 1) * qb], 0.0)
        bits = pltpu.bitcast(sc, I32)
        key = bits ^ (lax.shift_right_arithmetic(bits, 31) & 0x7FFFFFFF)
        keys_ref[pl.ds(r, kb), :] = jnp.where(j * kb + krow <= qpos, key, INT_MIN)
        return carry

    lax.fori_loop(0, nsb, score_tile, 0)

    def count(cand, strict):
        def body(j, acc):
            kt = keys_ref[pl.ds(pl.multiple_of(j * kb, kb), kb), :]
            hit = (kt > cand) if strict else (kt >= cand)
            return acc + jnp.sum(jnp.where(hit, 1.0, 0.0).reshape(kb // SUBLANES, SUBLANES, qb), axis=0)
        acc = lax.fori_loop(0, nsb, body, jnp.zeros((SUBLANES, qb), F32))
        return jnp.sum(acc, axis=0, keepdims=True)

    def bit_step(it, tu):
        cand_u = tu | lax.shift_left(jnp.int32(1), 31 - it)
        cnt = count(cand_u ^ INT_MIN, False)
        return jnp.where(cnt >= float(ksel), cand_u, tu)

    tau = lax.fori_loop(0, 32, bit_step, jnp.zeros((1, qb), I32)) ^ INT_MIN
    need = jnp.where(tau == INT_MIN, 0.0, float(ksel) - count(tau, True))

    zblk = jnp.zeros((ATT_HEAD_DIM, qb), BF16)
    rhs_att = jnp.concatenate(
        [jnp.concatenate([qnt_ref[h * 64:(h + 1) * 64, :] if hh == h else zblk for hh in range(ATT_HEADS)], axis=1)
         for h in range(ATT_HEADS)], axis=0)

    def pass1(j, carry):
        m, eq_before = carry
        r = pl.multiple_of(j * kb, kb)
        s = _mm(kn_ref[pl.ds(r, kb), :], rhs_att)
        kt = keys_ref[pl.ds(r, kb), :]
        eq = kt == tau
        eqf = jnp.where(eq, 1.0, 0.0)
        rank = eq_before + _mm(tril_ref[...], eqf.astype(BF16))
        sel = (kt > tau) | (eq & (rank < need))
        s = jnp.concatenate([jnp.where(sel, s[:, h * qb:(h + 1) * qb], NEG_INF) for h in range(ATT_HEADS)], axis=1)
        s_ref[pl.ds(r, kb), :] = s
        return (jnp.maximum(m, jnp.max(s, axis=0, keepdims=True)),
                eq_before + jnp.sum(eqf, axis=0, keepdims=True))

    m, _ = lax.fori_loop(0, nsb, pass1, (jnp.full((1, ATT_HEADS * qb), NEG_INF, F32), jnp.zeros((1, qb), F32)))

    acc_ref[...] = jnp.zeros(acc_ref.shape, F32)

    def pass2(j, l):
        p = jnp.exp(s_ref[pl.ds(pl.multiple_of(j * kb, kb), kb), :] - m)
        acc_ref[...] += _mm(vt_ref[j], p.astype(BF16))
        return l + jnp.sum(p, axis=0, keepdims=True)

    l = lax.fori_loop(0, nsb, pass2, jnp.zeros((1, ATT_HEADS * qb), F32))
    o_t = jnp.concatenate(
        [acc_ref[h * 64:(h + 1) * 64, h * qb:(h + 1) * qb] / l[:, h * qb:(h + 1) * qb] for h in range(ATT_HEADS)],
        axis=0)
    out_ref[...] = o_t.T


def _dsa(qit, qnt, wt, kib, kn, vt, tril, batch, seq):
    qb = Q_TILE
    nq = seq // qb
    nkb = seq // KEY_TILE
    ksel = min(TOPK_MAX, seq // 4)
    colq = lambda r: pl.BlockSpec((r, qb), lambda b, i: (0, b * nq + i))
    return pl.pallas_call(
        functools.partial(_dsa_kernel, ksel=ksel), grid=(batch, nq),
        in_specs=[colq(512), colq(256), colq(8),
                  pl.BlockSpec((seq, LANES), lambda b, i: (b, 0)), pl.BlockSpec((seq, 256), lambda b, i: (b, 0)),
                  pl.BlockSpec((nkb, 256, KEY_TILE), lambda b, i: (b, 0, 0)), _full((KEY_TILE, KEY_TILE))],
        out_specs=pl.BlockSpec((qb, D_ATT), lambda b, i: (b * nq + i, 0)),
        out_shape=jax.ShapeDtypeStruct((batch * seq, D_ATT), F32),
        scratch_shapes=[pltpu.VMEM((seq, qb), I32), pltpu.VMEM((seq, ATT_HEADS * qb), F32),
                        pltpu.VMEM((D_ATT, ATT_HEADS * qb), F32)],
        compiler_params=_params(2), name="dsa_attention")(qit, qnt, wt, kib, kn, vt, tril)


def _mem_prep_kernel(mem_ref, g_ref, wkv_ref, kg_ref, gmat_ref, kbd_ref, vbd_ref):
    nm = mem_ref.shape[0]
    mb = _rms_rows(mem_ref[...], g_ref[...]).astype(BF16)
    kv = _mm(mb, wkv_ref[...])
    k = kv[:, 0:D_MEMATT]
    kn = (k * lax.rsqrt(_mm_f32_lhs(k * k, gmat_ref[...]) + EPS)) * kg_ref[...]
    k_t = kn.T.astype(BF16)
    zero = jnp.zeros((MEM_HEAD_DIM, nm), BF16)
    kbd_ref[...] = jnp.concatenate(
        [jnp.concatenate([k_t[h * 64:(h + 1) * 64, :] if hh == h else zero for hh in range(MEM_HEADS)], axis=1)
         for h in range(MEM_HEADS)], axis=0)
    v = kv[:, D_MEMATT:]
    lane = lax.broadcasted_iota(I32, (nm, D_MEMATT), 1)
    vbd_ref[...] = jnp.concatenate(
        [jnp.where((lane >= h * 64) & (lane < (h + 1) * 64), v, 0.0) for h in range(MEM_HEADS)],
        axis=0).astype(BF16)


def _mem_prep(mem2, g, wkv, kg, gmat, batch, nm):
    return pl.pallas_call(
        _mem_prep_kernel, grid=(batch,),
        in_specs=[pl.BlockSpec((nm, D_MODEL), lambda b: (b, 0)), _full((1, D_MODEL)), _full((D_MODEL, 512)),
                  _full((1, 256)), _full((256, 256))],
        out_specs=(pl.BlockSpec((None, D_MEMATT, MEM_HEADS * nm), lambda b: (b, 0, 0)),
                   pl.BlockSpec((None, MEM_HEADS * nm, D_MEMATT), lambda b: (b, 0, 0))),
        out_shape=(jax.ShapeDtypeStruct((batch, D_MEMATT, MEM_HEADS * nm), BF16),
                   jax.ShapeDtypeStruct((batch, MEM_HEADS * nm, D_MEMATT), BF16)),
        compiler_params=_params(1), name="mem_prep")(mem2, g, wkv, kg, gmat)


def _mix_memx_kernel(x_ref, yc_ref, ys_ref, ya_ref, wout_ref, g_ref, wq_ref, qg_ref, gmat_ref, kbd_ref, vbd_ref,
                     wo_ref, out_ref):
    nm = kbd_ref.shape[1] // MEM_HEADS
    y_mix = jnp.concatenate([yc_ref[...], ys_ref[...], ya_ref[...]], axis=1).astype(BF16)
    x1 = x_ref[...] + _mm(y_mix, wout_ref[...])
    hb = _rms_rows(x1, g_ref[...]).astype(BF16)
    q = _mm(hb, wq_ref[...])
    qn = (q * lax.rsqrt(_mm_f32_lhs(q * q, gmat_ref[...]) + EPS)) * qg_ref[...]
    s = _mm(qn.astype(BF16), kbd_ref[...]) * (MEM_HEAD_DIM ** -0.5)
    ps = []
    for h in range(MEM_HEADS):
        sh = s[:, h * nm:(h + 1) * nm]
        e = jnp.exp(sh - jnp.max(sh, axis=-1, keepdims=True))
        ps.append(e / jnp.sum(e, axis=-1, keepdims=True))
    o = _mm(jnp.concatenate(ps, axis=1).astype(BF16), vbd_ref[...])
    out_ref[...] = x1 + _mm(o.astype(BF16), wo_ref[...])


def _mix_memx(x2, yc, ys, ya, wout, g, wq, qg, gmat, kbd, vbd, wo, seq):
    t = x2.shape[0]
    tm = TOKEN_TILE
    per_b = seq // tm
    nm4 = kbd.shape[2]
    row = lambda w: pl.BlockSpec((tm, w), lambda i: (i, 0))
    return pl.pallas_call(
        _mix_memx_kernel, grid=(t // tm,),
        in_specs=[row(D_MODEL), row(256), row(512), row(256), _full((D_MODEL, D_MODEL)), _full((1, D_MODEL)),
                  _full((D_MODEL, 256)), _full((1, 256)), _full((256, 256)),
                  pl.BlockSpec((None, D_MEMATT, nm4), lambda i: (i // per_b, 0, 0)),
                  pl.BlockSpec((None, nm4, D_MEMATT), lambda i: (i // per_b, 0, 0)), _full((256, D_MODEL))],
        out_specs=row(D_MODEL), out_shape=jax.ShapeDtypeStruct((t, D_MODEL), F32),
        compiler_params=_params(1), name="mix_memx")(x2, yc, ys, ya, wout, g, wq, qg, gmat, kbd, vbd, wo)


def _ffn_kernel(x_ref, g_ref, wg_ref, wu_ref, wd_ref, out_ref, hb_ref, acc_ref):
    c = pl.program_id(1)

    @pl.when(c == 0)
    def _():
        hb_ref[...] = _rms_rows(x_ref[...], g_ref[...]).astype(BF16)
        acc_ref[...] = jnp.zeros(acc_ref.shape, F32)

    hb = hb_ref[...]
    a = _silu(_mm(hb, wg_ref[...])) * _mm(hb, wu_ref[...])
    acc_ref[...] += _mm(a.astype(BF16), wd_ref[...])

    @pl.when(c == pl.num_programs(1) - 1)
    def _():
        out_ref[...] = x_ref[...] + acc_ref[...]


def _ffn(x2, g, wg, wu, wd):
    t = x2.shape[0]
    tm = TOKEN_TILE
    fc = FF_CHUNK
    return pl.pallas_call(
        _ffn_kernel, grid=(t // tm, D_FF // fc),
        in_specs=[pl.BlockSpec((tm, D_MODEL), lambda i, c: (i, 0)), _full((1, D_MODEL)),
                  pl.BlockSpec((D_MODEL, fc), lambda i, c: (0, c)), pl.BlockSpec((D_MODEL, fc), lambda i, c: (0, c)),
                  pl.BlockSpec((fc, D_MODEL), lambda i, c: (c, 0))],
        out_specs=pl.BlockSpec((tm, D_MODEL), lambda i, c: (i, 0)),
        out_shape=jax.ShapeDtypeStruct((t, D_MODEL), F32),
        scratch_shapes=[pltpu.VMEM((tm, D_MODEL), BF16), pltpu.VMEM((tm, D_MODEL), F32)],
        compiler_params=_params(2), name="ffn_dense")(x2, g, wg, wu, wd)


def _moe_kernel(x_ref, g_ref, wr_ref, wg_ref, wu_ref, wd_ref, out_ref, hb_ref, gates_ref, ye_ref, acc_ref):
    e = pl.program_id(1)
    c = pl.program_id(2)
    last_c = pl.num_programs(2) - 1
    tm = x_ref.shape[0]
    lane = lax.broadcasted_iota(I32, (tm, LANES), 1)

    @pl.when((e == 0) & (c == 0))
    def _():
        h = _rms_rows(x_ref[...], g_ref[...])
        hb_ref[...] = h.astype(BF16)
        logits = jnp.where(lane < N_EXPERTS, _mm_f32_f32(h, wr_ref[...]), NEG_INF)
        v1 = jnp.max(logits, axis=-1, keepdims=True)
        i1 = jnp.min(jnp.where(logits == v1, lane, LANES), axis=-1, keepdims=True)
        rest = jnp.where(lane == i1, NEG_INF, logits)
        v2 = jnp.max(rest, axis=-1, keepdims=True)
        i2 = jnp.min(jnp.where(rest == v2, lane, LANES), axis=-1, keepdims=True)
        e2 = jnp.exp(v2 - v1)
        den = 1.0 + e2
        gates_ref[...] = jnp.where(lane == i1, 1.0 / den, 0.0) + jnp.where(lane == i2, e2 / den, 0.0)
        acc_ref[...] = jnp.zeros(acc_ref.shape, F32)

    @pl.when(c == 0)
    def _():
        ye_ref[...] = jnp.zeros(ye_ref.shape, F32)

    hb = hb_ref[...]
    a = _silu(_mm(hb, wg_ref[...])) * _mm(hb, wu_ref[...])
    ye_ref[...] += _mm(a.astype(BF16), wd_ref[...])

    @pl.when(c == last_c)
    def _():
        ge = jnp.sum(jnp.where(lane == e, gates_ref[...], 0.0), axis=-1, keepdims=True)
        acc_ref[...] += ye_ref[...] * ge

    @pl.when((e == N_EXPERTS - 1) & (c == last_c))
    def _():
        out_ref[...] = x_ref[...] + acc_ref[...]


def _moe(x2, g, wr, wg, wu, wd):
    t = x2.shape[0]
    tm = TOKEN_TILE
    fc = FF_CHUNK
    return pl.pallas_call(
        _moe_kernel, grid=(t // tm, N_EXPERTS, D_FF // fc),
        in_specs=[pl.BlockSpec((tm, D_MODEL), lambda i, e, c: (i, 0)), _full((1, D_MODEL)), _full((D_MODEL, LANES)),
                  pl.BlockSpec((None, D_MODEL, fc), lambda i, e, c: (e, 0, c)),
                  pl.BlockSpec((None, D_MODEL, fc), lambda i, e, c: (e, 0, c)),
                  pl.BlockSpec((None, fc, D_MODEL), lambda i, e, c: (e, c, 0))],
        out_specs=pl.BlockSpec((tm, D_MODEL), lambda i, e, c: (i, 0)),
        out_shape=jax.ShapeDtypeStruct((t, D_MODEL), F32),
        scratch_shapes=[pltpu.VMEM((tm, D_MODEL), BF16), pltpu.VMEM((tm, LANES), F32),
                        pltpu.VMEM((tm, D_MODEL), F32), pltpu.VMEM((tm, D_MODEL), F32)],
        compiler_params=_params(3), name="moe_dense")(x2, g, wr, wg, wu, wd)


def _const_mats():
    r = np.arange(256)
    gmat = ((r[:, None] // 64) == (r[None, :] // 64)).astype(np.float32) / 64.0
    d = r % 64
    same = (r[:, None] // 64) == (r[None, :] // 64)
    swap = ((d[None, :] < 8) & (d[:, None] == d[None, :] + 8)) | \
           ((d[None, :] >= 8) & (d[None, :] < 16) & (d[:, None] == d[None, :] - 8))
    pmat = (same & swap).astype(np.float32)
    tri = np.tril(np.ones((SSD_CHUNK, SSD_CHUNK), np.float32))
    rsel = np.zeros((LANES, D_SSD), np.float32)
    rep = np.zeros((D_SSD, SSD_HEADS * SSD_CHUNK), np.float32)
    sel8 = np.zeros((SSD_HEADS, D_SSD), np.float32)
    for h in range(SSD_HEADS):
        rsel[IDX_DIM + h, h * 64:(h + 1) * 64] = 1.0
        rep[h * 64, h * SSD_CHUNK:(h + 1) * SSD_CHUNK] = 1.0
        sel8[h, h * 64] = 1.0
    tril = np.tril(np.ones((KEY_TILE, KEY_TILE), np.float32), -1)
    to = lambda a: jnp.asarray(a, dtype=BF16)
    return to(gmat), to(pmat), (to(tri), to(rsel), to(rep), to(sel8)), to(tril)


def kernel(x, mem, positions, mix_norm_g, w_in, conv_dw_w, conv_ln_g, conv_ln_b, conv_pw_w, ssd_conv_w, ssd_conv_b, ssd_dt_bias, ssd_a_log, ssd_d, ssd_norm_g, att_q_norm_g, att_k_norm_g, w_out, memx_norm_g, mem_norm_g, memx_w_q, memx_w_kv, memx_q_norm_g, memx_k_norm_g, memx_w_o, ffn_norm_g, ffn_w_gate, ffn_w_up, ffn_w_down, moe_w_router, moe_w_gate, moe_w_up, moe_w_down):
    batch, seq, d = x.shape
    nm = mem.shape[1]
    depth = w_in.shape[0]
    t = batch * seq
    assert d == D_MODEL and seq % TOKEN_TILE == 0 and seq % SEQ_TILE == 0 and seq % KEY_TILE == 0
    assert TOKEN_TILE % KEY_TILE == 0 and KEY_TILE % Q_TILE == 0 and D_FF % FF_CHUNK == 0

    gmat, pmat, ssd_consts, tril = _const_mats()

    inv = ROPE_THETA ** (-jnp.arange(0, ROT_DIM, 2, dtype=F32) / ROT_DIM)
    ang = positions.astype(F32).reshape(t, 1) * inv
    cos, sin = jnp.cos(ang), jnp.sin(ang)
    c64 = jnp.concatenate([cos, cos, jnp.ones((t, 48), F32)], axis=1)
    s64 = jnp.concatenate([-sin, sin, jnp.zeros((t, 48), F32)], axis=1)
    c128 = jnp.concatenate([c64, c64], axis=1)
    s128 = jnp.concatenate([s64, s64], axis=1)
    ct, st = cos.T, sin.T

    x2 = x.reshape(t, d)
    mem2 = mem.reshape(batch * nm, d)
    row = lambda v: v.reshape(1, -1)
    rep64 = lambda v: jnp.repeat(v, SSD_HEAD_DIM).reshape(1, -1)
    o = _OFF

    for i in range(depth):
        w = w_in[i]
        wstd = jnp.concatenate([w[:, o[0]:o[3]], w[:, o[5]:o[6]], w[:, o[8]:o[9]], w[:, o[3]:o[4]],
                                jnp.zeros((d, LANES - IDX_DIM - SSD_HEADS), F32)], axis=1).astype(BF16)
        wtrn = jnp.concatenate([w[:, o[7]:o[8]].T, w[:, o[4]:o[5]].T, w[:, o[6]:o[7]].T, w[:, o[9]:o[10]].T,
                                jnp.zeros((N_TRN - 1032, d), F32)], axis=0).astype(BF16)
        uconv, z, xbc, kn, small, kib, qit, qnt, vt, wt = _in_proj(
            x2, row(mix_norm_g[i]), wstd, wtrn, c128, s128, ct, st, att_q_norm_g[i].reshape(64, 1),
            jnp.tile(att_k_norm_g[i], ATT_HEADS).reshape(1, 256), gmat, pmat)

        dw = jnp.concatenate([conv_dw_w[i], jnp.zeros((32 - CONV_WIDTH, D_CONV), F32)], axis=0)
        y_conv = _conformer_conv(uconv, dw, row(conv_ln_g[i]), row(conv_ln_b[i]), conv_pw_w[i].astype(BF16),
                                 batch, seq)
        cw = jnp.concatenate([ssd_conv_w[i], jnp.zeros((8 - SSD_CONV, D_XBC), F32)], axis=0)
        y_ssd = _ssd(z, xbc, small, cw, row(ssd_conv_b[i]), rep64(ssd_dt_bias[i]), rep64(ssd_a_log[i]),
                     rep64(ssd_d[i]), row(ssd_norm_g[i]), ssd_consts, batch, seq)
        y_att = _dsa(qit, qnt, wt, kib, kn, vt, tril, batch, seq)

        kbd, vbd = _mem_prep(mem2, row(mem_norm_g[i]), memx_w_kv[i].astype(BF16),
                             jnp.tile(memx_k_norm_g[i], MEM_HEADS).reshape(1, 256), gmat, batch, nm)
        x2 = _mix_memx(x2, y_conv, y_ssd, y_att, w_out[i].astype(BF16), row(memx_norm_g[i]),
                       memx_w_q[i].astype(BF16), jnp.tile(memx_q_norm_g[i], MEM_HEADS).reshape(1, 256), gmat,
                       kbd, vbd, memx_w_o[i].astype(BF16), seq)

        j = i // 2
        if i % 2 == 0:
            x2 = _ffn(x2, row(ffn_norm_g[i]), ffn_w_gate[j].astype(BF16), ffn_w_up[j].astype(BF16),
                      ffn_w_down[j].astype(BF16))
        else:
            wr = jnp.concatenate([moe_w_router[j], jnp.zeros((d, LANES - N_EXPERTS), F32)], axis=1)
            x2 = _moe(x2, row(ffn_norm_g[i]), wr, moe_w_gate[j].astype(BF16), moe_w_up[j].astype(BF16),
                      moe_w_down[j].astype(BF16))
    return x2.reshape(batch, seq, d)
```

```python
import functools
import math

import numpy as np
import jax
import jax.numpy as jnp
from jax import lax
from jax.experimental import pallas as pl
from jax.experimental.pallas import tpu as pltpu

F32 = jnp.float32
BF16 = jnp.bfloat16
I32 = jnp.int32

EPS = 1e-6
D_MODEL = 1024
D_CONV = 256
CONV_WIDTH = 31
SSD_HEAD_DIM = 64
D_SSD = 512
SSD_HEADS = 8
SSD_GROUPS = 2
SSD_STATE = 128
SSD_CONV = 4
SSD_CHUNK = 128
D_XBC = D_SSD + 2 * SSD_GROUPS * SSD_STATE
ATT_HEAD_DIM = 64
D_ATT = 256
ATT_HEADS = 4
IDX_HEADS = 8
IDX_DIM = 64
TOPK_MAX = 256
ROPE_THETA = 500000.0
ROT_DIM = ATT_HEAD_DIM // 4
MEM_HEADS = 4
MEM_HEAD_DIM = 64
D_MEMATT = 256
D_FF = 2816
N_EXPERTS = 8

LANES = 128
SUBLANES = 8
VMEM_LIMIT_BYTES = 56 * 1024 * 1024

TOKEN_TILE = 512
SEQ_TILE = 512
Q_TILE = 128
KEY_TILE = 256
FF_CHUNK = 1408
MOE_SLOT_BLOCK = 512
MOE_DISPATCH_TILE = 512
MOE_COMBINE_TILE = 256
CONV_ROWS = 64
CONV_HALO = 32
SSD_HALO = 8
INT_MIN = -2 ** 31
NEG_INF = float("-inf")

_SZ = [2 * D_CONV, D_SSD, D_XBC, SSD_HEADS, D_ATT, D_ATT, D_ATT, IDX_HEADS * IDX_DIM, IDX_DIM, IDX_HEADS]
_OFF = np.concatenate([[0], np.cumsum(_SZ)]).tolist()
N_STD = 2 * D_CONV + D_SSD + D_XBC + D_ATT + LANES
N_TRN = IDX_HEADS * IDX_DIM + D_ATT + D_ATT + 16


def _mm(a, b):
    return jnp.dot(a, b, preferred_element_type=F32)


def _mm_nt(a, b):
    return lax.dot_general(a, b, (((1,), (1,)), ((), ())), preferred_element_type=F32)


def _split3(a):
    hi = a.astype(BF16)
    r = a - hi.astype(F32)
    mid = r.astype(BF16)
    lo = (r - mid.astype(F32)).astype(BF16)
    return hi, mid, lo


def _mm_f32_lhs(a, b_exact):
    hi, mid, lo = _split3(a)
    return (_mm(hi, b_exact) + _mm(mid, b_exact)) + _mm(lo, b_exact)


def _mm_f32_rhs(a_exact, b):
    hi, mid, lo = _split3(b)
    return (_mm(a_exact, hi) + _mm(a_exact, mid)) + _mm(a_exact, lo)


def _mm_nt_f32_rhs(a_exact, b):
    hi, mid, lo = _split3(b)
    return (_mm_nt(a_exact, hi) + _mm_nt(a_exact, mid)) + _mm_nt(a_exact, lo)


def _mm_f32_f32(a, b):
    a1, a2, a3 = _split3(a)
    b1, b2, b3 = _split3(b)
    return (_mm(a1, b1) + (_mm(a1, b2) + _mm(a2, b1))) + ((_mm(a1, b3) + _mm(a2, b2)) + _mm(a3, b1))


def _silu(x):
    return x * jax.nn.sigmoid(x)


def _rms_rows(x, g):
    ms = jnp.mean(x * x, axis=-1, keepdims=True)
    return (x * lax.rsqrt(ms + EPS)) * g


def _params(n_axes):
    return pltpu.CompilerParams(dimension_semantics=("arbitrary",) * n_axes,
                                vmem_limit_bytes=VMEM_LIMIT_BYTES)


def _full(shape):
    n = len(shape)
    return pl.BlockSpec(shape, lambda *_: (0,) * n)


def _in_proj_kernel(x_ref, g_ref, wstd_ref, wtrn_ref, c128_ref, s128_ref, ct_ref, st_ref, qg_ref, kg_ref,
                    gmat_ref, pmat_ref,
                    uconv_ref, z_ref, xbc_ref, kn_ref, small_ref, kib_ref, qit_ref, qnt_ref, vt_ref, wt_ref):
    tm = x_ref.shape[0]
    hb = _rms_rows(x_ref[...], g_ref[...]).astype(BF16)
    uconv_ref[...] = _mm(hb, wstd_ref[:, 0:512])
    z_ref[...] = _mm(hb, wstd_ref[:, 512:1024])
    xbc_ref[...] = _mm(hb, wstd_ref[:, 1024:2048])

    c128 = c128_ref[...]
    s128 = s128_ref[...]
    kraw = _mm(hb, wstd_ref[:, 2048:2304])
    kn = (kraw * lax.rsqrt(_mm_f32_lhs(kraw * kraw, gmat_ref[...]) + EPS)) * kg_ref[...]
    kn = kn * jnp.concatenate([c128, c128], axis=1) + \
        _mm_f32_lhs(kn, pmat_ref[...]) * jnp.concatenate([s128, s128], axis=1)
    kn_ref[...] = kn.astype(BF16)

    sm = _mm(hb, wstd_ref[:, 2304:2432])
    lane = lax.broadcasted_iota(I32, (tm, LANES), 1)
    is_idx = lane < IDX_DIM
    sm = sm * jnp.where(is_idx, c128, 1.0) + _mm_f32_lhs(sm, pmat_ref[0:LANES, 0:LANES]) * jnp.where(is_idx, s128, 0.0)
    small_ref[...] = sm
    kib_ref[...] = jnp.where(is_idx, sm, 0.0).astype(BF16)

    out_t = _mm_nt(wtrn_ref[...], hb)
    ct = ct_ref[...]
    st = st_ref[...]

    def rope_t(blk):
        x1 = blk[0:8]
        x2 = blk[8:16]
        return jnp.concatenate([x1 * ct - x2 * st, x2 * ct + x1 * st, blk[16:64]], axis=0)

    qi = [rope_t(out_t[h * 64:(h + 1) * 64]) for h in range(IDX_HEADS)]
    qit_ref[...] = jnp.concatenate(qi, axis=0).astype(BF16)

    qn = []
    for h in range(ATT_HEADS):
        blk = out_t[512 + h * 64:512 + (h + 1) * 64]
        ms = jnp.mean(blk * blk, axis=0, keepdims=True)
        blk = (blk * lax.rsqrt(ms + EPS)) * qg_ref[...]
        qn.append(rope_t(blk) * (ATT_HEAD_DIM ** -0.5))
    qnt_ref[...] = jnp.concatenate(qn, axis=0).astype(BF16)

    vt = out_t[768:1024].astype(BF16)
    for c in range(tm // KEY_TILE):
        vt_ref[c] = vt[:, c * KEY_TILE:(c + 1) * KEY_TILE]
    wt_ref[...] = out_t[1024:1032] * (IDX_HEADS ** -0.5 * IDX_DIM ** -0.5)


def _in_proj(x2, g, wstd, wtrn, c128, s128, ct, st, qg, kg, gmat, pmat):
    t = x2.shape[0]
    tm = TOKEN_TILE
    row = lambda w: pl.BlockSpec((tm, w), lambda i: (i, 0))
    col = lambda r: pl.BlockSpec((r, tm), lambda i: (0, i))
    out_shape = (
        jax.ShapeDtypeStruct((t, 512), F32), jax.ShapeDtypeStruct((t, 512), F32),
        jax.ShapeDtypeStruct((t, 1024), F32), jax.ShapeDtypeStruct((t, 256), BF16),
        jax.ShapeDtypeStruct((t, LANES), F32), jax.ShapeDtypeStruct((t, LANES), BF16),
        jax.ShapeDtypeStruct((512, t), BF16), jax.ShapeDtypeStruct((256, t), BF16),
        jax.ShapeDtypeStruct((t // KEY_TILE, 256, KEY_TILE), BF16), jax.ShapeDtypeStruct((8, t), F32))
    out_specs = (row(512), row(512), row(1024), row(256), row(LANES), row(LANES), col(512), col(256),
                 pl.BlockSpec((tm // KEY_TILE, 256, KEY_TILE), lambda i: (i, 0, 0)), col(8))
    in_specs = [row(D_MODEL), _full((1, D_MODEL)), _full(wstd.shape), _full(wtrn.shape), row(LANES), row(LANES),
                col(8), col(8), _full((64, 1)), _full((1, 256)), _full((256, 256)), _full((256, 256))]
    return pl.pallas_call(_in_proj_kernel, grid=(t // tm,), in_specs=in_specs, out_specs=out_specs,
                          out_shape=out_shape, compiler_params=_params(1), name="in_proj")(
        x2, g, wstd, wtrn, c128, s128, ct, st, qg, kg, gmat, pmat)


def _conv_kernel(u_ref, dw_ref, lg_ref, lb_ref, pw_ref, out_ref, hp_ref):
    ts = u_ref.shape[0]

    @pl.when(pl.program_id(1) == 0)
    def _():
        hp_ref[0:CONV_HALO, :] = jnp.zeros((CONV_HALO, D_CONV), F32)

    u = u_ref[...]
    hp_ref[CONV_HALO:CONV_HALO + ts, :] = u[:, :D_CONV] * jax.nn.sigmoid(u[:, D_CONV:])
    base = CONV_HALO - (CONV_WIDTH - 1)
    for r0 in range(0, ts, CONV_ROWS):
        acc = dw_ref[0:1, :] * hp_ref[pl.ds(r0 + base, CONV_ROWS), :]
        for k in range(1, CONV_WIDTH):
            acc = acc + dw_ref[k:k + 1, :] * hp_ref[pl.ds(r0 + base + k, CONV_ROWS), :]
        mu = jnp.mean(acc, axis=-1, keepdims=True)
        xc = acc - mu
        y = xc * lax.rsqrt(jnp.mean(xc * xc, axis=-1, keepdims=True) + EPS)
        y = _silu(y * lg_ref[...] + lb_ref[...])
        out_ref[pl.ds(r0, CONV_ROWS), :] = _mm(y.astype(BF16), pw_ref[...])
    hp_ref[0:CONV_HALO, :] = hp_ref[ts:ts + CONV_HALO, :]


def _conformer_conv(uconv, dw, lg, lb, pw, batch, seq):
    ts = SEQ_TILE
    ns = seq // ts
    return pl.pallas_call(
        _conv_kernel, grid=(batch, ns),
        in_specs=[pl.BlockSpec((ts, 512), lambda b, s: (b * ns + s, 0)), _full((32, D_CONV)), _full((1, D_CONV)),
                  _full((1, D_CONV)), _full((D_CONV, D_CONV))],
        out_specs=pl.BlockSpec((ts, D_CONV), lambda b, s: (b * ns + s, 0)),
        out_shape=jax.ShapeDtypeStruct((batch * seq, D_CONV), F32),
        scratch_shapes=[pltpu.VMEM((CONV_HALO + ts, D_CONV), F32)],
        compiler_params=_params(2), name="conformer_conv")(uconv, dw, lg, lb, pw)


def _ssd_kernel(z_ref, xbc_ref, small_ref, cw_ref, cb_ref, dtb_ref, alog_ref, dskip_ref, ng_ref,
                tri_ref, rsel_ref, rep_ref, sel8_ref, out_ref, xp_ref, state_ref):
    ts = z_ref.shape[0]
    L = SSD_CHUNK

    @pl.when(pl.program_id(1) == 0)
    def _():
        xp_ref[0:SSD_HALO, :] = jnp.zeros((SSD_HALO, D_XBC), F32)
        state_ref[...] = jnp.zeros(state_ref.shape, F32)

    xp_ref[SSD_HALO:SSD_HALO + ts, :] = xbc_ref[...]
    a_neg = -jnp.exp(alog_ref[...])
    ii = lax.broadcasted_iota(I32, (L, L), 0)
    jj = lax.broadcasted_iota(I32, (L, L), 1)
    causal = ii >= jj
    first_half = jj < SSD_HEAD_DIM
    gw = D_SSD // SSD_GROUPS

    for c in range(ts // L):
        r0 = c * L
        base = r0 + SSD_HALO - (SSD_CONV - 1)
        conv = cb_ref[...] + cw_ref[0:1, :] * xp_ref[pl.ds(base, L), :]
        for k in range(1, SSD_CONV):
            conv = conv + cw_ref[k:k + 1, :] * xp_ref[pl.ds(base + k, L), :]
        xa = _silu(conv)
        xs = xa[:, 0:D_SSD]
        bm = xa[:, D_SSD:D_SSD + 256]
        cm = xa[:, D_SSD + 256:D_SSD + 512]

        dt_arg = _mm_f32_lhs(small_ref[pl.ds(r0, L), :], rsel_ref[...]) + dtb_ref[...]
        dt = jnp.maximum(dt_arg, 0.0) + jnp.log1p(jnp.exp(-jnp.abs(dt_arg)))
        acs = _mm_f32_rhs(tri_ref[...], dt * a_neg)
        acs_wide = _mm_f32_lhs(acs, rep_ref[...])
        acs_t = _mm_nt_f32_rhs(sel8_ref[...], acs)
        last = acs[L - 1:L, :]
        decay_end = jnp.exp(last - acs)
        decay_in = jnp.exp(acs)
        xd = xs * dt
        xdb = xd.astype(BF16)

        y_parts = []
        for g in range(SSD_GROUPS):
            bg = bm[:, g * SSD_STATE:(g + 1) * SSD_STATE]
            cgb = cm[:, g * SSD_STATE:(g + 1) * SSD_STATE].astype(BF16)
            cb = _mm_nt(cgb, bg.astype(BF16))
            pair_out = []
            for pr in range(2):
                h0 = g * 4 + pr * 2
                xpair = xdb[:, h0 * 64:(h0 + 2) * 64]
                outs = []
                for h in (h0, h0 + 1):
                    seg = acs_wide[:, h * L:(h + 1) * L] - acs_t[h:h + 1, :]
                    m_h = (cb * jnp.exp(jnp.where(causal, seg, NEG_INF))).astype(BF16)
                    outs.append(_mm(m_h, xpair))
                pair_out.append(jnp.where(first_half, outs[0], outs[1]))
            y_diag = jnp.concatenate(pair_out, axis=1)
            gs = slice(g * gw, (g + 1) * gw)
            st = state_ref[:, gs]
            y_off = _mm(cgb, st.astype(BF16)) * decay_in[:, gs]
            contrib = _mm(bg.T.astype(BF16), (xd[:, gs] * decay_end[:, gs]).astype(BF16))
            state_ref[:, gs] = jnp.exp(last[:, gs]) * st + contrib
            y_parts.append(y_diag + y_off)
        y = jnp.concatenate(y_parts, axis=1) + xs * dskip_ref[...]
        y = y * _silu(z_ref[pl.ds(r0, L), :])
        outs = []
        for g in range(SSD_GROUPS):
            gs = slice(g * gw, (g + 1) * gw)
            outs.append(_rms_rows(y[:, gs], ng_ref[:, gs]))
        out_ref[pl.ds(r0, L), :] = jnp.concatenate(outs, axis=1)
    xp_ref[0:SSD_HALO, :] = xp_ref[ts:ts + SSD_HALO, :]


def _ssd(z, xbc, small, cw, cb, dtb, alog, dskip, ng, consts, batch, seq):
    ts = SEQ_TILE
    ns = seq // ts
    tri, rsel, rep, sel8 = consts
    row = lambda w: pl.BlockSpec((ts, w), lambda b, s: (b * ns + s, 0))
    return pl.pallas_call(
        _ssd_kernel, grid=(batch, ns),
        in_specs=[row(512), row(1024), row(LANES), _full((8, D_XBC)), _full((1, D_XBC)), _full((1, 512)),
                  _full((1, 512)), _full((1, 512)), _full((1, 512)), _full(tri.shape), _full(rsel.shape),
                  _full(rep.shape), _full(sel8.shape)],
        out_specs=row(512),
        out_shape=jax.ShapeDtypeStruct((batch * seq, 512), F32),
        scratch_shapes=[pltpu.VMEM((SSD_HALO + ts, D_XBC), F32), pltpu.VMEM((SSD_STATE, D_SSD), F32)],
        compiler_params=_params(2), name="ssd_mixer")(z, xbc, small, cw, cb, dtb, alog, dskip, ng, tri, rsel, rep, sel8)


def _dsa_kernel(qit_ref, qnt_ref, wt_ref, kib_ref, kn_ref, vt_ref, tril_ref, out_ref, keys_ref, s_ref, acc_ref,
                *, ksel):
    qb = Q_TILE
    kb = KEY_TILE
    i = pl.program_id(1)
    nsb = (i * qb + qb + kb - 1) // kb
    qpos = i * qb + lax.broadcasted_iota(I32, (kb, qb), 1)
    krow = lax.broadcasted_iota(I32, (kb, qb), 0)

    rhs_idx = jnp.concatenate([qit_ref[h * IDX_DIM:(h + 1) * IDX_DIM, :] for h in range(IDX_HEADS)], axis=1)
    rhs_idx = jnp.concatenate([rhs_idx, jnp.zeros((LANES - IDX_DIM, IDX_HEADS * qb), BF16)], axis=0)
    wt = wt_ref[...]

    def score_tile(j, carry):
        r = pl.multiple_of(j * kb, kb)
        lg = _mm(kib_ref[pl.ds(r, kb), :], rhs_idx)
        sc = wt[0:1, :] * jnp.maximum(lg[:, 0:qb], 0.0)
        for h in range(1, IDX_HEADS):
            sc = sc + wt[h:h + 1, :] * jnp.maximum(lg[:, h * qb:(h + 1) * qb], 0.0)
        bits = pltpu.bitcast(sc, I32)
        key = bits ^ (lax.shift_right_arithmetic(bits, 31) & 0x7FFFFFFF)
        keys_ref[pl.ds(r, kb), :] = jnp.where(j * kb + krow <= qpos, key, INT_MIN)
        return carry

    lax.fori_loop(0, nsb, score_tile, 0)

    def count(cand, strict):
        def body(j, acc):
            kt = keys_ref[pl.ds(pl.multiple_of(j * kb, kb), kb), :]
            hit = (kt > cand) if strict else (kt >= cand)
            return acc + jnp.sum(jnp.where(hit, 1.0, 0.0).reshape(kb // SUBLANES, SUBLANES, qb), axis=0)
        acc = lax.fori_loop(0, nsb, body, jnp.zeros((SUBLANES, qb), F32))
        return jnp.sum(acc, axis=0, keepdims=True)

    def bit_step(it, tu):
        cand_u = tu | lax.shift_left(jnp.int32(1), 31 - it)
        cnt = count(cand_u ^ INT_MIN, False)
        return jnp.where(cnt >= float(ksel), cand_u, tu)

    tau = lax.fori_loop(0, 32, bit_step, jnp.zeros((1, qb), I32)) ^ INT_MIN
    need = jnp.where(tau == INT_MIN, 0.0, float(ksel) - count(tau, True))

    zblk = jnp.zeros((ATT_HEAD_DIM, qb), BF16)
    rhs_att = jnp.concatenate(
        [jnp.concatenate([qnt_ref[h * 64:(h + 1) * 64, :] if hh == h else zblk for hh in range(ATT_HEADS)], axis=1)
         for h in range(ATT_HEADS)], axis=0)

    def pass1(j, carry):
        m, eq_before = carry
        r = pl.multiple_of(j * kb, kb)
        s = _mm(kn_ref[pl.ds(r, kb), :], rhs_att)
        kt = keys_ref[pl.ds(r, kb), :]
        eq = kt == tau
        eqf = jnp.where(eq, 1.0, 0.0)
        rank = eq_before + _mm(tril_ref[...], eqf.astype(BF16))
        sel = (kt > tau) | (eq & (rank < need))
        s = jnp.concatenate([jnp.where(sel, s[:, h * qb:(h + 1) * qb], NEG_INF) for h in range(ATT_HEADS)], axis=1)
        s_ref[pl.ds(r, kb), :] = s
        return (jnp.maximum(m, jnp.max(s, axis=0, keepdims=True)),
                eq_before + jnp.sum(eqf, axis=0, keepdims=True))

    m, _ = lax.fori_loop(0, nsb, pass1, (jnp.full((1, ATT_HEADS * qb), NEG_INF, F32), jnp.zeros((1, qb), F32)))

    acc_ref[...] = jnp.zeros(acc_ref.shape, F32)

    def pass2(j, l):
        p = jnp.exp(s_ref[pl.ds(pl.multiple_of(j * kb, kb), kb), :] - m)
        acc_ref[...] += _mm(vt_ref[j], p.astype(BF16))
        return l + jnp.sum(p, axis=0, keepdims=True)

    l = lax.fori_loop(0, nsb, pass2, jnp.zeros((1, ATT_HEADS * qb), F32))
    o_t = jnp.concatenate(
        [acc_ref[h * 64:(h + 1) * 64, h * qb:(h + 1) * qb] / l[:, h * qb:(h + 1) * qb] for h in range(ATT_HEADS)],
        axis=0)
    out_ref[...] = o_t.T


def _dsa(qit, qnt, wt, kib, kn, vt, tril, batch, seq):
    qb = Q_TILE
    nq = seq // qb
    nkb = seq // KEY_TILE
    ksel = min(TOPK_MAX, seq // 4)
    colq = lambda r: pl.BlockSpec((r, qb), lambda b, i: (0, b * nq + i))
    return pl.pallas_call(
        functools.partial(_dsa_kernel, ksel=ksel), grid=(batch, nq),
        in_specs=[colq(512), colq(256), colq(8),
                  pl.BlockSpec((seq, LANES), lambda b, i: (b, 0)), pl.BlockSpec((seq, 256), lambda b, i: (b, 0)),
                  pl.BlockSpec((nkb, 256, KEY_TILE), lambda b, i: (b, 0, 0)), _full((KEY_TILE, KEY_TILE))],
        out_specs=pl.BlockSpec((qb, D_ATT), lambda b, i: (b * nq + i, 0)),
        out_shape=jax.ShapeDtypeStruct((batch * seq, D_ATT), F32),
        scratch_shapes=[pltpu.VMEM((seq, qb), I32), pltpu.VMEM((seq, ATT_HEADS * qb), F32),
                        pltpu.VMEM((D_ATT, ATT_HEADS * qb), F32)],
        compiler_params=_params(2), name="dsa_attention")(qit, qnt, wt, kib, kn, vt, tril)


def _mem_prep_kernel(mem_ref, g_ref, wkv_ref, kg_ref, gmat_ref, kbd_ref, vbd_ref):
    nm = mem_ref.shape[0]
    mb = _rms_rows(mem_ref[...], g_ref[...]).astype(BF16)
    kv = _mm(mb, wkv_ref[...])
    k = kv[:, 0:D_MEMATT]
    kn = (k * lax.rsqrt(_mm_f32_lhs(k * k, gmat_ref[...]) + EPS)) * kg_ref[...]
    k_t = kn.T.astype(BF16)
    zero = jnp.zeros((MEM_HEAD_DIM, nm), BF16)
    kbd_ref[...] = jnp.concatenate(
        [jnp.concatenate([k_t[h * 64:(h + 1) * 64, :] if hh == h else zero for hh in range(MEM_HEADS)], axis=1)
         for h in range(MEM_HEADS)], axis=0)
    v = kv[:, D_MEMATT:]
    lane = lax.broadcasted_iota(I32, (nm, D_MEMATT), 1)
    vbd_ref[...] = jnp.concatenate(
        [jnp.where((lane >= h * 64) & (lane < (h + 1) * 64), v, 0.0) for h in range(MEM_HEADS)],
        axis=0).astype(BF16)


def _mem_prep(mem2, g, wkv, kg, gmat, batch, nm):
    return pl.pallas_call(
        _mem_prep_kernel, grid=(batch,),
        in_specs=[pl.BlockSpec((nm, D_MODEL), lambda b: (b, 0)), _full((1, D_MODEL)), _full((D_MODEL, 512)),
                  _full((1, 256)), _full((256, 256))],
        out_specs=(pl.BlockSpec((None, D_MEMATT, MEM_HEADS * nm), lambda b: (b, 0, 0)),
                   pl.BlockSpec((None, MEM_HEADS * nm, D_MEMATT), lambda b: (b, 0, 0))),
        out_shape=(jax.ShapeDtypeStruct((batch, D_MEMATT, MEM_HEADS * nm), BF16),
                   jax.ShapeDtypeStruct((batch, MEM_HEADS * nm, D_MEMATT), BF16)),
        compiler_params=_params(1), name="mem_prep")(mem2, g, wkv, kg, gmat)


def _mix_memx_kernel(x_ref, yc_ref, ys_ref, ya_ref, wout_ref, g_ref, wq_ref, qg_ref, gmat_ref, kbd_ref, vbd_ref,
                     wo_ref, out_ref):
    nm = kbd_ref.shape[1] // MEM_HEADS
    y_mix = jnp.concatenate([yc_ref[...], ys_ref[...], ya_ref[...]], axis=1).astype(BF16)
    x1 = x_ref[...] + _mm(y_mix, wout_ref[...])
    hb = _rms_rows(x1, g_ref[...]).astype(BF16)
    q = _mm(hb, wq_ref[...])
    qn = (q * lax.rsqrt(_mm_f32_lhs(q * q, gmat_ref[...]) + EPS)) * qg_ref[...]
    s = _mm(qn.astype(BF16), kbd_ref[...]) * (MEM_HEAD_DIM ** -0.5)
    ps = []
    for h in range(MEM_HEADS):
        sh = s[:, h * nm:(h + 1) * nm]
        e = jnp.exp(sh - jnp.max(sh, axis=-1, keepdims=True))
        ps.append(e / jnp.sum(e, axis=-1, keepdims=True))
    o = _mm(jnp.concatenate(ps, axis=1).astype(BF16), vbd_ref[...])
    out_ref[...] = x1 + _mm(o.astype(BF16), wo_ref[...])


def _mix_memx(x2, yc, ys, ya, wout, g, wq, qg, gmat, kbd, vbd, wo, seq):
    t = x2.shape[0]
    tm = TOKEN_TILE
    per_b = seq // tm
    nm4 = kbd.shape[2]
    row = lambda w: pl.BlockSpec((tm, w), lambda i: (i, 0))
    return pl.pallas_call(
        _mix_memx_kernel, grid=(t // tm,),
        in_specs=[row(D_MODEL), row(256), row(512), row(256), _full((D_MODEL, D_MODEL)), _full((1, D_MODEL)),
                  _full((D_MODEL, 256)), _full((1, 256)), _full((256, 256)),
                  pl.BlockSpec((None, D_MEMATT, nm4), lambda i: (i // per_b, 0, 0)),
                  pl.BlockSpec((None, nm4, D_MEMATT), lambda i: (i // per_b, 0, 0)), _full((256, D_MODEL))],
        out_specs=row(D_MODEL), out_shape=jax.ShapeDtypeStruct((t, D_MODEL), F32),
        compiler_params=_params(1), name="mix_memx")(x2, yc, ys, ya, wout, g, wq, qg, gmat, kbd, vbd, wo)


def _ffn_kernel(x_ref, g_ref, wg_ref, wu_ref, wd_ref, out_ref, hb_ref, acc_ref):
    c = pl.program_id(1)

    @pl.when(c == 0)
    def _():
        hb_ref[...] = _rms_rows(x_ref[...], g_ref[...]).astype(BF16)
        acc_ref[...] = jnp.zeros(acc_ref.shape, F32)

    hb = hb_ref[...]
    a = _silu(_mm(hb, wg_ref[...])) * _mm(hb, wu_ref[...])
    acc_ref[...] += _mm(a.astype(BF16), wd_ref[...])

    @pl.when(c == pl.num_programs(1) - 1)
    def _():
        out_ref[...] = x_ref[...] + acc_ref[...]


def _ffn(x2, g, wg, wu, wd):
    t = x2.shape[0]
    tm = TOKEN_TILE
    fc = FF_CHUNK
    return pl.pallas_call(
        _ffn_kernel, grid=(t // tm, D_FF // fc),
        in_specs=[pl.BlockSpec((tm, D_MODEL), lambda i, c: (i, 0)), _full((1, D_MODEL)),
                  pl.BlockSpec((D_MODEL, fc), lambda i, c: (0, c)), pl.BlockSpec((D_MODEL, fc), lambda i, c: (0, c)),
                  pl.BlockSpec((fc, D_MODEL), lambda i, c: (c, 0))],
        out_specs=pl.BlockSpec((tm, D_MODEL), lambda i, c: (i, 0)),
        out_shape=jax.ShapeDtypeStruct((t, D_MODEL), F32),
        scratch_shapes=[pltpu.VMEM((tm, D_MODEL), BF16), pltpu.VMEM((tm, D_MODEL), F32)],
        compiler_params=_params(2), name="ffn_dense")(x2, g, wg, wu, wd)


def _router_kernel(x_ref, g_ref, wr_ref, ltri_ref, ri_ref, rf_ref, cnt_ref, carry_ref):
    tm = x_ref.shape[0]
    lane = lax.broadcasted_iota(I32, (tm, LANES), 1)

    @pl.when(pl.program_id(0) == 0)
    def _():
        carry_ref[...] = jnp.zeros(carry_ref.shape, F32)

    h = _rms_rows(x_ref[...], g_ref[...])
    logits = jnp.where(lane < N_EXPERTS, _mm_f32_f32(h, wr_ref[...]), NEG_INF)
    v1 = jnp.max(logits, axis=-1, keepdims=True)
    i1 = jnp.min(jnp.where(logits == v1, lane, LANES), axis=-1, keepdims=True)
    rest = jnp.where(lane == i1, NEG_INF, logits)
    v2 = jnp.max(rest, axis=-1, keepdims=True)
    i2 = jnp.min(jnp.where(rest == v2, lane, LANES), axis=-1, keepdims=True)
    e2 = jnp.exp(v2 - v1)
    den = 1.0 + e2
    hit = jnp.where((lane == i1) | (lane == i2), 1.0, 0.0)
    before = carry_ref[...] + _mm(ltri_ref[...], hit.astype(BF16))
    r1 = jnp.sum(jnp.where(lane == i1, before, 0.0), axis=-1, keepdims=True).astype(I32)
    r2 = jnp.sum(jnp.where(lane == i2, before, 0.0), axis=-1, keepdims=True).astype(I32)
    carry_ref[...] = carry_ref[...] + jnp.sum(hit, axis=0, keepdims=True)
    cnt_ref[...] = carry_ref[...]
    ri_ref[...] = jnp.where(lane == 0, i1, jnp.where(lane == 1, i2, jnp.where(lane == 2, r1,
                                                                             jnp.where(lane == 3, r2, 0))))
    rf_ref[...] = jnp.where(lane == 0, 1.0 / den, jnp.where(lane == 1, e2 / den, 0.0))


def _router(x2, g, wr, ltri):
    t = x2.shape[0]
    tm = TOKEN_TILE
    row = lambda w: pl.BlockSpec((tm, w), lambda i: (i, 0))
    return pl.pallas_call(
        _router_kernel, grid=(t // tm,),
        in_specs=[row(D_MODEL), _full((1, D_MODEL)), _full((D_MODEL, LANES)), _full((tm, tm))],
        out_specs=(row(LANES), row(LANES), _full((1, LANES))),
        out_shape=(jax.ShapeDtypeStruct((t, LANES), I32), jax.ShapeDtypeStruct((t, LANES), F32),
                   jax.ShapeDtypeStruct((1, LANES), F32)),
        scratch_shapes=[pltpu.VMEM((1, LANES), F32)],
        compiler_params=_params(1), name="moe_router")(x2, g, wr, ltri)


def _row_copy(src_hbm, src_row, dst, dst_row, sem):
    return pltpu.make_async_copy(src_hbm.at[pl.ds(src_row, 1), :], dst.at[pl.ds(dst_row, 1), :], sem)


def _dispatch_kernel(dest_ref, x_hbm, xs_init_hbm, xs_hbm, sem):
    del xs_init_hbm
    n_tok = dest_ref.shape[0] // 2
    t0 = pl.program_id(0) * n_tok

    def issue(r, c):
        for k in range(2):
            _row_copy(x_hbm, t0 + r, xs_hbm, dest_ref[2 * r + k], sem).start()
        return c

    lax.fori_loop(0, n_tok, issue, 0)

    def drain(r, c):
        _row_copy(x_hbm, 0, xs_hbm, 0, sem).wait()
        return c

    lax.fori_loop(0, 2 * n_tok, drain, 0)


def _dispatch(dest_flat, x2, n_slots):
    t = x2.shape[0]
    td = MOE_DISPATCH_TILE
    xs0 = jnp.zeros((n_slots, D_MODEL), F32)
    return pl.pallas_call(
        _dispatch_kernel, grid=(t // td,),
        in_specs=[pl.BlockSpec((2 * td,), lambda i: (i,), memory_space=pltpu.SMEM),
                  pl.BlockSpec(memory_space=pl.ANY), pl.BlockSpec(memory_space=pl.ANY)],
        out_specs=pl.BlockSpec(memory_space=pl.ANY),
        out_shape=jax.ShapeDtypeStruct((n_slots, D_MODEL), F32),
        scratch_shapes=[pltpu.SemaphoreType.DMA(())],
        input_output_aliases={2: 0},
        compiler_params=_params(1), name="moe_dispatch")(dest_flat, x2, xs0)


def _experts_kernel(bexp_ref, nused_ref, xs_ref, g_ref, wg_ref, wu_ref, wd_ref, out_ref, hb_ref, acc_ref):
    del bexp_ref
    c = pl.program_id(1)
    last_c = pl.num_programs(1) - 1
    used = pl.program_id(0) < nused_ref[0]

    @pl.when(used & (c == 0))
    def _():
        hb_ref[...] = _rms_rows(xs_ref[...], g_ref[...]).astype(BF16)
        acc_ref[...] = jnp.zeros(acc_ref.shape, F32)

    @pl.when(used)
    def _():
        hb = hb_ref[...]
        a = _silu(_mm(hb, wg_ref[...])) * _mm(hb, wu_ref[...])
        acc_ref[...] += _mm(a.astype(BF16), wd_ref[...])

    @pl.when(used & (c == last_c))
    def _():
        out_ref[...] = acc_ref[...]

    @pl.when(jnp.logical_not(used) & (c == last_c))
    def _():
        out_ref[...] = jnp.zeros(out_ref.shape, F32)


def _experts(block_exp, n_used, xs, g, wg, wu, wd):
    n_slots = xs.shape[0]
    bm = MOE_SLOT_BLOCK
    fc = FF_CHUNK
    grid_spec = pltpu.PrefetchScalarGridSpec(
        num_scalar_prefetch=2, grid=(n_slots // bm, D_FF // fc),
        in_specs=[pl.BlockSpec((bm, D_MODEL), lambda b, c, be, nu: (b, 0)),
                  pl.BlockSpec((1, D_MODEL), lambda b, c, be, nu: (0, 0)),
                  pl.BlockSpec((None, D_MODEL, fc), lambda b, c, be, nu: (be[b], 0, c)),
                  pl.BlockSpec((None, D_MODEL, fc), lambda b, c, be, nu: (be[b], 0, c)),
                  pl.BlockSpec((None, fc, D_MODEL), lambda b, c, be, nu: (be[b], c, 0))],
        out_specs=pl.BlockSpec((bm, D_MODEL), lambda b, c, be, nu: (b, 0)),
        scratch_shapes=[pltpu.VMEM((bm, D_MODEL), BF16), pltpu.VMEM((bm, D_MODEL), F32)])
    return pl.pallas_call(
        _experts_kernel, grid_spec=grid_spec, out_shape=jax.ShapeDtypeStruct((n_slots, D_MODEL), F32),
        compiler_params=_params(2), name="moe_experts")(block_exp, n_used, xs, g, wg, wu, wd)


def _combine_kernel(dest_ref, x_ref, rf_ref, ys_hbm, out_ref, buf_ref, sem):
    n_tok = x_ref.shape[0]

    def issue(r, c):
        for k in range(2):
            _row_copy(ys_hbm, dest_ref[2 * r + k], buf_ref.at[k], r, sem).start()
        return c

    lax.fori_loop(0, n_tok, issue, 0)

    def drain(r, c):
        _row_copy(ys_hbm, 0, buf_ref.at[0], 0, sem).wait()
        return c

    lax.fori_loop(0, 2 * n_tok, drain, 0)
    gates = rf_ref[...]
    out_ref[...] = x_ref[...] + (buf_ref[0] * gates[:, 0:1] + buf_ref[1] * gates[:, 1:2])


def _combine(dest_flat, x2, rf, ys):
    t = x2.shape[0]
    tc = MOE_COMBINE_TILE
    return pl.pallas_call(
        _combine_kernel, grid=(t // tc,),
        in_specs=[pl.BlockSpec((2 * tc,), lambda i: (i,), memory_space=pltpu.SMEM),
                  pl.BlockSpec((tc, D_MODEL), lambda i: (i, 0)), pl.BlockSpec((tc, LANES), lambda i: (i, 0)),
                  pl.BlockSpec(memory_space=pl.ANY)],
        out_specs=pl.BlockSpec((tc, D_MODEL), lambda i: (i, 0)),
        out_shape=jax.ShapeDtypeStruct((t, D_MODEL), F32),
        scratch_shapes=[pltpu.VMEM((2, tc, D_MODEL), F32), pltpu.SemaphoreType.DMA(())],
        compiler_params=_params(1), name="moe_combine")(dest_flat, x2, rf, ys)


def _moe(x2, g, wr, wg, wu, wd, ltri):
    t = x2.shape[0]
    bm = MOE_SLOT_BLOCK
    n_blocks = (2 * t) // bm + N_EXPERTS
    ri, rf, cnt = _router(x2, g, wr, ltri)
    counts = cnt[0, :N_EXPERTS].astype(I32)
    padded = (counts + bm - 1) // bm * bm
    pend = jnp.cumsum(padded)
    pstart = pend - padded
    dest = (pstart[ri[:, 0:2]] + ri[:, 2:4]).reshape(2 * t)
    n_used = (pend[-1:] // bm).astype(I32)
    block_exp = jnp.minimum(jnp.searchsorted(pend, jnp.arange(n_blocks, dtype=I32) * bm, side="right"),
                            N_EXPERTS - 1).astype(I32)
    xs = _dispatch(dest, x2, n_blocks * bm)
    ys = _experts(block_exp, n_used, xs, g, wg, wu, wd)
    return _combine(dest, x2, rf, ys)


def _const_mats():
    r = np.arange(256)
    gmat = ((r[:, None] // 64) == (r[None, :] // 64)).astype(np.float32) / 64.0
    d = r % 64
    same = (r[:, None] // 64) == (r[None, :] // 64)
    swap = ((d[None, :] < 8) & (d[:, None] == d[None, :] + 8)) | \
           ((d[None, :] >= 8) & (d[None, :] < 16) & (d[:, None] == d[None, :] - 8))
    pmat = (same & swap).astype(np.float32)
    tri = np.tril(np.ones((SSD_CHUNK, SSD_CHUNK), np.float32))
    rsel = np.zeros((LANES, D_SSD), np.float32)
    rep = np.zeros((D_SSD, SSD_HEADS * SSD_CHUNK), np.float32)
    sel8 = np.zeros((SSD_HEADS, D_SSD), np.float32)
    for h in range(SSD_HEADS):
        rsel[IDX_DIM + h, h * 64:(h + 1) * 64] = 1.0
        rep[h * 64, h * SSD_CHUNK:(h + 1) * SSD_CHUNK] = 1.0
        sel8[h, h * 64] = 1.0
    tril = np.tril(np.ones((KEY_TILE, KEY_TILE), np.float32), -1)
    ltri = np.tril(np.ones((TOKEN_TILE, TOKEN_TILE), np.float32), -1)
    to = lambda a: jnp.asarray(a, dtype=BF16)
    return to(gmat), to(pmat), (to(tri), to(rsel), to(rep), to(sel8)), to(tril), to(ltri)


def kernel(x, mem, positions, mix_norm_g, w_in, conv_dw_w, conv_ln_g, conv_ln_b, conv_pw_w, ssd_conv_w, ssd_conv_b, ssd_dt_bias, ssd_a_log, ssd_d, ssd_norm_g, att_q_norm_g, att_k_norm_g, w_out, memx_norm_g, mem_norm_g, memx_w_q, memx_w_kv, memx_q_norm_g, memx_k_norm_g, memx_w_o, ffn_norm_g, ffn_w_gate, ffn_w_up, ffn_w_down, moe_w_router, moe_w_gate, moe_w_up, moe_w_down):
    batch, seq, d = x.shape
    nm = mem.shape[1]
    depth = w_in.shape[0]
    t = batch * seq
    assert d == D_MODEL and seq % TOKEN_TILE == 0 and seq % SEQ_TILE == 0 and seq % KEY_TILE == 0
    assert TOKEN_TILE % KEY_TILE == 0 and KEY_TILE % Q_TILE == 0 and D_FF % FF_CHUNK == 0

    gmat, pmat, ssd_consts, tril, ltri = _const_mats()

    inv = ROPE_THETA ** (-jnp.arange(0, ROT_DIM, 2, dtype=F32) / ROT_DIM)
    ang = positions.astype(F32).reshape(t, 1) * inv
    cos, sin = jnp.cos(ang), jnp.sin(ang)
    c64 = jnp.concatenate([cos, cos, jnp.ones((t, 48), F32)], axis=1)
    s64 = jnp.concatenate([-sin, sin, jnp.zeros((t, 48), F32)], axis=1)
    c128 = jnp.concatenate([c64, c64], axis=1)
    s128 = jnp.concatenate([s64, s64], axis=1)
    ct, st = cos.T, sin.T

    x2 = x.reshape(t, d)
    mem2 = mem.reshape(batch * nm, d)
    row = lambda v: v.reshape(1, -1)
    rep64 = lambda v: jnp.repeat(v, SSD_HEAD_DIM).reshape(1, -1)
    o = _OFF

    for i in range(depth):
        w = w_in[i]
        wstd = jnp.concatenate([w[:, o[0]:o[3]], w[:, o[5]:o[6]], w[:, o[8]:o[9]], w[:, o[3]:o[4]],
                                jnp.zeros((d, LANES - IDX_DIM - SSD_HEADS), F32)], axis=1).astype(BF16)
        wtrn = jnp.concatenate([w[:, o[7]:o[8]].T, w[:, o[4]:o[5]].T, w[:, o[6]:o[7]].T, w[:, o[9]:o[10]].T,
                                jnp.zeros((N_TRN - 1032, d), F32)], axis=0).astype(BF16)
        uconv, z, xbc, kn, small, kib, qit, qnt, vt, wt = _in_proj(
            x2, row(mix_norm_g[i]), wstd, wtrn, c128, s128, ct, st, att_q_norm_g[i].reshape(64, 1),
            jnp.tile(att_k_norm_g[i], ATT_HEADS).reshape(1, 256), gmat, pmat)

        dw = jnp.concatenate([conv_dw_w[i], jnp.zeros((32 - CONV_WIDTH, D_CONV), F32)], axis=0)
        y_conv = _conformer_conv(uconv, dw, row(conv_ln_g[i]), row(conv_ln_b[i]), conv_pw_w[i].astype(BF16),
                                 batch, seq)
        cw = jnp.concatenate([ssd_conv_w[i], jnp.zeros((8 - SSD_CONV, D_XBC), F32)], axis=0)
        y_ssd = _ssd(z, xbc, small, cw, row(ssd_conv_b[i]), rep64(ssd_dt_bias[i]), rep64(ssd_a_log[i]),
                     rep64(ssd_d[i]), row(ssd_norm_g[i]), ssd_consts, batch, seq)
        y_att = _dsa(qit, qnt, wt, kib, kn, vt, tril, batch, seq)

        kbd, vbd = _mem_prep(mem2, row(mem_norm_g[i]), memx_w_kv[i].astype(BF16),
                             jnp.tile(memx_k_norm_g[i], MEM_HEADS).reshape(1, 256), gmat, batch, nm)
        x2 = _mix_memx(x2, y_conv, y_ssd, y_att, w_out[i].astype(BF16), row(memx_norm_g[i]),
                       memx_w_q[i].astype(BF16), jnp.tile(memx_q_norm_g[i], MEM_HEADS).reshape(1, 256), gmat,
                       kbd, vbd, memx_w_o[i].astype(BF16), seq)

        j = i // 2
        if i % 2 == 0:
            x2 = _ffn(x2, row(ffn_norm_g[i]), ffn_w_gate[j].astype(BF16), ffn_w_up[j].astype(BF16),
                      ffn_w_down[j].astype(BF16))
        else:
            wr = jnp.concatenate([moe_w_router[j], jnp.zeros((d, LANES - N_EXPERTS), F32)], axis=1)
            x2 = _moe(x2, row(ffn_norm_g[i]), wr, moe_w_gate[j].astype(BF16), moe_w_up[j].astype(BF16),
                      moe_w_down[j].astype(BF16), ltri)
    return x2.reshape(batch, seq, d)
```

```python
import functools
import math

import numpy as np
import jax
import jax.numpy as jnp
from jax import lax
from jax.experimental import pallas as pl
from jax.experimental.pallas import tpu as pltpu

F32 = jnp.float32
BF16 = jnp.bfloat16
I32 = jnp.int32

EPS = 1e-6
D_MODEL = 1024
D_CONV = 256
CONV_WIDTH = 31
SSD_HEAD_DIM = 64
D_SSD = 512
SSD_HEADS = 8
SSD_GROUPS = 2
SSD_STATE = 128
SSD_CONV = 4
SSD_CHUNK = 128
D_XBC = D_SSD + 2 * SSD_GROUPS * SSD_STATE
ATT_HEAD_DIM = 64
D_ATT = 256
ATT_HEADS = 4
IDX_HEADS = 8
IDX_DIM = 64
TOPK_MAX = 256
ROPE_THETA = 500000.0
ROT_DIM = ATT_HEAD_DIM // 4
MEM_HEADS = 4
MEM_HEAD_DIM = 64
D_MEMATT = 256
D_FF = 2816
N_EXPERTS = 8

LANES = 128
SUBLANES = 8
VMEM_LIMIT_BYTES = 56 * 1024 * 1024

TOKEN_TILE = 512
SEQ_TILE = 512
Q_TILE = 128
KEY_TILE = 256
FF_CHUNK = 1408
MOE_SLOT_BLOCK = 512
MOE_DISPATCH_TILE = 512
MOE_COMBINE_TILE = 256
MOE_DMA_UNROLL = 8
CONV_ROWS = 64
CONV_HALO = 32
SSD_HALO = 8
INT_MIN = -2 ** 31
NEG_INF = float("-inf")

_SZ = [2 * D_CONV, D_SSD, D_XBC, SSD_HEADS, D_ATT, D_ATT, D_ATT, IDX_HEADS * IDX_DIM, IDX_DIM, IDX_HEADS]
_OFF = np.concatenate([[0], np.cumsum(_SZ)]).tolist()
N_STD = 2 * D_CONV + D_SSD + D_XBC + D_ATT + LANES
N_TRN = IDX_HEADS * IDX_DIM + D_ATT + D_ATT + 16


def _mm(a, b):
    return jnp.dot(a, b, preferred_element_type=F32)


def _mm_nt(a, b):
    return lax.dot_general(a, b, (((1,), (1,)), ((), ())), preferred_element_type=F32)


def _split3(a):
    hi = a.astype(BF16)
    r = a - hi.astype(F32)
    mid = r.astype(BF16)
    lo = (r - mid.astype(F32)).astype(BF16)
    return hi, mid, lo


def _mm_f32_lhs(a, b_exact):
    hi, mid, lo = _split3(a)
    return (_mm(hi, b_exact) + _mm(mid, b_exact)) + _mm(lo, b_exact)


def _mm_f32_rhs(a_exact, b):
    hi, mid, lo = _split3(b)
    return (_mm(a_exact, hi) + _mm(a_exact, mid)) + _mm(a_exact, lo)


def _mm_nt_f32_rhs(a_exact, b):
    hi, mid, lo = _split3(b)
    return (_mm_nt(a_exact, hi) + _mm_nt(a_exact, mid)) + _mm_nt(a_exact, lo)


def _mm_f32_f32(a, b):
    a1, a2, a3 = _split3(a)
    b1, b2, b3 = _split3(b)
    return (_mm(a1, b1) + (_mm(a1, b2) + _mm(a2, b1))) + ((_mm(a1, b3) + _mm(a2, b2)) + _mm(a3, b1))


def _silu(x):
    return x * jax.nn.sigmoid(x)


def _rms_rows(x, g):
    ms = jnp.mean(x * x, axis=-1, keepdims=True)
    return (x * lax.rsqrt(ms + EPS)) * g


def _fold_rows(x, op):
    parts = [x[r:r + SUBLANES] for r in range(0, x.shape[0], SUBLANES)]
    while len(parts) > 1:
        nxt = [op(parts[k], parts[k + 1]) for k in range(0, len(parts) - 1, 2)]
        if len(parts) % 2:
            nxt.append(parts[-1])
        parts = nxt
    return parts[0]


def _params(n_axes):
    return pltpu.CompilerParams(dimension_semantics=("arbitrary",) * n_axes,
                                vmem_limit_bytes=VMEM_LIMIT_BYTES)


def _full(shape):
    n = len(shape)
    return pl.BlockSpec(shape, lambda *_: (0,) * n)


def _in_proj_kernel(x_ref, g_ref, wstd_ref, wtrn_ref, c128_ref, s128_ref, ct_ref, st_ref, qg_ref, kg_ref,
                    gmat_ref, pmat_ref,
                    uconv_ref, z_ref, xbc_ref, kn_ref, small_ref, kib_ref, qit_ref, qnt_ref, vt_ref, wt_ref):
    tm = x_ref.shape[0]
    hb = _rms_rows(x_ref[...], g_ref[...]).astype(BF16)
    uconv_ref[...] = _mm(hb, wstd_ref[:, 0:512])
    z_ref[...] = _mm(hb, wstd_ref[:, 512:1024])
    xbc_ref[...] = _mm(hb, wstd_ref[:, 1024:2048])

    c128 = c128_ref[...]
    s128 = s128_ref[...]
    kraw = _mm(hb, wstd_ref[:, 2048:2304])
    kn = (kraw * lax.rsqrt(_mm_f32_lhs(kraw * kraw, gmat_ref[...]) + EPS)) * kg_ref[...]
    kn = kn * jnp.concatenate([c128, c128], axis=1) + \
        _mm_f32_lhs(kn, pmat_ref[...]) * jnp.concatenate([s128, s128], axis=1)
    kn_ref[...] = kn.astype(BF16)

    sm = _mm(hb, wstd_ref[:, 2304:2432])
    lane = lax.broadcasted_iota(I32, (tm, LANES), 1)
    is_idx = lane < IDX_DIM
    sm = sm * jnp.where(is_idx, c128, 1.0) + _mm_f32_lhs(sm, pmat_ref[0:LANES, 0:LANES]) * jnp.where(is_idx, s128, 0.0)
    small_ref[...] = sm
    kib_ref[...] = jnp.where(is_idx, sm, 0.0).astype(BF16)

    out_t = _mm_nt(wtrn_ref[...], hb)
    ct = ct_ref[...]
    st = st_ref[...]

    def rope_t(blk):
        x1 = blk[0:8]
        x2 = blk[8:16]
        return jnp.concatenate([x1 * ct - x2 * st, x2 * ct + x1 * st, blk[16:64]], axis=0)

    qi = [rope_t(out_t[h * 64:(h + 1) * 64]) for h in range(IDX_HEADS)]
    qit_ref[...] = jnp.concatenate(qi, axis=0).astype(BF16)

    qn = []
    for h in range(ATT_HEADS):
        blk = out_t[512 + h * 64:512 + (h + 1) * 64]
        ms = jnp.mean(blk * blk, axis=0, keepdims=True)
        blk = (blk * lax.rsqrt(ms + EPS)) * qg_ref[...]
        qn.append(rope_t(blk) * (ATT_HEAD_DIM ** -0.5))
    qnt_ref[...] = jnp.concatenate(qn, axis=0).astype(BF16)

    vt = out_t[768:1024].astype(BF16)
    for c in range(tm // KEY_TILE):
        vt_ref[c] = vt[:, c * KEY_TILE:(c + 1) * KEY_TILE]
    wt_ref[...] = out_t[1024:1032] * (IDX_HEADS ** -0.5 * IDX_DIM ** -0.5)


def _in_proj(x2, g, wstd, wtrn, c128, s128, ct, st, qg, kg, gmat, pmat):
    t = x2.shape[0]
    tm = TOKEN_TILE
    row = lambda w: pl.BlockSpec((tm, w), lambda i: (i, 0))
    col = lambda r: pl.BlockSpec((r, tm), lambda i: (0, i))
    out_shape = (
        jax.ShapeDtypeStruct((t, 512), F32), jax.ShapeDtypeStruct((t, 512), F32),
        jax.ShapeDtypeStruct((t, 1024), F32), jax.ShapeDtypeStruct((t, 256), BF16),
        jax.ShapeDtypeStruct((t, LANES), F32), jax.ShapeDtypeStruct((t, LANES), BF16),
        jax.ShapeDtypeStruct((512, t), BF16), jax.ShapeDtypeStruct((256, t), BF16),
        jax.ShapeDtypeStruct((t // KEY_TILE, 256, KEY_TILE), BF16), jax.ShapeDtypeStruct((8, t), F32))
    out_specs = (row(512), row(512), row(1024), row(256), row(LANES), row(LANES), col(512), col(256),
                 pl.BlockSpec((tm // KEY_TILE, 256, KEY_TILE), lambda i: (i, 0, 0)), col(8))
    in_specs = [row(D_MODEL), _full((1, D_MODEL)), _full(wstd.shape), _full(wtrn.shape), row(LANES), row(LANES),
                col(8), col(8), _full((64, 1)), _full((1, 256)), _full((256, 256)), _full((256, 256))]
    return pl.pallas_call(_in_proj_kernel, grid=(t // tm,), in_specs=in_specs, out_specs=out_specs,
                          out_shape=out_shape, compiler_params=_params(1), name="in_proj")(
        x2, g, wstd, wtrn, c128, s128, ct, st, qg, kg, gmat, pmat)


def _conv_kernel(u_ref, dw_ref, lg_ref, lb_ref, pw_ref, out_ref, hp_ref):
    ts = u_ref.shape[0]

    @pl.when(pl.program_id(1) == 0)
    def _():
        hp_ref[0:CONV_HALO, :] = jnp.zeros((CONV_HALO, D_CONV), F32)

    u = u_ref[...]
    hp_ref[CONV_HALO:CONV_HALO + ts, :] = u[:, :D_CONV] * jax.nn.sigmoid(u[:, D_CONV:])
    base = CONV_HALO - (CONV_WIDTH - 1)
    for r0 in range(0, ts, CONV_ROWS):
        acc = dw_ref[0:1, :] * hp_ref[pl.ds(r0 + base, CONV_ROWS), :]
        for k in range(1, CONV_WIDTH):
            acc = acc + dw_ref[k:k + 1, :] * hp_ref[pl.ds(r0 + base + k, CONV_ROWS), :]
        mu = jnp.mean(acc, axis=-1, keepdims=True)
        xc = acc - mu
        y = xc * lax.rsqrt(jnp.mean(xc * xc, axis=-1, keepdims=True) + EPS)
        y = _silu(y * lg_ref[...] + lb_ref[...])
        out_ref[pl.ds(r0, CONV_ROWS), :] = _mm(y.astype(BF16), pw_ref[...])
    hp_ref[0:CONV_HALO, :] = hp_ref[ts:ts + CONV_HALO, :]


def _conformer_conv(uconv, dw, lg, lb, pw, batch, seq):
    ts = SEQ_TILE
    ns = seq // ts
    return pl.pallas_call(
        _conv_kernel, grid=(batch, ns),
        in_specs=[pl.BlockSpec((ts, 512), lambda b, s: (b * ns + s, 0)), _full((32, D_CONV)), _full((1, D_CONV)),
                  _full((1, D_CONV)), _full((D_CONV, D_CONV))],
        out_specs=pl.BlockSpec((ts, D_CONV), lambda b, s: (b * ns + s, 0)),
        out_shape=jax.ShapeDtypeStruct((batch * seq, D_CONV), F32),
        scratch_shapes=[pltpu.VMEM((CONV_HALO + ts, D_CONV), F32)],
        compiler_params=_params(2), name="conformer_conv")(uconv, dw, lg, lb, pw)


def _ssd_kernel(z_ref, xbc_ref, small_ref, cw_ref, cb_ref, dtb_ref, alog_ref, dskip_ref, ng_ref,
                tri_ref, rsel_ref, rep_ref, sel8_ref, out_ref, xp_ref, state_ref):
    ts = z_ref.shape[0]
    L = SSD_CHUNK

    @pl.when(pl.program_id(1) == 0)
    def _():
        xp_ref[0:SSD_HALO, :] = jnp.zeros((SSD_HALO, D_XBC), F32)
        state_ref[...] = jnp.zeros(state_ref.shape, F32)

    xp_ref[SSD_HALO:SSD_HALO + ts, :] = xbc_ref[...]
    a_neg = -jnp.exp(alog_ref[...])
    ii = lax.broadcasted_iota(I32, (L, L), 0)
    jj = lax.broadcasted_iota(I32, (L, L), 1)
    causal = ii >= jj
    first_half = jj < SSD_HEAD_DIM
    gw = D_SSD // SSD_GROUPS

    for c in range(ts // L):
        r0 = c * L
        base = r0 + SSD_HALO - (SSD_CONV - 1)
        conv = cb_ref[...] + cw_ref[0:1, :] * xp_ref[pl.ds(base, L), :]
        for k in range(1, SSD_CONV):
            conv = conv + cw_ref[k:k + 1, :] * xp_ref[pl.ds(base + k, L), :]
        xa = _silu(conv)
        xs = xa[:, 0:D_SSD]
        bm = xa[:, D_SSD:D_SSD + 256]
        cm = xa[:, D_SSD + 256:D_SSD + 512]

        dt_arg = _mm_f32_lhs(small_ref[pl.ds(r0, L), :], rsel_ref[...]) + dtb_ref[...]
        dt = jnp.maximum(dt_arg, 0.0) + jnp.log1p(jnp.exp(-jnp.abs(dt_arg)))
        acs = _mm_f32_rhs(tri_ref[...], dt * a_neg)
        acs_wide = _mm_f32_lhs(acs, rep_ref[...])
        acs_t = _mm_nt_f32_rhs(sel8_ref[...], acs)
        last = acs[L - 1:L, :]
        decay_end = jnp.exp(last - acs)
        decay_in = jnp.exp(acs)
        xd = xs * dt
        xdb = xd.astype(BF16)

        y_parts = []
        for g in range(SSD_GROUPS):
            bg = bm[:, g * SSD_STATE:(g + 1) * SSD_STATE]
            cgb = cm[:, g * SSD_STATE:(g + 1) * SSD_STATE].astype(BF16)
            cb = _mm_nt(cgb, bg.astype(BF16))
            pair_out = []
            for pr in range(2):
                h0 = g * 4 + pr * 2
                xpair = xdb[:, h0 * 64:(h0 + 2) * 64]
                outs = []
                for h in (h0, h0 + 1):
                    seg = acs_wide[:, h * L:(h + 1) * L] - acs_t[h:h + 1, :]
                    m_h = (cb * jnp.exp(jnp.where(causal, seg, NEG_INF))).astype(BF16)
                    outs.append(_mm(m_h, xpair))
                pair_out.append(jnp.where(first_half, outs[0], outs[1]))
            y_diag = jnp.concatenate(pair_out, axis=1)
            gs = slice(g * gw, (g + 1) * gw)
            st = state_ref[:, gs]
            y_off = _mm(cgb, st.astype(BF16)) * decay_in[:, gs]
            contrib = _mm(bg.T.astype(BF16), (xd[:, gs] * decay_end[:, gs]).astype(BF16))
            state_ref[:, gs] = jnp.exp(last[:, gs]) * st + contrib
            y_parts.append(y_diag + y_off)
        y = jnp.concatenate(y_parts, axis=1) + xs * dskip_ref[...]
        y = y * _silu(z_ref[pl.ds(r0, L), :])
        outs = []
        for g in range(SSD_GROUPS):
            gs = slice(g * gw, (g + 1) * gw)
            outs.append(_rms_rows(y[:, gs], ng_ref[:, gs]))
        out_ref[pl.ds(r0, L), :] = jnp.concatenate(outs, axis=1)
    xp_ref[0:SSD_HALO, :] = xp_ref[ts:ts + SSD_HALO, :]


def _ssd(z, xbc, small, cw, cb, dtb, alog, dskip, ng, consts, batch, seq):
    ts = SEQ_TILE
    ns = seq // ts
    tri, rsel, rep, sel8 = consts
    row = lambda w: pl.BlockSpec((ts, w), lambda b, s: (b * ns + s, 0))
    return pl.pallas_call(
        _ssd_kernel, grid=(batch, ns),
        in_specs=[row(512), row(1024), row(LANES), _full((8, D_XBC)), _full((1, D_XBC)), _full((1, 512)),
                  _full((1, 512)), _full((1, 512)), _full((1, 512)), _full(tri.shape), _full(rsel.shape),
                  _full(rep.shape), _full(sel8.shape)],
        out_specs=row(512),
        out_shape=jax.ShapeDtypeStruct((batch * seq, 512), F32),
        scratch_shapes=[pltpu.VMEM((SSD_HALO + ts, D_XBC), F32), pltpu.VMEM((SSD_STATE, D_SSD), F32)],
        compiler_params=_params(2), name="ssd_mixer")(z, xbc, small, cw, cb, dtb, alog, dskip, ng, tri, rsel, rep, sel8)


def _dsa_kernel(qit_ref, qnt_ref, wt_ref, kib_ref, kn_ref, vt_ref, tril_ref, out_ref, keys_ref, s_ref, acc_ref,
                *, ksel):
    qb = Q_TILE
    kb = KEY_TILE
    i = pl.program_id(1)
    nsb = (i * qb + qb + kb - 1) // kb
    qpos = i * qb + lax.broadcasted_iota(I32, (kb, qb), 1)
    krow = lax.broadcasted_iota(I32, (kb, qb), 0)

    rhs_idx = jnp.concatenate([qit_ref[h * IDX_DIM:(h + 1) * IDX_DIM, :] for h in range(IDX_HEADS)], axis=1)
    rhs_idx = jnp.concatenate([rhs_idx, jnp.zeros((LANES - IDX_DIM, IDX_HEADS * qb), BF16)], axis=0)
    wt = wt_ref[...]

    def score_tile(j, carry):
        r = pl.multiple_of(j * kb, kb)
        lg = _mm(kib_ref[pl.ds(r, kb), :], rhs_idx)
        sc = wt[0:1, :] * jnp.maximum(lg[:, 0:qb], 0.0)
        for h in range(1, IDX_HEADS):
            sc = sc + wt[h:h + 1, :] * jnp.maximum(lg[:, h * qb:(h + 1) * qb], 0.0)
        keys_ref[pl.ds(r, kb), :] = jnp.where(j * kb + krow <= qpos, sc, NEG_INF)
        return carry

    lax.fori_loop(0, nsb, score_tile, 0)

    def ordered_to_float(u):
        k = u ^ INT_MIN
        return pltpu.bitcast(k ^ (lax.shift_right_arithmetic(k, 31) & 0x7FFFFFFF), F32)

    def count(cand, strict):
        def body(j, acc):
            kt = keys_ref[pl.ds(pl.multiple_of(j * kb, kb), kb), :]
            hit = (kt > cand) if strict else (kt >= cand)
            return acc + _fold_rows(jnp.where(hit, 1.0, 0.0), jnp.add)
        acc = lax.fori_loop(0, nsb, body, jnp.zeros((SUBLANES, qb), F32))
        return jnp.sum(acc, axis=0, keepdims=True)

    def bit_step(it, tu):
        cand_u = tu | lax.shift_left(jnp.int32(1), 31 - it)
        cnt = count(ordered_to_float(cand_u)[0:1, :], False)
        return jnp.where(cnt >= float(ksel), cand_u, tu)

    tau = ordered_to_float(lax.fori_loop(0, 32, bit_step, jnp.zeros((SUBLANES, qb), I32)))[0:1, :]
    takes_all = qpos[0:1, :] < ksel
    tau = jnp.where(takes_all, NEG_INF, tau)
    need = jnp.where(takes_all, 0.0, float(ksel) - count(tau, True))

    zblk = jnp.zeros((ATT_HEAD_DIM, qb), BF16)
    rhs_att = jnp.concatenate(
        [jnp.concatenate([qnt_ref[h * 64:(h + 1) * 64, :] if hh == h else zblk for hh in range(ATT_HEADS)], axis=1)
         for h in range(ATT_HEADS)], axis=0)

    def pass1(j, carry):
        m, eq_before = carry
        r = pl.multiple_of(j * kb, kb)
        s = _mm(kn_ref[pl.ds(r, kb), :], rhs_att)
        kt = keys_ref[pl.ds(r, kb), :]
        eq = kt == tau
        eqf = jnp.where(eq, 1.0, 0.0)
        rank = eq_before + _mm(tril_ref[...], eqf.astype(BF16))
        sel = (kt > tau) | (eq & (rank < need))
        s = jnp.concatenate([jnp.where(sel, s[:, h * qb:(h + 1) * qb], NEG_INF) for h in range(ATT_HEADS)], axis=1)
        s_ref[pl.ds(r, kb), :] = s
        return (jnp.maximum(m, _fold_rows(s, jnp.maximum)),
                eq_before + jnp.sum(_fold_rows(eqf, jnp.add), axis=0, keepdims=True))

    m8, _ = lax.fori_loop(0, nsb, pass1,
                          (jnp.full((SUBLANES, ATT_HEADS * qb), NEG_INF, F32), jnp.zeros((1, qb), F32)))
    m = jnp.max(m8, axis=0, keepdims=True)

    acc_ref[...] = jnp.zeros(acc_ref.shape, F32)

    def pass2(j, l8):
        p = jnp.exp(s_ref[pl.ds(pl.multiple_of(j * kb, kb), kb), :] - m)
        acc_ref[...] += _mm(vt_ref[j], p.astype(BF16))
        return l8 + _fold_rows(p, jnp.add)

    l = jnp.sum(lax.fori_loop(0, nsb, pass2, jnp.zeros((SUBLANES, ATT_HEADS * qb), F32)), axis=0, keepdims=True)
    o_t = jnp.concatenate(
        [acc_ref[h * 64:(h + 1) * 64, h * qb:(h + 1) * qb] / l[:, h * qb:(h + 1) * qb] for h in range(ATT_HEADS)],
        axis=0)
    out_ref[...] = o_t.T


def _dsa(qit, qnt, wt, kib, kn, vt, tril, batch, seq):
    qb = Q_TILE
    nq = seq // qb
    nkb = seq // KEY_TILE
    ksel = min(TOPK_MAX, seq // 4)
    colq = lambda r: pl.BlockSpec((r, qb), lambda b, i: (0, b * nq + i))
    return pl.pallas_call(
        functools.partial(_dsa_kernel, ksel=ksel), grid=(batch, nq),
        in_specs=[colq(512), colq(256), colq(8),
                  pl.BlockSpec((seq, LANES), lambda b, i: (b, 0)), pl.BlockSpec((seq, 256), lambda b, i: (b, 0)),
                  pl.BlockSpec((nkb, 256, KEY_TILE), lambda b, i: (b, 0, 0)), _full((KEY_TILE, KEY_TILE))],
        out_specs=pl.BlockSpec((qb, D_ATT), lambda b, i: (b * nq + i, 0)),
        out_shape=jax.ShapeDtypeStruct((batch * seq, D_ATT), F32),
        scratch_shapes=[pltpu.VMEM((seq, qb), F32), pltpu.VMEM((seq, ATT_HEADS * qb), F32),
                        pltpu.VMEM((D_ATT, ATT_HEADS * qb), F32)],
        compiler_params=_params(2), name="dsa_attention")(qit, qnt, wt, kib, kn, vt, tril)


def _mem_prep_kernel(mem_ref, g_ref, wkv_ref, kg_ref, gmat_ref, kbd_ref, vbd_ref):
    nm = mem_ref.shape[0]
    mb = _rms_rows(mem_ref[...], g_ref[...]).astype(BF16)
    kv = _mm(mb, wkv_ref[...])
    k = kv[:, 0:D_MEMATT]
    kn = (k * lax.rsqrt(_mm_f32_lhs(k * k, gmat_ref[...]) + EPS)) * kg_ref[...]
    k_t = kn.T.astype(BF16)
    zero = jnp.zeros((MEM_HEAD_DIM, nm), BF16)
    kbd_ref[...] = jnp.concatenate(
        [jnp.concatenate([k_t[h * 64:(h + 1) * 64, :] if hh == h else zero for hh in range(MEM_HEADS)], axis=1)
         for h in range(MEM_HEADS)], axis=0)
    v = kv[:, D_MEMATT:]
    lane = lax.broadcasted_iota(I32, (nm, D_MEMATT), 1)
    vbd_ref[...] = jnp.concatenate(
        [jnp.where((lane >= h * 64) & (lane < (h + 1) * 64), v, 0.0) for h in range(MEM_HEADS)],
        axis=0).astype(BF16)


def _mem_prep(mem2, g, wkv, kg, gmat, batch, nm):
    return pl.pallas_call(
        _mem_prep_kernel, grid=(batch,),
        in_specs=[pl.BlockSpec((nm, D_MODEL), lambda b: (b, 0)), _full((1, D_MODEL)), _full((D_MODEL, 512)),
                  _full((1, 256)), _full((256, 256))],
        out_specs=(pl.BlockSpec((None, D_MEMATT, MEM_HEADS * nm), lambda b: (b, 0, 0)),
                   pl.BlockSpec((None, MEM_HEADS * nm, D_MEMATT), lambda b: (b, 0, 0))),
        out_shape=(jax.ShapeDtypeStruct((batch, D_MEMATT, MEM_HEADS * nm), BF16),
                   jax.ShapeDtypeStruct((batch, MEM_HEADS * nm, D_MEMATT), BF16)),
        compiler_params=_params(1), name="mem_prep")(mem2, g, wkv, kg, gmat)


def _mix_memx_kernel(x_ref, yc_ref, ys_ref, ya_ref, wout_ref, g_ref, wq_ref, qg_ref, gmat_ref, kbd_ref, vbd_ref,
                     wo_ref, out_ref):
    nm = kbd_ref.shape[1] // MEM_HEADS
    y_mix = jnp.concatenate([yc_ref[...], ys_ref[...], ya_ref[...]], axis=1).astype(BF16)
    x1 = x_ref[...] + _mm(y_mix, wout_ref[...])
    hb = _rms_rows(x1, g_ref[...]).astype(BF16)
    q = _mm(hb, wq_ref[...])
    qn = (q * lax.rsqrt(_mm_f32_lhs(q * q, gmat_ref[...]) + EPS)) * qg_ref[...]
    s = _mm(qn.astype(BF16), kbd_ref[...]) * (MEM_HEAD_DIM ** -0.5)
    ps = []
    for h in range(MEM_HEADS):
        sh = s[:, h * nm:(h + 1) * nm]
        e = jnp.exp(sh - jnp.max(sh, axis=-1, keepdims=True))
        ps.append(e / jnp.sum(e, axis=-1, keepdims=True))
    o = _mm(jnp.concatenate(ps, axis=1).astype(BF16), vbd_ref[...])
    out_ref[...] = x1 + _mm(o.astype(BF16), wo_ref[...])


def _mix_memx(x2, yc, ys, ya, wout, g, wq, qg, gmat, kbd, vbd, wo, seq):
    t = x2.shape[0]
    tm = TOKEN_TILE
    per_b = seq // tm
    nm4 = kbd.shape[2]
    row = lambda w: pl.BlockSpec((tm, w), lambda i: (i, 0))
    return pl.pallas_call(
        _mix_memx_kernel, grid=(t // tm,),
        in_specs=[row(D_MODEL), row(256), row(512), row(256), _full((D_MODEL, D_MODEL)), _full((1, D_MODEL)),
                  _full((D_MODEL, 256)), _full((1, 256)), _full((256, 256)),
                  pl.BlockSpec((None, D_MEMATT, nm4), lambda i: (i // per_b, 0, 0)),
                  pl.BlockSpec((None, nm4, D_MEMATT), lambda i: (i // per_b, 0, 0)), _full((256, D_MODEL))],
        out_specs=row(D_MODEL), out_shape=jax.ShapeDtypeStruct((t, D_MODEL), F32),
        compiler_params=_params(1), name="mix_memx")(x2, yc, ys, ya, wout, g, wq, qg, gmat, kbd, vbd, wo)


def _ffn_kernel(x_ref, g_ref, wg_ref, wu_ref, wd_ref, out_ref, hb_ref, acc_ref):
    c = pl.program_id(1)

    @pl.when(c == 0)
    def _():
        hb_ref[...] = _rms_rows(x_ref[...], g_ref[...]).astype(BF16)
        acc_ref[...] = jnp.zeros(acc_ref.shape, F32)

    hb = hb_ref[...]
    a = _silu(_mm(hb, wg_ref[...])) * _mm(hb, wu_ref[...])
    acc_ref[...] += _mm(a.astype(BF16), wd_ref[...])

    @pl.when(c == pl.num_programs(1) - 1)
    def _():
        out_ref[...] = x_ref[...] + acc_ref[...]


def _ffn(x2, g, wg, wu, wd):
    t = x2.shape[0]
    tm = TOKEN_TILE
    fc = FF_CHUNK
    return pl.pallas_call(
        _ffn_kernel, grid=(t // tm, D_FF // fc),
        in_specs=[pl.BlockSpec((tm, D_MODEL), lambda i, c: (i, 0)), _full((1, D_MODEL)),
                  pl.BlockSpec((D_MODEL, fc), lambda i, c: (0, c)), pl.BlockSpec((D_MODEL, fc), lambda i, c: (0, c)),
                  pl.BlockSpec((fc, D_MODEL), lambda i, c: (c, 0))],
        out_specs=pl.BlockSpec((tm, D_MODEL), lambda i, c: (i, 0)),
        out_shape=jax.ShapeDtypeStruct((t, D_MODEL), F32),
        scratch_shapes=[pltpu.VMEM((tm, D_MODEL), BF16), pltpu.VMEM((tm, D_MODEL), F32)],
        compiler_params=_params(2), name="ffn_dense")(x2, g, wg, wu, wd)


def _router_kernel(x_ref, g_ref, wr_ref, ltri_ref, ri_ref, rf_ref, cnt_ref, carry_ref):
    tm = x_ref.shape[0]
    lane = lax.broadcasted_iota(I32, (tm, LANES), 1)

    @pl.when(pl.program_id(0) == 0)
    def _():
        carry_ref[...] = jnp.zeros(carry_ref.shape, F32)

    h = _rms_rows(x_ref[...], g_ref[...])
    logits = jnp.where(lane < N_EXPERTS, _mm_f32_f32(h, wr_ref[...]), NEG_INF)
    v1 = jnp.max(logits, axis=-1, keepdims=True)
    i1 = jnp.min(jnp.where(logits == v1, lane, LANES), axis=-1, keepdims=True)
    rest = jnp.where(lane == i1, NEG_INF, logits)
    v2 = jnp.max(rest, axis=-1, keepdims=True)
    i2 = jnp.min(jnp.where(rest == v2, lane, LANES), axis=-1, keepdims=True)
    e2 = jnp.exp(v2 - v1)
    den = 1.0 + e2
    hit = jnp.where((lane == i1) | (lane == i2), 1.0, 0.0)
    before = carry_ref[...] + _mm(ltri_ref[...], hit.astype(BF16))
    r1 = jnp.sum(jnp.where(lane == i1, before, 0.0), axis=-1, keepdims=True).astype(I32)
    r2 = jnp.sum(jnp.where(lane == i2, before, 0.0), axis=-1, keepdims=True).astype(I32)
    carry_ref[...] = carry_ref[...] + jnp.sum(hit, axis=0, keepdims=True)
    cnt_ref[...] = carry_ref[...]
    ri_ref[...] = jnp.where(lane == 0, i1, jnp.where(lane == 1, i2, jnp.where(lane == 2, r1,
                                                                             jnp.where(lane == 3, r2, 0))))
    rf_ref[...] = jnp.where(lane == 0, 1.0 / den, jnp.where(lane == 1, e2 / den, 0.0))


def _router(x2, g, wr, ltri):
    t = x2.shape[0]
    tm = TOKEN_TILE
    row = lambda w: pl.BlockSpec((tm, w), lambda i: (i, 0))
    return pl.pallas_call(
        _router_kernel, grid=(t // tm,),
        in_specs=[row(D_MODEL), _full((1, D_MODEL)), _full((D_MODEL, LANES)), _full((tm, tm))],
        out_specs=(row(LANES), row(LANES), _full((1, LANES))),
        out_shape=(jax.ShapeDtypeStruct((t, LANES), I32), jax.ShapeDtypeStruct((t, LANES), F32),
                   jax.ShapeDtypeStruct((1, LANES), F32)),
        scratch_shapes=[pltpu.VMEM((1, LANES), F32)],
        compiler_params=_params(1), name="moe_router")(x2, g, wr, ltri)


def _row_copy(src, src_row, dst, dst_row, sem):
    return pltpu.make_async_copy(src.at[pl.ds(src_row, 1), :], dst.at[pl.ds(dst_row, 1), :], sem)


def _dispatch_kernel(dest_ref, x_ref, xs_init_hbm, xs_hbm, sem):
    del xs_init_hbm
    n_tok = x_ref.shape[0]

    def issue(r, c):
        for k in range(2):
            _row_copy(x_ref, r, xs_hbm, dest_ref[2 * r + k], sem).start(priority=k)
        return c

    lax.fori_loop(0, n_tok, issue, 0, unroll=MOE_DMA_UNROLL)
    for _ in range(2):
        pltpu.make_async_copy(x_ref, xs_hbm.at[pl.ds(0, n_tok), :], sem).wait()


def _dispatch(dest_flat, x2, n_slots):
    t = x2.shape[0]
    td = MOE_DISPATCH_TILE
    xs0 = jnp.zeros((n_slots, D_MODEL), F32)
    return pl.pallas_call(
        _dispatch_kernel, grid=(t // td,),
        in_specs=[pl.BlockSpec((2 * td,), lambda i: (i,), memory_space=pltpu.SMEM),
                  pl.BlockSpec((td, D_MODEL), lambda i: (i, 0)), pl.BlockSpec(memory_space=pl.ANY)],
        out_specs=pl.BlockSpec(memory_space=pl.ANY),
        out_shape=jax.ShapeDtypeStruct((n_slots, D_MODEL), F32),
        scratch_shapes=[pltpu.SemaphoreType.DMA(())],
        input_output_aliases={2: 0},
        compiler_params=_params(1), name="moe_dispatch")(dest_flat, x2, xs0)


def _experts_kernel(bexp_ref, nused_ref, xs_ref, g_ref, wg_ref, wu_ref, wd_ref, out_ref, hb_ref, acc_ref):
    del bexp_ref
    c = pl.program_id(1)
    last_c = pl.num_programs(1) - 1
    used = pl.program_id(0) < nused_ref[0]

    @pl.when(used & (c == 0))
    def _():
        hb_ref[...] = _rms_rows(xs_ref[...], g_ref[...]).astype(BF16)
        acc_ref[...] = jnp.zeros(acc_ref.shape, F32)

    @pl.when(used)
    def _():
        hb = hb_ref[...]
        a = _silu(_mm(hb, wg_ref[...])) * _mm(hb, wu_ref[...])
        acc_ref[...] += _mm(a.astype(BF16), wd_ref[...])

    @pl.when(used & (c == last_c))
    def _():
        out_ref[...] = acc_ref[...]

    @pl.when(jnp.logical_not(used) & (c == last_c))
    def _():
        out_ref[...] = jnp.zeros(out_ref.shape, F32)


def _experts(block_exp, n_used, xs, g, wg, wu, wd):
    n_slots = xs.shape[0]
    bm = MOE_SLOT_BLOCK
    fc = FF_CHUNK
    grid_spec = pltpu.PrefetchScalarGridSpec(
        num_scalar_prefetch=2, grid=(n_slots // bm, D_FF // fc),
        in_specs=[pl.BlockSpec((bm, D_MODEL), lambda b, c, be, nu: (b, 0)),
                  pl.BlockSpec((1, D_MODEL), lambda b, c, be, nu: (0, 0)),
                  pl.BlockSpec((None, D_MODEL, fc), lambda b, c, be, nu: (be[b], 0, c)),
                  pl.BlockSpec((None, D_MODEL, fc), lambda b, c, be, nu: (be[b], 0, c)),
                  pl.BlockSpec((None, fc, D_MODEL), lambda b, c, be, nu: (be[b], c, 0))],
        out_specs=pl.BlockSpec((bm, D_MODEL), lambda b, c, be, nu: (b, 0)),
        scratch_shapes=[pltpu.VMEM((bm, D_MODEL), BF16), pltpu.VMEM((bm, D_MODEL), F32)])
    return pl.pallas_call(
        _experts_kernel, grid_spec=grid_spec, out_shape=jax.ShapeDtypeStruct((n_slots, D_MODEL), F32),
        compiler_params=_params(2), name="moe_experts")(block_exp, n_used, xs, g, wg, wu, wd)


def _combine_kernel(dest_ref, x_ref, rf_ref, ys_hbm, out_ref, buf_ref, sem):
    n_tok = x_ref.shape[0]

    def issue(r, c):
        for k in range(2):
            _row_copy(ys_hbm, dest_ref[2 * r + k], buf_ref.at[k], r, sem).start(priority=k)
        return c

    lax.fori_loop(0, n_tok, issue, 0, unroll=MOE_DMA_UNROLL)
    for k in range(2):
        pltpu.make_async_copy(ys_hbm.at[pl.ds(0, n_tok), :], buf_ref.at[k], sem).wait()
    gates = rf_ref[...]
    out_ref[...] = x_ref[...] + (buf_ref[0] * gates[:, 0:1] + buf_ref[1] * gates[:, 1:2])


def _combine(dest_flat, x2, rf, ys):
    t = x2.shape[0]
    tc = MOE_COMBINE_TILE
    return pl.pallas_call(
        _combine_kernel, grid=(t // tc,),
        in_specs=[pl.BlockSpec((2 * tc,), lambda i: (i,), memory_space=pltpu.SMEM),
                  pl.BlockSpec((tc, D_MODEL), lambda i: (i, 0)), pl.BlockSpec((tc, LANES), lambda i: (i, 0)),
                  pl.BlockSpec(memory_space=pl.ANY)],
        out_specs=pl.BlockSpec((tc, D_MODEL), lambda i: (i, 0)),
        out_shape=jax.ShapeDtypeStruct((t, D_MODEL), F32),
        scratch_shapes=[pltpu.VMEM((2, tc, D_MODEL), F32), pltpu.SemaphoreType.DMA(())],
        compiler_params=_params(1), name="moe_combine")(dest_flat, x2, rf, ys)


def _moe(x2, g, wr, wg, wu, wd, ltri):
    t = x2.shape[0]
    bm = MOE_SLOT_BLOCK
    n_blocks = (2 * t) // bm + N_EXPERTS
    ri, rf, cnt = _router(x2, g, wr, ltri)
    counts = cnt[0, :N_EXPERTS].astype(I32)
    padded = (counts + bm - 1) // bm * bm
    pend = jnp.cumsum(padded)
    pstart = pend - padded
    dest = (pstart[ri[:, 0:2]] + ri[:, 2:4]).reshape(2 * t)
    n_used = (pend[-1:] // bm).astype(I32)
    starts = jnp.arange(n_blocks, dtype=I32) * bm
    block_exp = jnp.minimum(jnp.sum((pend[None, :] <= starts[:, None]).astype(I32), axis=1), N_EXPERTS - 1)
    xs = _dispatch(dest, x2, n_blocks * bm)
    ys = _experts(block_exp, n_used, xs, g, wg, wu, wd)
    return _combine(dest, x2, rf, ys)


def _const_mats():
    r = np.arange(256)
    gmat = ((r[:, None] // 64) == (r[None, :] // 64)).astype(np.float32) / 64.0
    d = r % 64
    same = (r[:, None] // 64) == (r[None, :] // 64)
    swap = ((d[None, :] < 8) & (d[:, None] == d[None, :] + 8)) | \
           ((d[None, :] >= 8) & (d[None, :] < 16) & (d[:, None] == d[None, :] - 8))
    pmat = (same & swap).astype(np.float32)
    tri = np.tril(np.ones((SSD_CHUNK, SSD_CHUNK), np.float32))
    rsel = np.zeros((LANES, D_SSD), np.float32)
    rep = np.zeros((D_SSD, SSD_HEADS * SSD_CHUNK), np.float32)
    sel8 = np.zeros((SSD_HEADS, D_SSD), np.float32)
    for h in range(SSD_HEADS):
        rsel[IDX_DIM + h, h * 64:(h + 1) * 64] = 1.0
        rep[h * 64, h * SSD_CHUNK:(h + 1) * SSD_CHUNK] = 1.0
        sel8[h, h * 64] = 1.0
    tril = np.tril(np.ones((KEY_TILE, KEY_TILE), np.float32), -1)
    ltri = np.tril(np.ones((TOKEN_TILE, TOKEN_TILE), np.float32), -1)
    to = lambda a: jnp.asarray(a, dtype=BF16)
    return to(gmat), to(pmat), (to(tri), to(rsel), to(rep), to(sel8)), to(tril), to(ltri)


def kernel(x, mem, positions, mix_norm_g, w_in, conv_dw_w, conv_ln_g, conv_ln_b, conv_pw_w, ssd_conv_w, ssd_conv_b, ssd_dt_bias, ssd_a_log, ssd_d, ssd_norm_g, att_q_norm_g, att_k_norm_g, w_out, memx_norm_g, mem_norm_g, memx_w_q, memx_w_kv, memx_q_norm_g, memx_k_norm_g, memx_w_o, ffn_norm_g, ffn_w_gate, ffn_w_up, ffn_w_down, moe_w_router, moe_w_gate, moe_w_up, moe_w_down):
    batch, seq, d = x.shape
    nm = mem.shape[1]
    depth = w_in.shape[0]
    t = batch * seq
    assert d == D_MODEL and seq % TOKEN_TILE == 0 and seq % SEQ_TILE == 0 and seq % KEY_TILE == 0
    assert TOKEN_TILE % KEY_TILE == 0 and KEY_TILE % Q_TILE == 0 and D_FF % FF_CHUNK == 0

    gmat, pmat, ssd_consts, tril, ltri = _const_mats()

    inv = ROPE_THETA ** (-jnp.arange(0, ROT_DIM, 2, dtype=F32) / ROT_DIM)
    ang = positions.astype(F32).reshape(t, 1) * inv
    cos, sin = jnp.cos(ang), jnp.sin(ang)
    c64 = jnp.concatenate([cos, cos, jnp.ones((t, 48), F32)], axis=1)
    s64 = jnp.concatenate([-sin, sin, jnp.zeros((t, 48), F32)], axis=1)
    c128 = jnp.concatenate([c64, c64], axis=1)
    s128 = jnp.concatenate([s64, s64], axis=1)
    ct, st = cos.T, sin.T

    x2 = x.reshape(t, d)
    mem2 = mem.reshape(batch * nm, d)
    row = lambda v: v.reshape(1, -1)
    rep64 = lambda v: jnp.repeat(v, SSD_HEAD_DIM).reshape(1, -1)
    o = _OFF

    for i in range(depth):
        w = w_in[i]
        wstd = jnp.concatenate([w[:, o[0]:o[3]], w[:, o[5]:o[6]], w[:, o[8]:o[9]], w[:, o[3]:o[4]],
                                jnp.zeros((d, LANES - IDX_DIM - SSD_HEADS), F32)], axis=1).astype(BF16)
        wtrn = jnp.concatenate([w[:, o[7]:o[8]].T, w[:, o[4]:o[5]].T, w[:, o[6]:o[7]].T, w[:, o[9]:o[10]].T,
                                jnp.zeros((N_TRN - 1032, d), F32)], axis=0).astype(BF16)
        uconv, z, xbc, kn, small, kib, qit, qnt, vt, wt = _in_proj(
            x2, row(mix_norm_g[i]), wstd, wtrn, c128, s128, ct, st, att_q_norm_g[i].reshape(64, 1),
            jnp.tile(att_k_norm_g[i], ATT_HEADS).reshape(1, 256), gmat, pmat)

        dw = jnp.concatenate([conv_dw_w[i], jnp.zeros((32 - CONV_WIDTH, D_CONV), F32)], axis=0)
        y_conv = _conformer_conv(uconv, dw, row(conv_ln_g[i]), row(conv_ln_b[i]), conv_pw_w[i].astype(BF16),
                                 batch, seq)
        cw = jnp.concatenate([ssd_conv_w[i], jnp.zeros((8 - SSD_CONV, D_XBC), F32)], axis=0)
        y_ssd = _ssd(z, xbc, small, cw, row(ssd_conv_b[i]), rep64(ssd_dt_bias[i]), rep64(ssd_a_log[i]),
                     rep64(ssd_d[i]), row(ssd_norm_g[i]), ssd_consts, batch, seq)
        y_att = _dsa(qit, qnt, wt, kib, kn, vt, tril, batch, seq)

        kbd, vbd = _mem_prep(mem2, row(mem_norm_g[i]), memx_w_kv[i].astype(BF16),
                             jnp.tile(memx_k_norm_g[i], MEM_HEADS).reshape(1, 256), gmat, batch, nm)
        x2 = _mix_memx(x2, y_conv, y_ssd, y_att, w_out[i].astype(BF16), row(memx_norm_g[i]),
                       memx_w_q[i].astype(BF16), jnp.tile(memx_q_norm_g[i], MEM_HEADS).reshape(1, 256), gmat,
                       kbd, vbd, memx_w_o[i].astype(BF16), seq)

        j = i // 2
        if i % 2 == 0:
            x2 = _ffn(x2, row(ffn_norm_g[i]), ffn_w_gate[j].astype(BF16), ffn_w_up[j].astype(BF16),
                      ffn_w_down[j].astype(BF16))
        else:
            wr = jnp.concatenate([moe_w_router[j], jnp.zeros((d, LANES - N_EXPERTS), F32)], axis=1)
            x2 = _moe(x2, row(ffn_norm_g[i]), wr, moe_w_gate[j].astype(BF16), moe_w_up[j].astype(BF16),
                      moe_w_down[j].astype(BF16), ltri)
    return x2.reshape(batch, seq, d)
```

```python
import functools
import math

import numpy as np
import jax
import jax.numpy as jnp
from jax import lax
from jax.experimental import pallas as pl
from jax.experimental.pallas import tpu as pltpu

F32 = jnp.float32
BF16 = jnp.bfloat16
I32 = jnp.int32

EPS = 1e-6
D_MODEL = 1024
D_CONV = 256
CONV_WIDTH = 31
SSD_HEAD_DIM = 64
D_SSD = 512
SSD_HEADS = 8
SSD_GROUPS = 2
SSD_STATE = 128
SSD_CONV = 4
SSD_CHUNK = 128
D_XBC = D_SSD + 2 * SSD_GROUPS * SSD_STATE
ATT_HEAD_DIM = 64
D_ATT = 256
ATT_HEADS = 4
IDX_HEADS = 8
IDX_DIM = 64
TOPK_MAX = 256
ROPE_THETA = 500000.0
ROT_DIM = ATT_HEAD_DIM // 4
MEM_HEADS = 4
MEM_HEAD_DIM = 64
D_MEMATT = 256
D_FF = 2816
N_EXPERTS = 8

LANES = 128
SUBLANES = 8
VMEM_LIMIT_BYTES = 56 * 1024 * 1024

TOKEN_TILE = 512
SEQ_TILE = 512
Q_TILE = 256
KEY_TILE = 256
FF_CHUNK = 1408
MOE_SLOT_BLOCK = 512
MOE_DISPATCH_TILE = 512
MOE_COMBINE_TILE = 256
MOE_DMA_UNROLL = 8
CONV_ROWS = 64
CONV_HALO = 32
SSD_HALO = 8
INT_MIN = -2 ** 31
NEG_INF = float("-inf")

_SZ = [2 * D_CONV, D_SSD, D_XBC, SSD_HEADS, D_ATT, D_ATT, D_ATT, IDX_HEADS * IDX_DIM, IDX_DIM, IDX_HEADS]
_OFF = np.concatenate([[0], np.cumsum(_SZ)]).tolist()
N_STD = 2 * D_CONV + D_SSD + D_XBC + D_ATT + LANES
N_TRN = IDX_HEADS * IDX_DIM + D_ATT + D_ATT + 16


def _mm(a, b):
    return jnp.dot(a, b, preferred_element_type=F32)


def _mm_nt(a, b):
    return lax.dot_general(a, b, (((1,), (1,)), ((), ())), preferred_element_type=F32)


def _split3(a):
    hi = a.astype(BF16)
    r = a - hi.astype(F32)
    mid = r.astype(BF16)
    lo = (r - mid.astype(F32)).astype(BF16)
    return hi, mid, lo


def _mm_f32_lhs(a, b_exact):
    hi, mid, lo = _split3(a)
    return (_mm(hi, b_exact) + _mm(mid, b_exact)) + _mm(lo, b_exact)


def _mm_f32_rhs(a_exact, b):
    hi, mid, lo = _split3(b)
    return (_mm(a_exact, hi) + _mm(a_exact, mid)) + _mm(a_exact, lo)


def _mm_nt_f32_rhs(a_exact, b):
    hi, mid, lo = _split3(b)
    return (_mm_nt(a_exact, hi) + _mm_nt(a_exact, mid)) + _mm_nt(a_exact, lo)


def _mm_f32_f32(a, b):
    a1, a2, a3 = _split3(a)
    b1, b2, b3 = _split3(b)
    return (_mm(a1, b1) + (_mm(a1, b2) + _mm(a2, b1))) + ((_mm(a1, b3) + _mm(a2, b2)) + _mm(a3, b1))


def _silu(x):
    return x * jax.nn.sigmoid(x)


def _rms_rows(x, g):
    ms = jnp.mean(x * x, axis=-1, keepdims=True)
    return (x * lax.rsqrt(ms + EPS)) * g


def _fold_rows(x, op):
    parts = [x[r:r + SUBLANES] for r in range(0, x.shape[0], SUBLANES)]
    while len(parts) > 1:
        nxt = [op(parts[k], parts[k + 1]) for k in range(0, len(parts) - 1, 2)]
        if len(parts) % 2:
            nxt.append(parts[-1])
        parts = nxt
    return parts[0]


def _params(n_axes):
    return pltpu.CompilerParams(dimension_semantics=("arbitrary",) * n_axes,
                                vmem_limit_bytes=VMEM_LIMIT_BYTES)


def _full(shape):
    n = len(shape)
    return pl.BlockSpec(shape, lambda *_: (0,) * n)


def _in_proj_kernel(x_ref, g_ref, wstd_ref, wtrn_ref, c128_ref, s128_ref, ct_ref, st_ref, qg_ref, kg_ref,
                    gmat_ref, pmat_ref,
                    uconv_ref, z_ref, xbc_ref, kn_ref, small_ref, kib_ref, qit_ref, qnt_ref, vt_ref, wt_ref):
    tm = x_ref.shape[0]
    hb = _rms_rows(x_ref[...], g_ref[...]).astype(BF16)
    uconv_ref[...] = _mm(hb, wstd_ref[:, 0:512])
    z_ref[...] = _mm(hb, wstd_ref[:, 512:1024])
    xbc_ref[...] = _mm(hb, wstd_ref[:, 1024:2048])

    c128 = c128_ref[...]
    s128 = s128_ref[...]
    kraw = _mm(hb, wstd_ref[:, 2048:2304])
    kn = (kraw * lax.rsqrt(_mm_f32_lhs(kraw * kraw, gmat_ref[...]) + EPS)) * kg_ref[...]
    kn = kn * jnp.concatenate([c128, c128], axis=1) + \
        _mm_f32_lhs(kn, pmat_ref[...]) * jnp.concatenate([s128, s128], axis=1)
    kn_ref[...] = kn.astype(BF16)

    sm = _mm(hb, wstd_ref[:, 2304:2432])
    lane = lax.broadcasted_iota(I32, (tm, LANES), 1)
    is_idx = lane < IDX_DIM
    sm = sm * jnp.where(is_idx, c128, 1.0) + _mm_f32_lhs(sm, pmat_ref[0:LANES, 0:LANES]) * jnp.where(is_idx, s128, 0.0)
    small_ref[...] = sm
    kib_ref[...] = jnp.where(is_idx, sm, 0.0).astype(BF16)

    out_t = _mm_nt(wtrn_ref[...], hb)
    ct = ct_ref[...]
    st = st_ref[...]

    def rope_t(blk):
        x1 = blk[0:8]
        x2 = blk[8:16]
        return jnp.concatenate([x1 * ct - x2 * st, x2 * ct + x1 * st, blk[16:64]], axis=0)

    qi = [rope_t(out_t[h * 64:(h + 1) * 64]) for h in range(IDX_HEADS)]
    qit_ref[...] = jnp.concatenate(qi, axis=0).astype(BF16)

    qn = []
    for h in range(ATT_HEADS):
        blk = out_t[512 + h * 64:512 + (h + 1) * 64]
        ms = jnp.mean(blk * blk, axis=0, keepdims=True)
        blk = (blk * lax.rsqrt(ms + EPS)) * qg_ref[...]
        qn.append(rope_t(blk) * (ATT_HEAD_DIM ** -0.5))
    qnt_ref[...] = jnp.concatenate(qn, axis=0).astype(BF16)

    vt = out_t[768:1024].astype(BF16)
    for c in range(tm // KEY_TILE):
        vt_ref[c] = vt[:, c * KEY_TILE:(c + 1) * KEY_TILE]
    wt_ref[...] = out_t[1024:1032] * (IDX_HEADS ** -0.5 * IDX_DIM ** -0.5)


def _in_proj(x2, g, wstd, wtrn, c128, s128, ct, st, qg, kg, gmat, pmat):
    t = x2.shape[0]
    tm = TOKEN_TILE
    row = lambda w: pl.BlockSpec((tm, w), lambda i: (i, 0))
    col = lambda r: pl.BlockSpec((r, tm), lambda i: (0, i))
    out_shape = (
        jax.ShapeDtypeStruct((t, 512), F32), jax.ShapeDtypeStruct((t, 512), F32),
        jax.ShapeDtypeStruct((t, 1024), F32), jax.ShapeDtypeStruct((t, 256), BF16),
        jax.ShapeDtypeStruct((t, LANES), F32), jax.ShapeDtypeStruct((t, LANES), BF16),
        jax.ShapeDtypeStruct((512, t), BF16), jax.ShapeDtypeStruct((256, t), BF16),
        jax.ShapeDtypeStruct((t // KEY_TILE, 256, KEY_TILE), BF16), jax.ShapeDtypeStruct((8, t), F32))
    out_specs = (row(512), row(512), row(1024), row(256), row(LANES), row(LANES), col(512), col(256),
                 pl.BlockSpec((tm // KEY_TILE, 256, KEY_TILE), lambda i: (i, 0, 0)), col(8))
    in_specs = [row(D_MODEL), _full((1, D_MODEL)), _full(wstd.shape), _full(wtrn.shape), row(LANES), row(LANES),
                col(8), col(8), _full((64, 1)), _full((1, 256)), _full((256, 256)), _full((256, 256))]
    return pl.pallas_call(_in_proj_kernel, grid=(t // tm,), in_specs=in_specs, out_specs=out_specs,
                          out_shape=out_shape, compiler_params=_params(1), name="in_proj")(
        x2, g, wstd, wtrn, c128, s128, ct, st, qg, kg, gmat, pmat)


def _conv_kernel(u_ref, dw_ref, lg_ref, lb_ref, pw_ref, out_ref, hp_ref, hs_ref):
    ts = u_ref.shape[0]

    @pl.when(pl.program_id(1) == 0)
    def _():
        hp_ref[0:CONV_HALO, :] = jnp.zeros((CONV_HALO, D_CONV), F32)

    u = u_ref[...]
    hp_ref[CONV_HALO:CONV_HALO + ts, :] = u[:, :D_CONV] * jax.nn.sigmoid(u[:, D_CONV:])
    n_sh = hs_ref.shape[1]
    for s in range(1, SUBLANES):
        hs_ref[s - 1] = hp_ref[pl.ds(s, n_sh), :]
    base = CONV_HALO - (CONV_WIDTH - 1)

    def tap(r0, k):
        a, s = divmod(base + k, SUBLANES)
        if s == 0:
            return hp_ref[pl.ds(r0 + a * SUBLANES, CONV_ROWS), :]
        return hs_ref[s - 1, pl.ds(r0 + a * SUBLANES, CONV_ROWS), :]

    for r0 in range(0, ts, CONV_ROWS):
        acc = dw_ref[0:1, :] * tap(r0, 0)
        for k in range(1, CONV_WIDTH):
            acc = acc + dw_ref[k:k + 1, :] * tap(r0, k)
        mu = jnp.mean(acc, axis=-1, keepdims=True)
        xc = acc - mu
        y = xc * lax.rsqrt(jnp.mean(xc * xc, axis=-1, keepdims=True) + EPS)
        y = _silu(y * lg_ref[...] + lb_ref[...])
        out_ref[pl.ds(r0, CONV_ROWS), :] = _mm(y.astype(BF16), pw_ref[...])
    hp_ref[0:CONV_HALO, :] = hp_ref[ts:ts + CONV_HALO, :]


def _conformer_conv(uconv, dw, lg, lb, pw, batch, seq):
    ts = SEQ_TILE
    ns = seq // ts
    return pl.pallas_call(
        _conv_kernel, grid=(batch, ns),
        in_specs=[pl.BlockSpec((ts, 512), lambda b, s: (b * ns + s, 0)), _full((32, D_CONV)), _full((1, D_CONV)),
                  _full((1, D_CONV)), _full((D_CONV, D_CONV))],
        out_specs=pl.BlockSpec((ts, D_CONV), lambda b, s: (b * ns + s, 0)),
        out_shape=jax.ShapeDtypeStruct((batch * seq, D_CONV), F32),
        scratch_shapes=[pltpu.VMEM((CONV_HALO + ts, D_CONV), F32),
                        pltpu.VMEM((SUBLANES - 1, CONV_HALO + ts - SUBLANES, D_CONV), F32)],
        compiler_params=_params(2), name="conformer_conv")(uconv, dw, lg, lb, pw)


def _ssd_kernel(z_ref, xbc_ref, small_ref, cw_ref, cb_ref, dtb_ref, alog_ref, dskip_ref, ng_ref,
                tri_ref, rsel_ref, rep_ref, sel8_ref, out_ref, xp_ref, state_ref):
    ts = z_ref.shape[0]
    L = SSD_CHUNK

    @pl.when(pl.program_id(1) == 0)
    def _():
        xp_ref[0:SSD_HALO, :] = jnp.zeros((SSD_HALO, D_XBC), F32)
        state_ref[...] = jnp.zeros(state_ref.shape, F32)

    xp_ref[SSD_HALO:SSD_HALO + ts, :] = xbc_ref[...]
    lane1 = lax.broadcasted_iota(I32, (1, LANES), 1)
    is_dt = (lane1 >= IDX_DIM) & (lane1 < IDX_DIM + SSD_HEADS)
    a_neg = jnp.where(is_dt, -jnp.exp(alog_ref[...]), 0.0)
    ii = lax.broadcasted_iota(I32, (L, L), 0)
    jj = lax.broadcasted_iota(I32, (L, L), 1)
    causal = ii >= jj
    first_half = jj < SSD_HEAD_DIM
    gw = D_SSD // SSD_GROUPS

    for c in range(ts // L):
        r0 = c * L
        base = r0 + SSD_HALO - (SSD_CONV - 1)
        conv = cb_ref[...] + cw_ref[0:1, :] * xp_ref[pl.ds(base, L), :]
        for k in range(1, SSD_CONV):
            conv = conv + cw_ref[k:k + 1, :] * xp_ref[pl.ds(base + k, L), :]
        xa = _silu(conv)
        xs = xa[:, 0:D_SSD]
        bm = xa[:, D_SSD:D_SSD + 256]
        cm = xa[:, D_SSD + 256:D_SSD + 512]

        dt_arg = small_ref[pl.ds(r0, L), :] + dtb_ref[...]
        dt_s = jnp.maximum(dt_arg, 0.0) + jnp.log1p(jnp.exp(-jnp.abs(dt_arg)))
        acs_s = _mm_f32_rhs(tri_ref[...], dt_s * a_neg)
        dt = _mm_f32_lhs(dt_s, rsel_ref[...])
        acs = _mm_f32_lhs(acs_s, rsel_ref[...])
        acs_wide = _mm_f32_lhs(acs_s, rep_ref[...])
        acs_t = _mm_nt_f32_rhs(sel8_ref[...], acs_s)
        last = acs[L - 1:L, :]
        decay_end = jnp.exp(last - acs)
        decay_in = jnp.exp(acs)
        xd = xs * dt
        xdb = xd.astype(BF16)

        y_parts = []
        for g in range(SSD_GROUPS):
            bg = bm[:, g * SSD_STATE:(g + 1) * SSD_STATE]
            cgb = cm[:, g * SSD_STATE:(g + 1) * SSD_STATE].astype(BF16)
            cb = _mm_nt(cgb, bg.astype(BF16))
            pair_out = []
            for pr in range(2):
                h0 = g * 4 + pr * 2
                xpair = xdb[:, h0 * 64:(h0 + 2) * 64]
                outs = []
                for h in (h0, h0 + 1):
                    seg = acs_wide[:, h * L:(h + 1) * L] - acs_t[h:h + 1, :]
                    m_h = (cb * jnp.exp(jnp.where(causal, seg, NEG_INF))).astype(BF16)
                    outs.append(_mm(m_h, xpair))
                pair_out.append(jnp.where(first_half, outs[0], outs[1]))
            y_diag = jnp.concatenate(pair_out, axis=1)
            gs = slice(g * gw, (g + 1) * gw)
            st = state_ref[:, gs]
            y_off = _mm(cgb, st.astype(BF16)) * decay_in[:, gs]
            contrib = _mm(bg.T.astype(BF16), (xd[:, gs] * decay_end[:, gs]).astype(BF16))
            state_ref[:, gs] = jnp.exp(last[:, gs]) * st + contrib
            y_parts.append(y_diag + y_off)
        y = jnp.concatenate(y_parts, axis=1) + xs * dskip_ref[...]
        y = y * _silu(z_ref[pl.ds(r0, L), :])
        outs = []
        for g in range(SSD_GROUPS):
            gs = slice(g * gw, (g + 1) * gw)
            outs.append(_rms_rows(y[:, gs], ng_ref[:, gs]))
        out_ref[pl.ds(r0, L), :] = jnp.concatenate(outs, axis=1)
    xp_ref[0:SSD_HALO, :] = xp_ref[ts:ts + SSD_HALO, :]


def _ssd(z, xbc, small, cw, cb, dtb, alog, dskip, ng, consts, batch, seq):
    ts = SEQ_TILE
    ns = seq // ts
    tri, rsel, rep, sel8 = consts
    row = lambda w: pl.BlockSpec((ts, w), lambda b, s: (b * ns + s, 0))
    return pl.pallas_call(
        _ssd_kernel, grid=(batch, ns),
        in_specs=[row(512), row(1024), row(LANES), _full((8, D_XBC)), _full((1, D_XBC)), _full((1, LANES)),
                  _full((1, LANES)), _full((1, 512)), _full((1, 512)), _full(tri.shape), _full(rsel.shape),
                  _full(rep.shape), _full(sel8.shape)],
        out_specs=row(512),
        out_shape=jax.ShapeDtypeStruct((batch * seq, 512), F32),
        scratch_shapes=[pltpu.VMEM((SSD_HALO + ts, D_XBC), F32), pltpu.VMEM((SSD_STATE, D_SSD), F32)],
        compiler_params=_params(2), name="ssd_mixer")(z, xbc, small, cw, cb, dtb, alog, dskip, ng, tri, rsel, rep, sel8)


def _dsa_kernel(qit_ref, qnt_ref, wt_ref, kib_ref, kn_ref, vt_ref, tril_ref, out_ref, keys_ref, s_ref, acc_ref,
                *, ksel):
    qb = Q_TILE
    kb = KEY_TILE
    i = pl.program_id(1)
    nsb = (i * qb + qb + kb - 1) // kb
    qpos = i * qb + lax.broadcasted_iota(I32, (kb, qb), 1)
    krow = lax.broadcasted_iota(I32, (kb, qb), 0)

    rhs_idx = jnp.concatenate([qit_ref[h * IDX_DIM:(h + 1) * IDX_DIM, :] for h in range(IDX_HEADS)], axis=1)
    rhs_idx = jnp.concatenate([rhs_idx, jnp.zeros((LANES - IDX_DIM, IDX_HEADS * qb), BF16)], axis=0)
    wt = wt_ref[...]

    def score_tile(j, carry):
        r = pl.multiple_of(j * kb, kb)
        lg = _mm(kib_ref[pl.ds(r, kb), :], rhs_idx)
        sc = wt[0:1, :] * jnp.maximum(lg[:, 0:qb], 0.0)
        for h in range(1, IDX_HEADS):
            sc = sc + wt[h:h + 1, :] * jnp.maximum(lg[:, h * qb:(h + 1) * qb], 0.0)
        keys_ref[pl.ds(r, kb), :] = jnp.where(j * kb + krow <= qpos, sc, NEG_INF)
        return carry

    lax.fori_loop(0, nsb, score_tile, 0)

    def ordered_to_float(u):
        k = u ^ INT_MIN
        return pltpu.bitcast(k ^ (lax.shift_right_arithmetic(k, 31) & 0x7FFFFFFF), F32)

    def count(cand, strict):
        def body(j, acc):
            kt = keys_ref[pl.ds(pl.multiple_of(j * kb, kb), kb), :]
            hit = (kt > cand) if strict else (kt >= cand)
            return acc + _fold_rows(jnp.where(hit, 1.0, 0.0), jnp.add)
        acc = lax.fori_loop(0, nsb, body, jnp.zeros((SUBLANES, qb), F32))
        return jnp.sum(acc, axis=0, keepdims=True)

    def bit_step(it, tu):
        cand_u = tu | lax.shift_left(jnp.int32(1), 31 - it)
        cnt = count(ordered_to_float(cand_u)[0:1, :], False)
        return jnp.where(cnt >= float(ksel), cand_u, tu)

    tau = ordered_to_float(lax.fori_loop(0, 32, bit_step, jnp.zeros((SUBLANES, qb), I32)))[0:1, :]
    takes_all = qpos[0:1, :] < ksel
    tau = jnp.where(takes_all, NEG_INF, tau)
    need = jnp.where(takes_all, 0.0, float(ksel) - count(tau, True))

    zblk = jnp.zeros((ATT_HEAD_DIM, qb), BF16)
    rhs_att = jnp.concatenate(
        [jnp.concatenate([qnt_ref[h * 64:(h + 1) * 64, :] if hh == h else zblk for hh in range(ATT_HEADS)], axis=1)
         for h in range(ATT_HEADS)], axis=0)

    def pass1(j, carry):
        m, eq_before = carry
        r = pl.multiple_of(j * kb, kb)
        s = _mm(kn_ref[pl.ds(r, kb), :], rhs_att)
        kt = keys_ref[pl.ds(r, kb), :]
        eq = kt == tau
        eqf = jnp.where(eq, 1.0, 0.0)
        rank = eq_before + _mm(tril_ref[...], eqf.astype(BF16))
        sel = (kt > tau) | (eq & (rank < need))
        s = jnp.concatenate([jnp.where(sel, s[:, h * qb:(h + 1) * qb], NEG_INF) for h in range(ATT_HEADS)], axis=1)
        s_ref[pl.ds(r, kb), :] = s
        return (jnp.maximum(m, _fold_rows(s, jnp.maximum)),
                eq_before + jnp.sum(_fold_rows(eqf, jnp.add), axis=0, keepdims=True))

    m8, _ = lax.fori_loop(0, nsb, pass1,
                          (jnp.full((SUBLANES, ATT_HEADS * qb), NEG_INF, F32), jnp.zeros((1, qb), F32)))
    m = jnp.max(m8, axis=0, keepdims=True)

    acc_ref[...] = jnp.zeros(acc_ref.shape, F32)

    def pass2(j, l8):
        p = jnp.exp(s_ref[pl.ds(pl.multiple_of(j * kb, kb), kb), :] - m)
        acc_ref[...] += _mm(vt_ref[j], p.astype(BF16))
        return l8 + _fold_rows(p, jnp.add)

    l = jnp.sum(lax.fori_loop(0, nsb, pass2, jnp.zeros((SUBLANES, ATT_HEADS * qb), F32)), axis=0, keepdims=True)
    o_t = jnp.concatenate(
        [acc_ref[h * 64:(h + 1) * 64, h * qb:(h + 1) * qb] / l[:, h * qb:(h + 1) * qb] for h in range(ATT_HEADS)],
        axis=0)
    out_ref[...] = o_t.T


def _dsa(qit, qnt, wt, kib, kn, vt, tril, batch, seq):
    qb = Q_TILE
    nq = seq // qb
    nkb = seq // KEY_TILE
    ksel = min(TOPK_MAX, seq // 4)
    colq = lambda r: pl.BlockSpec((r, qb), lambda b, i: (0, b * nq + i))
    return pl.pallas_call(
        functools.partial(_dsa_kernel, ksel=ksel), grid=(batch, nq),
        in_specs=[colq(512), colq(256), colq(8),
                  pl.BlockSpec((seq, LANES), lambda b, i: (b, 0)), pl.BlockSpec((seq, 256), lambda b, i: (b, 0)),
                  pl.BlockSpec((nkb, 256, KEY_TILE), lambda b, i: (b, 0, 0)), _full((KEY_TILE, KEY_TILE))],
        out_specs=pl.BlockSpec((qb, D_ATT), lambda b, i: (b * nq + i, 0)),
        out_shape=jax.ShapeDtypeStruct((batch * seq, D_ATT), F32),
        scratch_shapes=[pltpu.VMEM((seq, qb), F32), pltpu.VMEM((seq, ATT_HEADS * qb), F32),
                        pltpu.VMEM((D_ATT, ATT_HEADS * qb), F32)],
        compiler_params=_params(2), name="dsa_attention")(qit, qnt, wt, kib, kn, vt, tril)


def _mem_prep_kernel(mem_ref, g_ref, wkv_ref, kg_ref, gmat_ref, kbd_ref, vbd_ref):
    nm = mem_ref.shape[0]
    mb = _rms_rows(mem_ref[...], g_ref[...]).astype(BF16)
    kv = _mm(mb, wkv_ref[...])
    k = kv[:, 0:D_MEMATT]
    kn = (k * lax.rsqrt(_mm_f32_lhs(k * k, gmat_ref[...]) + EPS)) * kg_ref[...]
    k_t = kn.T.astype(BF16)
    zero = jnp.zeros((MEM_HEAD_DIM, nm), BF16)
    kbd_ref[...] = jnp.concatenate(
        [jnp.concatenate([k_t[h * 64:(h + 1) * 64, :] if hh == h else zero for hh in range(MEM_HEADS)], axis=1)
         for h in range(MEM_HEADS)], axis=0)
    v = kv[:, D_MEMATT:]
    lane = lax.broadcasted_iota(I32, (nm, D_MEMATT), 1)
    vbd_ref[...] = jnp.concatenate(
        [jnp.where((lane >= h * 64) & (lane < (h + 1) * 64), v, 0.0) for h in range(MEM_HEADS)],
        axis=0).astype(BF16)


def _mem_prep(mem2, g, wkv, kg, gmat, batch, nm):
    return pl.pallas_call(
        _mem_prep_kernel, grid=(batch,),
        in_specs=[pl.BlockSpec((nm, D_MODEL), lambda b: (b, 0)), _full((1, D_MODEL)), _full((D_MODEL, 512)),
                  _full((1, 256)), _full((256, 256))],
        out_specs=(pl.BlockSpec((None, D_MEMATT, MEM_HEADS * nm), lambda b: (b, 0, 0)),
                   pl.BlockSpec((None, MEM_HEADS * nm, D_MEMATT), lambda b: (b, 0, 0))),
        out_shape=(jax.ShapeDtypeStruct((batch, D_MEMATT, MEM_HEADS * nm), BF16),
                   jax.ShapeDtypeStruct((batch, MEM_HEADS * nm, D_MEMATT), BF16)),
        compiler_params=_params(1), name="mem_prep")(mem2, g, wkv, kg, gmat)


def _mix_memx_kernel(x_ref, yc_ref, ys_ref, ya_ref, wout_ref, g_ref, wq_ref, qg_ref, gmat_ref, kbd_ref, vbd_ref,
                     wo_ref, out_ref):
    nm = kbd_ref.shape[1] // MEM_HEADS
    y_mix = jnp.concatenate([yc_ref[...], ys_ref[...], ya_ref[...]], axis=1).astype(BF16)
    x1 = x_ref[...] + _mm(y_mix, wout_ref[...])
    hb = _rms_rows(x1, g_ref[...]).astype(BF16)
    q = _mm(hb, wq_ref[...])
    qn = (q * lax.rsqrt(_mm_f32_lhs(q * q, gmat_ref[...]) + EPS)) * qg_ref[...]
    s = _mm(qn.astype(BF16), kbd_ref[...]) * (MEM_HEAD_DIM ** -0.5)
    ps = []
    for h in range(MEM_HEADS):
        sh = s[:, h * nm:(h + 1) * nm]
        e = jnp.exp(sh - jnp.max(sh, axis=-1, keepdims=True))
        ps.append(e / jnp.sum(e, axis=-1, keepdims=True))
    o = _mm(jnp.concatenate(ps, axis=1).astype(BF16), vbd_ref[...])
    out_ref[...] = x1 + _mm(o.astype(BF16), wo_ref[...])


def _mix_memx(x2, yc, ys, ya, wout, g, wq, qg, gmat, kbd, vbd, wo, seq):
    t = x2.shape[0]
    tm = TOKEN_TILE
    per_b = seq // tm
    nm4 = kbd.shape[2]
    row = lambda w: pl.BlockSpec((tm, w), lambda i: (i, 0))
    return pl.pallas_call(
        _mix_memx_kernel, grid=(t // tm,),
        in_specs=[row(D_MODEL), row(256), row(512), row(256), _full((D_MODEL, D_MODEL)), _full((1, D_MODEL)),
                  _full((D_MODEL, 256)), _full((1, 256)), _full((256, 256)),
                  pl.BlockSpec((None, D_MEMATT, nm4), lambda i: (i // per_b, 0, 0)),
                  pl.BlockSpec((None, nm4, D_MEMATT), lambda i: (i // per_b, 0, 0)), _full((256, D_MODEL))],
        out_specs=row(D_MODEL), out_shape=jax.ShapeDtypeStruct((t, D_MODEL), F32),
        compiler_params=_params(1), name="mix_memx")(x2, yc, ys, ya, wout, g, wq, qg, gmat, kbd, vbd, wo)


def _ffn_kernel(x_ref, g_ref, wg_ref, wu_ref, wd_ref, out_ref, hb_ref, acc_ref):
    c = pl.program_id(1)

    @pl.when(c == 0)
    def _():
        hb_ref[...] = _rms_rows(x_ref[...], g_ref[...]).astype(BF16)
        acc_ref[...] = jnp.zeros(acc_ref.shape, F32)

    hb = hb_ref[...]
    a = _silu(_mm(hb, wg_ref[...])) * _mm(hb, wu_ref[...])
    acc_ref[...] += _mm(a.astype(BF16), wd_ref[...])

    @pl.when(c == pl.num_programs(1) - 1)
    def _():
        out_ref[...] = x_ref[...] + acc_ref[...]


def _ffn(x2, g, wg, wu, wd):
    t = x2.shape[0]
    tm = TOKEN_TILE
    fc = FF_CHUNK
    return pl.pallas_call(
        _ffn_kernel, grid=(t // tm, D_FF // fc),
        in_specs=[pl.BlockSpec((tm, D_MODEL), lambda i, c: (i, 0)), _full((1, D_MODEL)),
                  pl.BlockSpec((D_MODEL, fc), lambda i, c: (0, c)), pl.BlockSpec((D_MODEL, fc), lambda i, c: (0, c)),
                  pl.BlockSpec((fc, D_MODEL), lambda i, c: (c, 0))],
        out_specs=pl.BlockSpec((tm, D_MODEL), lambda i, c: (i, 0)),
        out_shape=jax.ShapeDtypeStruct((t, D_MODEL), F32),
        scratch_shapes=[pltpu.VMEM((tm, D_MODEL), BF16), pltpu.VMEM((tm, D_MODEL), F32)],
        compiler_params=_params(2), name="ffn_dense")(x2, g, wg, wu, wd)


def _router_kernel(x_ref, g_ref, wr_ref, ltri_ref, ri_ref, rf_ref, cnt_ref, carry_ref):
    tm = x_ref.shape[0]
    lane = lax.broadcasted_iota(I32, (tm, LANES), 1)

    @pl.when(pl.program_id(0) == 0)
    def _():
        carry_ref[...] = jnp.zeros(carry_ref.shape, F32)

    h = _rms_rows(x_ref[...], g_ref[...])
    logits = jnp.where(lane < N_EXPERTS, _mm_f32_f32(h, wr_ref[...]), NEG_INF)
    v1 = jnp.max(logits, axis=-1, keepdims=True)
    i1 = jnp.min(jnp.where(logits == v1, lane, LANES), axis=-1, keepdims=True)
    rest = jnp.where(lane == i1, NEG_INF, logits)
    v2 = jnp.max(rest, axis=-1, keepdims=True)
    i2 = jnp.min(jnp.where(rest == v2, lane, LANES), axis=-1, keepdims=True)
    e2 = jnp.exp(v2 - v1)
    den = 1.0 + e2
    hit = jnp.where((lane == i1) | (lane == i2), 1.0, 0.0)
    before = carry_ref[...] + _mm(ltri_ref[...], hit.astype(BF16))
    r1 = jnp.sum(jnp.where(lane == i1, before, 0.0), axis=-1, keepdims=True).astype(I32)
    r2 = jnp.sum(jnp.where(lane == i2, before, 0.0), axis=-1, keepdims=True).astype(I32)
    carry_ref[...] = carry_ref[...] + jnp.sum(hit, axis=0, keepdims=True)
    cnt_ref[...] = carry_ref[...]
    ri_ref[...] = jnp.where(lane == 0, i1, jnp.where(lane == 1, i2, jnp.where(lane == 2, r1,
                                                                             jnp.where(lane == 3, r2, 0))))
    rf_ref[...] = jnp.where(lane == 0, 1.0 / den, jnp.where(lane == 1, e2 / den, 0.0))


def _router(x2, g, wr, ltri):
    t = x2.shape[0]
    tm = TOKEN_TILE
    row = lambda w: pl.BlockSpec((tm, w), lambda i: (i, 0))
    return pl.pallas_call(
        _router_kernel, grid=(t // tm,),
        in_specs=[row(D_MODEL), _full((1, D_MODEL)), _full((D_MODEL, LANES)), _full((tm, tm))],
        out_specs=(row(LANES), row(LANES), _full((1, LANES))),
        out_shape=(jax.ShapeDtypeStruct((t, LANES), I32), jax.ShapeDtypeStruct((t, LANES), F32),
                   jax.ShapeDtypeStruct((1, LANES), F32)),
        scratch_shapes=[pltpu.VMEM((1, LANES), F32)],
        compiler_params=_params(1), name="moe_router")(x2, g, wr, ltri)


def _row_copy(src, src_row, dst, dst_row, sem):
    return pltpu.make_async_copy(src.at[pl.ds(src_row, 1), :], dst.at[pl.ds(dst_row, 1), :], sem)


def _dispatch_kernel(dest_ref, x_ref, xs_init_hbm, xs_hbm, sem):
    del xs_init_hbm
    n_tok = x_ref.shape[0]

    def issue(r, c):
        for k in range(2):
            _row_copy(x_ref, r, xs_hbm, dest_ref[2 * r + k], sem).start(priority=k)
        return c

    lax.fori_loop(0, n_tok, issue, 0, unroll=MOE_DMA_UNROLL)
    for _ in range(2):
        pltpu.make_async_copy(x_ref, xs_hbm.at[pl.ds(0, n_tok), :], sem).wait()


def _dispatch(dest_flat, x2, n_slots):
    t = x2.shape[0]
    td = MOE_DISPATCH_TILE
    xs0 = jnp.zeros((n_slots, D_MODEL), F32)
    return pl.pallas_call(
        _dispatch_kernel, grid=(t // td,),
        in_specs=[pl.BlockSpec((2 * td,), lambda i: (i,), memory_space=pltpu.SMEM),
                  pl.BlockSpec((td, D_MODEL), lambda i: (i, 0)), pl.BlockSpec(memory_space=pl.ANY)],
        out_specs=pl.BlockSpec(memory_space=pl.ANY),
        out_shape=jax.ShapeDtypeStruct((n_slots, D_MODEL), F32),
        scratch_shapes=[pltpu.SemaphoreType.DMA(())],
        input_output_aliases={2: 0},
        compiler_params=_params(1), name="moe_dispatch")(dest_flat, x2, xs0)


def _experts_kernel(bexp_ref, nused_ref, xs_ref, g_ref, wg_ref, wu_ref, wd_ref, out_ref, hb_ref, acc_ref):
    del bexp_ref
    c = pl.program_id(1)
    last_c = pl.num_programs(1) - 1
    used = pl.program_id(0) < nused_ref[0]

    @pl.when(used & (c == 0))
    def _():
        hb_ref[...] = _rms_rows(xs_ref[...], g_ref[...]).astype(BF16)
        acc_ref[...] = jnp.zeros(acc_ref.shape, F32)

    @pl.when(used)
    def _():
        hb = hb_ref[...]
        a = _silu(_mm(hb, wg_ref[...])) * _mm(hb, wu_ref[...])
        acc_ref[...] += _mm(a.astype(BF16), wd_ref[...])

    @pl.when(used & (c == last_c))
    def _():
        out_ref[...] = acc_ref[...]

    @pl.when(jnp.logical_not(used) & (c == last_c))
    def _():
        out_ref[...] = jnp.zeros(out_ref.shape, F32)


def _experts(block_exp, n_used, xs, g, wg, wu, wd):
    n_slots = xs.shape[0]
    bm = MOE_SLOT_BLOCK
    fc = FF_CHUNK
    grid_spec = pltpu.PrefetchScalarGridSpec(
        num_scalar_prefetch=2, grid=(n_slots // bm, D_FF // fc),
        in_specs=[pl.BlockSpec((bm, D_MODEL), lambda b, c, be, nu: (b, 0)),
                  pl.BlockSpec((1, D_MODEL), lambda b, c, be, nu: (0, 0)),
                  pl.BlockSpec((None, D_MODEL, fc), lambda b, c, be, nu: (be[b], 0, c)),
                  pl.BlockSpec((None, D_MODEL, fc), lambda b, c, be, nu: (be[b], 0, c)),
                  pl.BlockSpec((None, fc, D_MODEL), lambda b, c, be, nu: (be[b], c, 0))],
        out_specs=pl.BlockSpec((bm, D_MODEL), lambda b, c, be, nu: (b, 0)),
        scratch_shapes=[pltpu.VMEM((bm, D_MODEL), BF16), pltpu.VMEM((bm, D_MODEL), F32)])
    return pl.pallas_call(
        _experts_kernel, grid_spec=grid_spec, out_shape=jax.ShapeDtypeStruct((n_slots, D_MODEL), F32),
        compiler_params=_params(2), name="moe_experts")(block_exp, n_used, xs, g, wg, wu, wd)


def _combine_kernel(dest_ref, x_ref, rf_ref, ys_hbm, out_ref, buf_ref, sem):
    n_tok = x_ref.shape[0]

    def issue(r, c):
        for k in range(2):
            _row_copy(ys_hbm, dest_ref[2 * r + k], buf_ref.at[k], r, sem).start(priority=k)
        return c

    lax.fori_loop(0, n_tok, issue, 0, unroll=MOE_DMA_UNROLL)
    for k in range(2):
        pltpu.make_async_copy(ys_hbm.at[pl.ds(0, n_tok), :], buf_ref.at[k], sem).wait()
    gates = rf_ref[...]
    out_ref[...] = x_ref[...] + (buf_ref[0] * gates[:, 0:1] + buf_ref[1] * gates[:, 1:2])


def _combine(dest_flat, x2, rf, ys):
    t = x2.shape[0]
    tc = MOE_COMBINE_TILE
    return pl.pallas_call(
        _combine_kernel, grid=(t // tc,),
        in_specs=[pl.BlockSpec((2 * tc,), lambda i: (i,), memory_space=pltpu.SMEM),
                  pl.BlockSpec((tc, D_MODEL), lambda i: (i, 0)), pl.BlockSpec((tc, LANES), lambda i: (i, 0)),
                  pl.BlockSpec(memory_space=pl.ANY)],
        out_specs=pl.BlockSpec((tc, D_MODEL), lambda i: (i, 0)),
        out_shape=jax.ShapeDtypeStruct((t, D_MODEL), F32),
        scratch_shapes=[pltpu.VMEM((2, tc, D_MODEL), F32), pltpu.SemaphoreType.DMA(())],
        compiler_params=_params(1), name="moe_combine")(dest_flat, x2, rf, ys)


def _moe(x2, g, wr, wg, wu, wd, ltri):
    t = x2.shape[0]
    bm = MOE_SLOT_BLOCK
    n_blocks = (2 * t) // bm + N_EXPERTS
    ri, rf, cnt = _router(x2, g, wr, ltri)
    counts = cnt[0, :N_EXPERTS].astype(I32)
    padded = (counts + bm - 1) // bm * bm
    pend = jnp.cumsum(padded)
    pstart = pend - padded
    dest = (pstart[ri[:, 0:2]] + ri[:, 2:4]).reshape(2 * t)
    n_used = (pend[-1:] // bm).astype(I32)
    starts = jnp.arange(n_blocks, dtype=I32) * bm
    block_exp = jnp.minimum(jnp.sum((pend[None, :] <= starts[:, None]).astype(I32), axis=1), N_EXPERTS - 1)
    xs = _dispatch(dest, x2, n_blocks * bm)
    ys = _experts(block_exp, n_used, xs, g, wg, wu, wd)
    return _combine(dest, x2, rf, ys)


def _const_mats():
    r = np.arange(256)
    gmat = ((r[:, None] // 64) == (r[None, :] // 64)).astype(np.float32) / 64.0
    d = r % 64
    same = (r[:, None] // 64) == (r[None, :] // 64)
    swap = ((d[None, :] < 8) & (d[:, None] == d[None, :] + 8)) | \
           ((d[None, :] >= 8) & (d[None, :] < 16) & (d[:, None] == d[None, :] - 8))
    pmat = (same & swap).astype(np.float32)
    tri = np.tril(np.ones((SSD_CHUNK, SSD_CHUNK), np.float32))
    rsel = np.zeros((LANES, D_SSD), np.float32)
    rep = np.zeros((LANES, SSD_HEADS * SSD_CHUNK), np.float32)
    sel8 = np.zeros((SSD_HEADS, LANES), np.float32)
    for h in range(SSD_HEADS):
        rsel[IDX_DIM + h, h * 64:(h + 1) * 64] = 1.0
        rep[IDX_DIM + h, h * SSD_CHUNK:(h + 1) * SSD_CHUNK] = 1.0
        sel8[h, IDX_DIM + h] = 1.0
    tril = np.tril(np.ones((KEY_TILE, KEY_TILE), np.float32), -1)
    ltri = np.tril(np.ones((TOKEN_TILE, TOKEN_TILE), np.float32), -1)
    to = lambda a: jnp.asarray(a, dtype=BF16)
    return to(gmat), to(pmat), (to(tri), to(rsel), to(rep), to(sel8)), to(tril), to(ltri)


def kernel(x, mem, positions, mix_norm_g, w_in, conv_dw_w, conv_ln_g, conv_ln_b, conv_pw_w, ssd_conv_w, ssd_conv_b, ssd_dt_bias, ssd_a_log, ssd_d, ssd_norm_g, att_q_norm_g, att_k_norm_g, w_out, memx_norm_g, mem_norm_g, memx_w_q, memx_w_kv, memx_q_norm_g, memx_k_norm_g, memx_w_o, ffn_norm_g, ffn_w_gate, ffn_w_up, ffn_w_down, moe_w_router, moe_w_gate, moe_w_up, moe_w_down):
    batch, seq, d = x.shape
    nm = mem.shape[1]
    depth = w_in.shape[0]
    t = batch * seq
    assert d == D_MODEL and seq % TOKEN_TILE == 0 and seq % SEQ_TILE == 0 and seq % KEY_TILE == 0
    assert TOKEN_TILE % KEY_TILE == 0 and KEY_TILE % Q_TILE == 0 and D_FF % FF_CHUNK == 0

    gmat, pmat, ssd_consts, tril, ltri = _const_mats()

    inv = ROPE_THETA ** (-jnp.arange(0, ROT_DIM, 2, dtype=F32) / ROT_DIM)
    ang = positions.astype(F32).reshape(t, 1) * inv
    cos, sin = jnp.cos(ang), jnp.sin(ang)
    c64 = jnp.concatenate([cos, cos, jnp.ones((t, 48), F32)], axis=1)
    s64 = jnp.concatenate([-sin, sin, jnp.zeros((t, 48), F32)], axis=1)
    c128 = jnp.concatenate([c64, c64], axis=1)
    s128 = jnp.concatenate([s64, s64], axis=1)
    ct, st = cos.T, sin.T

    x2 = x.reshape(t, d)
    mem2 = mem.reshape(batch * nm, d)
    row = lambda v: v.reshape(1, -1)
    rep64 = lambda v: jnp.repeat(v, SSD_HEAD_DIM).reshape(1, -1)
    dt_lanes = lambda v: jnp.concatenate(
        [jnp.zeros((IDX_DIM,), F32), v, jnp.zeros((LANES - IDX_DIM - SSD_HEADS,), F32)]).reshape(1, LANES)
    o = _OFF

    for i in range(depth):
        w = w_in[i].astype(BF16)
        wstd = jnp.concatenate([w[:, o[0]:o[3]], w[:, o[5]:o[6]], w[:, o[8]:o[9]], w[:, o[3]:o[4]],
                                jnp.zeros((d, LANES - IDX_DIM - SSD_HEADS), BF16)], axis=1)
        wtrn = jnp.concatenate([w[:, o[7]:o[8]].T, w[:, o[4]:o[5]].T, w[:, o[6]:o[7]].T, w[:, o[9]:o[10]].T,
                                jnp.zeros((N_TRN - 1032, d), BF16)], axis=0)
        uconv, z, xbc, kn, small, kib, qit, qnt, vt, wt = _in_proj(
            x2, row(mix_norm_g[i]), wstd, wtrn, c128, s128, ct, st, att_q_norm_g[i].reshape(64, 1),
            jnp.tile(att_k_norm_g[i], ATT_HEADS).reshape(1, 256), gmat, pmat)

        dw = jnp.concatenate([conv_dw_w[i], jnp.zeros((32 - CONV_WIDTH, D_CONV), F32)], axis=0)
        y_conv = _conformer_conv(uconv, dw, row(conv_ln_g[i]), row(conv_ln_b[i]), conv_pw_w[i].astype(BF16),
                                 batch, seq)
        cw = jnp.concatenate([ssd_conv_w[i], jnp.zeros((8 - SSD_CONV, D_XBC), F32)], axis=0)
        y_ssd = _ssd(z, xbc, small, cw, row(ssd_conv_b[i]), dt_lanes(ssd_dt_bias[i]), dt_lanes(ssd_a_log[i]),
                     rep64(ssd_d[i]), row(ssd_norm_g[i]), ssd_consts, batch, seq)
        y_att = _dsa(qit, qnt, wt, kib, kn, vt, tril, batch, seq)

        kbd, vbd = _mem_prep(mem2, row(mem_norm_g[i]), memx_w_kv[i].astype(BF16),
                             jnp.tile(memx_k_norm_g[i], MEM_HEADS).reshape(1, 256), gmat, batch, nm)
        x2 = _mix_memx(x2, y_conv, y_ssd, y_att, w_out[i].astype(BF16), row(memx_norm_g[i]),
                       memx_w_q[i].astype(BF16), jnp.tile(memx_q_norm_g[i], MEM_HEADS).reshape(1, 256), gmat,
                       kbd, vbd, memx_w_o[i].astype(BF16), seq)

        j = i // 2
        if i % 2 == 0:
            x2 = _ffn(x2, row(ffn_norm_g[i]), ffn_w_gate[j].astype(BF16), ffn_w_up[j].astype(BF16),
                      ffn_w_down[j].astype(BF16))
        else:
            wr = jnp.concatenate([moe_w_router[j], jnp.zeros((d, LANES - N_EXPERTS), F32)], axis=1)
            x2 = _moe(x2, row(ffn_norm_g[i]), wr, moe_w_gate[j].astype(BF16), moe_w_up[j].astype(BF16),
                      moe_w_down[j].astype(BF16), ltri)
    return x2.reshape(batch, seq, d)
```

```python
import functools
import math

import numpy as np
import jax
import jax.numpy as jnp
from jax import lax
from jax.experimental import pallas as pl
from jax.experimental.pallas import tpu as pltpu

F32 = jnp.float32
BF16 = jnp.bfloat16
I32 = jnp.int32

EPS = 1e-6
D_MODEL = 1024
D_CONV = 256
CONV_WIDTH = 31
SSD_HEAD_DIM = 64
D_SSD = 512
SSD_HEADS = 8
SSD_GROUPS = 2
SSD_STATE = 128
SSD_CONV = 4
SSD_CHUNK = 128
D_XBC = D_SSD + 2 * SSD_GROUPS * SSD_STATE
ATT_HEAD_DIM = 64
D_ATT = 256
ATT_HEADS = 4
IDX_HEADS = 8
IDX_DIM = 64
TOPK_MAX = 256
ROPE_THETA = 500000.0
ROT_DIM = ATT_HEAD_DIM // 4
MEM_HEADS = 4
MEM_HEAD_DIM = 64
D_MEMATT = 256
D_FF = 2816
N_EXPERTS = 8

LANES = 128
SUBLANES = 8
VMEM_LIMIT_BYTES = 56 * 1024 * 1024

TOKEN_TILE = 512
SEQ_TILE = 512
Q_TILE = 256
KEY_TILE = 256
FF_CHUNK = 1408
MOE_SLOT_BLOCK = 512
MOE_DISPATCH_TILE = 512
MOE_COMBINE_TILE = 256
MOE_DMA_UNROLL = 8
CONV_ROWS = 64
CONV_HALO = 32
SSD_HALO = 8
INT_MIN = -2 ** 31
LOG2_E = math.log2(math.e)
NEG_INF = float("-inf")

_SZ = [2 * D_CONV, D_SSD, D_XBC, SSD_HEADS, D_ATT, D_ATT, D_ATT, IDX_HEADS * IDX_DIM, IDX_DIM, IDX_HEADS]
_OFF = np.concatenate([[0], np.cumsum(_SZ)]).tolist()
N_STD = 2 * D_CONV + D_SSD + D_XBC + D_ATT + LANES
N_TRN = IDX_HEADS * IDX_DIM + D_ATT + D_ATT + 16


def _mm(a, b):
    return jnp.dot(a, b, preferred_element_type=F32)


def _mm_nt(a, b):
    return lax.dot_general(a, b, (((1,), (1,)), ((), ())), preferred_element_type=F32)


def _split3(a):
    hi = a.astype(BF16)
    r = a - hi.astype(F32)
    mid = r.astype(BF16)
    lo = (r - mid.astype(F32)).astype(BF16)
    return hi, mid, lo


def _mm_f32_lhs(a, b_exact):
    hi, mid, lo = _split3(a)
    return (_mm(hi, b_exact) + _mm(mid, b_exact)) + _mm(lo, b_exact)


def _mm_f32_rhs(a_exact, b):
    hi, mid, lo = _split3(b)
    return (_mm(a_exact, hi) + _mm(a_exact, mid)) + _mm(a_exact, lo)


def _mm_nt_f32_rhs(a_exact, b):
    hi, mid, lo = _split3(b)
    return (_mm_nt(a_exact, hi) + _mm_nt(a_exact, mid)) + _mm_nt(a_exact, lo)


def _mm_f32_f32(a, b):
    a1, a2, a3 = _split3(a)
    b1, b2, b3 = _split3(b)
    return (_mm(a1, b1) + (_mm(a1, b2) + _mm(a2, b1))) + ((_mm(a1, b3) + _mm(a2, b2)) + _mm(a3, b1))


def _silu(x):
    return x * jax.nn.sigmoid(x)


def _rms_rows(x, g):
    ms = jnp.mean(x * x, axis=-1, keepdims=True)
    return (x * lax.rsqrt(ms + EPS)) * g


def _fold_rows(x, op):
    parts = [x[r:r + SUBLANES] for r in range(0, x.shape[0], SUBLANES)]
    while len(parts) > 1:
        nxt = [op(parts[k], parts[k + 1]) for k in range(0, len(parts) - 1, 2)]
        if len(parts) % 2:
            nxt.append(parts[-1])
        parts = nxt
    return parts[0]


def _params(n_axes):
    return pltpu.CompilerParams(dimension_semantics=("arbitrary",) * n_axes,
                                vmem_limit_bytes=VMEM_LIMIT_BYTES)


def _full(shape):
    n = len(shape)
    return pl.BlockSpec(shape, lambda *_: (0,) * n)


def _in_proj_kernel(x_ref, g_ref, wstd_ref, wtrn_ref, ct_ref, st_ref, qg_ref, kg_ref,
                    gmat_ref, pmat_ref,
                    uconv_ref, z_ref, xbc_ref, kn_ref, small_ref, kib_ref, qit_ref, qnt_ref, vt_ref, wt_ref):
    tm = x_ref.shape[0]
    hb = _rms_rows(x_ref[...], g_ref[...]).astype(BF16)
    uconv_ref[...] = _mm(hb, wstd_ref[:, 0:512])
    z_ref[...] = _mm(hb, wstd_ref[:, 512:1024])
    xbc_ref[...] = _mm(hb, wstd_ref[:, 1024:2048])

    ct = ct_ref[...]
    st = st_ref[...]
    rest = ATT_HEAD_DIM - ROT_DIM
    c128 = jnp.concatenate([ct, ct, jnp.ones((rest, tm), F32)] * 2, axis=0).T
    s128 = jnp.concatenate([-st, st, jnp.zeros((rest, tm), F32)] * 2, axis=0).T
    kraw = _mm(hb, wstd_ref[:, 2048:2304])
    kn = (kraw * lax.rsqrt(_mm_f32_lhs(kraw * kraw, gmat_ref[...]) + EPS)) * kg_ref[...]
    kn = kn * jnp.concatenate([c128, c128], axis=1) + \
        _mm_f32_lhs(kn, pmat_ref[...]) * jnp.concatenate([s128, s128], axis=1)
    kn_ref[...] = kn.astype(BF16)

    sm = _mm(hb, wstd_ref[:, 2304:2432])
    lane = lax.broadcasted_iota(I32, (tm, LANES), 1)
    is_idx = lane < IDX_DIM
    sm = sm * jnp.where(is_idx, c128, 1.0) + _mm_f32_lhs(sm, pmat_ref[0:LANES, 0:LANES]) * jnp.where(is_idx, s128, 0.0)
    small_ref[...] = sm
    kib_ref[...] = jnp.where(is_idx, sm, 0.0).astype(BF16)

    out_t = _mm_nt(wtrn_ref[...], hb)

    def rope_t(blk):
        x1 = blk[0:8]
        x2 = blk[8:16]
        return jnp.concatenate([x1 * ct - x2 * st, x2 * ct + x1 * st, blk[16:64]], axis=0)

    qi = [rope_t(out_t[h * 64:(h + 1) * 64]) for h in range(IDX_HEADS)]
    qit_ref[...] = jnp.concatenate(qi, axis=0).astype(BF16)

    qn = []
    for h in range(ATT_HEADS):
        blk = out_t[512 + h * 64:512 + (h + 1) * 64]
        ms = jnp.mean(blk * blk, axis=0, keepdims=True)
        blk = (blk * lax.rsqrt(ms + EPS)) * qg_ref[...]
        qn.append(rope_t(blk) * (ATT_HEAD_DIM ** -0.5 * LOG2_E))
    qnt_ref[...] = jnp.concatenate(qn, axis=0).astype(BF16)

    vt = out_t[768:1024].astype(BF16)
    for c in range(tm // KEY_TILE):
        vt_ref[c] = vt[:, c * KEY_TILE:(c + 1) * KEY_TILE]
    wt_ref[...] = out_t[1024:1032] * (IDX_HEADS ** -0.5 * IDX_DIM ** -0.5)


def _in_proj(x2, g, wstd, wtrn, ct, st, qg, kg, gmat, pmat):
    t = x2.shape[0]
    tm = TOKEN_TILE
    row = lambda w: pl.BlockSpec((tm, w), lambda i: (i, 0))
    col = lambda r: pl.BlockSpec((r, tm), lambda i: (0, i))
    out_shape = (
        jax.ShapeDtypeStruct((t, 512), F32), jax.ShapeDtypeStruct((t, 512), F32),
        jax.ShapeDtypeStruct((t, 1024), F32), jax.ShapeDtypeStruct((t, 256), BF16),
        jax.ShapeDtypeStruct((t, LANES), F32), jax.ShapeDtypeStruct((t, LANES), BF16),
        jax.ShapeDtypeStruct((512, t), BF16), jax.ShapeDtypeStruct((256, t), BF16),
        jax.ShapeDtypeStruct((t // KEY_TILE, 256, KEY_TILE), BF16), jax.ShapeDtypeStruct((8, t), F32))
    out_specs = (row(512), row(512), row(1024), row(256), row(LANES), row(LANES), col(512), col(256),
                 pl.BlockSpec((tm // KEY_TILE, 256, KEY_TILE), lambda i: (i, 0, 0)), col(8))
    in_specs = [row(D_MODEL), _full((1, D_MODEL)), _full(wstd.shape), _full(wtrn.shape),
                col(8), col(8), _full((64, 1)), _full((1, 256)), _full((256, 256)), _full((256, 256))]
    return pl.pallas_call(_in_proj_kernel, grid=(t // tm,), in_specs=in_specs, out_specs=out_specs,
                          out_shape=out_shape, compiler_params=_params(1), name="in_proj")(
        x2, g, wstd, wtrn, ct, st, qg, kg, gmat, pmat)


def _conv_kernel(u_ref, dw_ref, lg_ref, lb_ref, pw_ref, out_ref, hp_ref, hs_ref):
    ts = u_ref.shape[0]

    @pl.when(pl.program_id(1) == 0)
    def _():
        hp_ref[0:CONV_HALO, :] = jnp.zeros((CONV_HALO, D_CONV), F32)

    u = u_ref[...]
    hp_ref[CONV_HALO:CONV_HALO + ts, :] = u[:, :D_CONV] * jax.nn.sigmoid(u[:, D_CONV:])
    n_sh = hs_ref.shape[1]
    for s in range(1, SUBLANES):
        hs_ref[s - 1] = hp_ref[pl.ds(s, n_sh), :]
    base = CONV_HALO - (CONV_WIDTH - 1)

    def tap(r0, k):
        a, s = divmod(base + k, SUBLANES)
        if s == 0:
            return hp_ref[pl.ds(r0 + a * SUBLANES, CONV_ROWS), :]
        return hs_ref[s - 1, pl.ds(r0 + a * SUBLANES, CONV_ROWS), :]

    for r0 in range(0, ts, CONV_ROWS):
        acc = dw_ref[0:1, :] * tap(r0, 0)
        for k in range(1, CONV_WIDTH):
            acc = acc + dw_ref[k:k + 1, :] * tap(r0, k)
        mu = jnp.mean(acc, axis=-1, keepdims=True)
        xc = acc - mu
        y = xc * lax.rsqrt(jnp.mean(xc * xc, axis=-1, keepdims=True) + EPS)
        y = _silu(y * lg_ref[...] + lb_ref[...])
        out_ref[pl.ds(r0, CONV_ROWS), :] = _mm(y.astype(BF16), pw_ref[...])
    hp_ref[0:CONV_HALO, :] = hp_ref[ts:ts + CONV_HALO, :]


def _conformer_conv(uconv, dw, lg, lb, pw, batch, seq):
    ts = SEQ_TILE
    ns = seq // ts
    return pl.pallas_call(
        _conv_kernel, grid=(batch, ns),
        in_specs=[pl.BlockSpec((ts, 512), lambda b, s: (b * ns + s, 0)), _full((32, D_CONV)), _full((1, D_CONV)),
                  _full((1, D_CONV)), _full((D_CONV, D_CONV))],
        out_specs=pl.BlockSpec((ts, D_CONV), lambda b, s: (b * ns + s, 0)),
        out_shape=jax.ShapeDtypeStruct((batch * seq, D_CONV), F32),
        scratch_shapes=[pltpu.VMEM((CONV_HALO + ts, D_CONV), F32),
                        pltpu.VMEM((SUBLANES - 1, CONV_HALO + ts - SUBLANES, D_CONV), F32)],
        compiler_params=_params(2), name="conformer_conv")(uconv, dw, lg, lb, pw)


def _ssd_kernel(z_ref, xbc_ref, small_ref, cw_ref, cb_ref, dtb_ref, alog_ref, dskip_ref, ng_ref,
                tri_ref, rsel_ref, rep_ref, sel8_ref, out_ref, xp_ref, state_ref):
    ts = z_ref.shape[0]
    L = SSD_CHUNK

    @pl.when(pl.program_id(1) == 0)
    def _():
        xp_ref[0:SSD_HALO, :] = jnp.zeros((SSD_HALO, D_XBC), F32)
        state_ref[...] = jnp.zeros(state_ref.shape, F32)

    xp_ref[SSD_HALO:SSD_HALO + ts, :] = xbc_ref[...]
    lane1 = lax.broadcasted_iota(I32, (1, LANES), 1)
    is_dt = (lane1 >= IDX_DIM) & (lane1 < IDX_DIM + SSD_HEADS)
    a_neg = jnp.where(is_dt, -jnp.exp(alog_ref[...]), 0.0)
    ii = lax.broadcasted_iota(I32, (L, L), 0)
    jj = lax.broadcasted_iota(I32, (L, L), 1)
    causal = ii >= jj
    first_half = jj < SSD_HEAD_DIM
    gw = D_SSD // SSD_GROUPS

    for c in range(ts // L):
        r0 = c * L
        base = r0 + SSD_HALO - (SSD_CONV - 1)
        conv = cb_ref[...] + cw_ref[0:1, :] * xp_ref[pl.ds(base, L), :]
        for k in range(1, SSD_CONV):
            conv = conv + cw_ref[k:k + 1, :] * xp_ref[pl.ds(base + k, L), :]
        xa = _silu(conv)
        xs = xa[:, 0:D_SSD]
        bm = xa[:, D_SSD:D_SSD + 256]
        cm = xa[:, D_SSD + 256:D_SSD + 512]

        dt_arg = small_ref[pl.ds(r0, L), :] + dtb_ref[...]
        dt_s = jnp.maximum(dt_arg, 0.0) + jnp.log1p(jnp.exp(-jnp.abs(dt_arg)))
        acs_s = _mm_f32_rhs(tri_ref[...], dt_s * a_neg)
        dt = _mm_f32_lhs(dt_s, rsel_ref[...])
        acs = _mm_f32_lhs(acs_s, rsel_ref[...])
        acs_wide = _mm_f32_lhs(acs_s, rep_ref[...])
        acs_t = _mm_nt_f32_rhs(sel8_ref[...], acs_s)
        last = acs[L - 1:L, :]
        decay_end = jnp.exp(last - acs)
        decay_in = jnp.exp(acs)
        xd = xs * dt
        xdb = xd.astype(BF16)

        y_parts = []
        for g in range(SSD_GROUPS):
            bg = bm[:, g * SSD_STATE:(g + 1) * SSD_STATE]
            cgb = cm[:, g * SSD_STATE:(g + 1) * SSD_STATE].astype(BF16)
            cb = _mm_nt(cgb, bg.astype(BF16))
            pair_out = []
            for pr in range(2):
                h0 = g * 4 + pr * 2
                xpair = xdb[:, h0 * 64:(h0 + 2) * 64]
                outs = []
                for h in (h0, h0 + 1):
                    seg = acs_wide[:, h * L:(h + 1) * L] - acs_t[h:h + 1, :]
                    m_h = (cb * jnp.exp(jnp.where(causal, seg, NEG_INF))).astype(BF16)
                    outs.append(_mm(m_h, xpair))
                pair_out.append(jnp.where(first_half, outs[0], outs[1]))
            y_diag = jnp.concatenate(pair_out, axis=1)
            gs = slice(g * gw, (g + 1) * gw)
            st = state_ref[:, gs]
            y_off = _mm(cgb, st.astype(BF16)) * decay_in[:, gs]
            contrib = _mm(bg.T.astype(BF16), (xd[:, gs] * decay_end[:, gs]).astype(BF16))
            state_ref[:, gs] = jnp.exp(last[:, gs]) * st + contrib
            y_parts.append(y_diag + y_off)
        y = jnp.concatenate(y_parts, axis=1) + xs * dskip_ref[...]
        y = y * _silu(z_ref[pl.ds(r0, L), :])
        outs = []
        for g in range(SSD_GROUPS):
            gs = slice(g * gw, (g + 1) * gw)
            outs.append(_rms_rows(y[:, gs], ng_ref[:, gs]))
        out_ref[pl.ds(r0, L), :] = jnp.concatenate(outs, axis=1)
    xp_ref[0:SSD_HALO, :] = xp_ref[ts:ts + SSD_HALO, :]


def _ssd(z, xbc, small, cw, cb, dtb, alog, dskip, ng, consts, batch, seq):
    ts = SEQ_TILE
    ns = seq // ts
    tri, rsel, rep, sel8 = consts
    row = lambda w: pl.BlockSpec((ts, w), lambda b, s: (b * ns + s, 0))
    return pl.pallas_call(
        _ssd_kernel, grid=(batch, ns),
        in_specs=[row(512), row(1024), row(LANES), _full((8, D_XBC)), _full((1, D_XBC)), _full((1, LANES)),
                  _full((1, LANES)), _full((1, 512)), _full((1, 512)), _full(tri.shape), _full(rsel.shape),
                  _full(rep.shape), _full(sel8.shape)],
        out_specs=row(512),
        out_shape=jax.ShapeDtypeStruct((batch * seq, 512), F32),
        scratch_shapes=[pltpu.VMEM((SSD_HALO + ts, D_XBC), F32), pltpu.VMEM((SSD_STATE, D_SSD), F32)],
        compiler_params=_params(2), name="ssd_mixer")(z, xbc, small, cw, cb, dtb, alog, dskip, ng, tri, rsel, rep, sel8)


def _dsa_kernel(qit_ref, qnt_ref, wt_ref, kib_ref, kn_ref, vt_ref, tril_ref, out_ref, keys_ref, keysb_ref, s_ref,
                acc_ref, *, ksel):
    qb = Q_TILE
    kb = KEY_TILE
    i = pl.program_id(1)
    nsb = (i * qb + qb + kb - 1) // kb
    qpos = i * qb + lax.broadcasted_iota(I32, (kb, qb), 1)
    krow = lax.broadcasted_iota(I32, (kb, qb), 0)

    rhs_idx = jnp.concatenate([qit_ref[h * IDX_DIM:(h + 1) * IDX_DIM, :] for h in range(IDX_HEADS)], axis=1)
    rhs_idx = jnp.concatenate([rhs_idx, jnp.zeros((LANES - IDX_DIM, IDX_HEADS * qb), BF16)], axis=0)
    wt = wt_ref[...]

    def score_tile(j, carry):
        r = pl.multiple_of(j * kb, kb)
        lg = _mm(kib_ref[pl.ds(r, kb), :], rhs_idx)
        sc = wt[0:1, :] * jnp.maximum(lg[:, 0:qb], 0.0)
        for h in range(1, IDX_HEADS):
            sc = sc + wt[h:h + 1, :] * jnp.maximum(lg[:, h * qb:(h + 1) * qb], 0.0)
        sc = jnp.where(j * kb + krow <= qpos, sc, NEG_INF)
        keys_ref[pl.ds(r, kb), :] = sc
        keysb_ref[pl.ds(r, kb), :] = sc.astype(BF16)
        return carry

    lax.fori_loop(0, nsb, score_tile, 0)

    def key_to_float(k):
        return pltpu.bitcast(k ^ (lax.shift_right_arithmetic(k, 31) & 0x7FFFFFFF), F32)

    def count(cand, strict):
        def body(j, acc):
            kt = keys_ref[pl.ds(pl.multiple_of(j * kb, kb), kb), :]
            hit = (kt > cand) if strict else (kt >= cand)
            return acc + _fold_rows(jnp.where(hit, 1.0, 0.0), jnp.add)
        acc = lax.fori_loop(0, nsb, body, jnp.zeros((SUBLANES, qb), F32))
        return jnp.sum(acc, axis=0, keepdims=True)

    pack = 2 * SUBLANES
    one_b = jnp.ones((pack, qb), BF16)
    zero_b = jnp.zeros((pack, qb), BF16)

    def count_coarse(cand_b):
        def body(j, acc):
            kt = keysb_ref[pl.ds(pl.multiple_of(j * kb, kb), kb), :]
            parts = [jnp.where(kt[g * pack:(g + 1) * pack] >= cand_b, one_b, zero_b) for g in range(kb // pack)]
            while len(parts) > 1:
                parts = [parts[k] + parts[k + 1] for k in range(0, len(parts), 2)]
            return acc + parts[0]
        acc = lax.fori_loop(0, nsb, body, zero_b)
        return jnp.sum(acc.astype(F32), axis=0, keepdims=True)

    def coarse_step(it, t16):
        cand16 = t16 | lax.shift_left(jnp.int32(1), 15 - it)
        k = lax.shift_left(cand16, 16) ^ INT_MIN
        cand = pltpu.bitcast(k ^ (lax.shift_right_arithmetic(k, 31) & 0x7FFF0000), F32)
        cnt = count_coarse(jnp.concatenate([cand, cand], axis=0).astype(BF16))
        return jnp.where(cnt >= float(ksel), cand16, t16)

    t16 = lax.fori_loop(0, 16, coarse_step, jnp.zeros((SUBLANES, qb), I32))
    k1 = lax.shift_left(t16, 16) ^ INT_MIN
    c1 = pltpu.bitcast(k1 ^ (lax.shift_right_arithmetic(k1, 31) & 0x7FFF0000), F32)
    bits1 = pltpu.bitcast(c1, I32)
    key1 = bits1 ^ (lax.shift_right_arithmetic(bits1, 31) & 0x7FFFFFFF)
    def fine_step(it, carry):
        lo, hi = carry
        mid = lo + lax.shift_right_arithmetic(hi - lo + 1, 1)
        ok = count(key_to_float(mid)[0:1, :], False) >= float(ksel)
        return jnp.where(ok, mid, lo), jnp.where(ok, hi, mid - 1)

    lo, _ = lax.fori_loop(0, 17, fine_step, (key1 - 0x8000, key1 + 0x10000))
    tau = key_to_float(lo)[0:1, :]
    takes_all = qpos[0:1, :] < ksel
    tau = jnp.where(takes_all, NEG_INF, tau)
    need = jnp.where(takes_all, 0.0, float(ksel) - count(tau, True))

    zblk = jnp.zeros((ATT_HEAD_DIM, qb), BF16)
    rhs_att = jnp.concatenate(
        [jnp.concatenate([qnt_ref[h * 64:(h + 1) * 64, :] if hh == h else zblk for hh in range(ATT_HEADS)], axis=1)
         for h in range(ATT_HEADS)], axis=0)

    def pass1(j, carry):
        m, eq_before = carry
        r = pl.multiple_of(j * kb, kb)
        s = _mm(kn_ref[pl.ds(r, kb), :], rhs_att)
        kt = keys_ref[pl.ds(r, kb), :]
        eq = kt == tau
        eqf = jnp.where(eq, 1.0, 0.0)
        rank = eq_before + _mm(tril_ref[...], eqf.astype(BF16))
        sel = (kt > tau) | (eq & (rank < need))
        s = jnp.concatenate([jnp.where(sel, s[:, h * qb:(h + 1) * qb], NEG_INF) for h in range(ATT_HEADS)], axis=1)
        s_ref[pl.ds(r, kb), :] = s
        return (jnp.maximum(m, _fold_rows(s, jnp.maximum)),
                eq_before + jnp.sum(_fold_rows(eqf, jnp.add), axis=0, keepdims=True))

    m8, _ = lax.fori_loop(0, nsb, pass1,
                          (jnp.full((SUBLANES, ATT_HEADS * qb), NEG_INF, F32), jnp.zeros((1, qb), F32)))
    m = jnp.max(m8, axis=0, keepdims=True)

    acc_ref[...] = jnp.zeros(acc_ref.shape, F32)

    def pass2(j, l8):
        p = jnp.exp2(s_ref[pl.ds(pl.multiple_of(j * kb, kb), kb), :] - m)
        pb = p.astype(BF16)
        vt = vt_ref[j]
        for h in range(ATT_HEADS):
            hs = slice(h * ATT_HEAD_DIM, (h + 1) * ATT_HEAD_DIM)
            acc_ref[hs, :] += _mm(vt[hs, :], pb[:, h * qb:(h + 1) * qb])
        return l8 + _fold_rows(p, jnp.add)

    l = jnp.sum(lax.fori_loop(0, nsb, pass2, jnp.zeros((SUBLANES, ATT_HEADS * qb), F32)), axis=0, keepdims=True)
    o_t = jnp.concatenate(
        [acc_ref[h * 64:(h + 1) * 64, :] / l[:, h * qb:(h + 1) * qb] for h in range(ATT_HEADS)],
        axis=0)
    out_ref[...] = o_t.T


def _dsa(qit, qnt, wt, kib, kn, vt, tril, batch, seq):
    qb = Q_TILE
    nq = seq // qb
    nkb = seq // KEY_TILE
    ksel = min(TOPK_MAX, seq // 4)
    colq = lambda r: pl.BlockSpec((r, qb), lambda b, i: (0, b * nq + i))
    return pl.pallas_call(
        functools.partial(_dsa_kernel, ksel=ksel), grid=(batch, nq),
        in_specs=[colq(512), colq(256), colq(8),
                  pl.BlockSpec((seq, LANES), lambda b, i: (b, 0)), pl.BlockSpec((seq, 256), lambda b, i: (b, 0)),
                  pl.BlockSpec((nkb, 256, KEY_TILE), lambda b, i: (b, 0, 0)), _full((KEY_TILE, KEY_TILE))],
        out_specs=pl.BlockSpec((qb, D_ATT), lambda b, i: (b * nq + i, 0)),
        out_shape=jax.ShapeDtypeStruct((batch * seq, D_ATT), F32),
        scratch_shapes=[pltpu.VMEM((seq, qb), F32), pltpu.VMEM((seq, qb), BF16),
                        pltpu.VMEM((seq, ATT_HEADS * qb), F32),
                        pltpu.VMEM((D_ATT, qb), F32)],
        compiler_params=_params(2), name="dsa_attention")(qit, qnt, wt, kib, kn, vt, tril)


def _mem_prep_kernel(mem_ref, g_ref, wkv_ref, kg_ref, gmat_ref, kbd_ref, vbd_ref):
    nm = mem_ref.shape[0]
    mb = _rms_rows(mem_ref[...], g_ref[...]).astype(BF16)
    kv = _mm(mb, wkv_ref[...])
    k = kv[:, 0:D_MEMATT]
    kn = (k * lax.rsqrt(_mm_f32_lhs(k * k, gmat_ref[...]) + EPS)) * kg_ref[...]
    k_t = kn.T.astype(BF16)
    zero = jnp.zeros((MEM_HEAD_DIM, nm), BF16)
    kbd_ref[...] = jnp.concatenate(
        [jnp.concatenate([k_t[h * 64:(h + 1) * 64, :] if hh == h else zero for hh in range(MEM_HEADS)], axis=1)
         for h in range(MEM_HEADS)], axis=0)
    v = kv[:, D_MEMATT:]
    lane = lax.broadcasted_iota(I32, (nm, D_MEMATT), 1)
    vbd_ref[...] = jnp.concatenate(
        [jnp.where((lane >= h * 64) & (lane < (h + 1) * 64), v, 0.0) for h in range(MEM_HEADS)],
        axis=0).astype(BF16)


def _mem_prep(mem2, g, wkv, kg, gmat, batch, nm):
    return pl.pallas_call(
        _mem_prep_kernel, grid=(batch,),
        in_specs=[pl.BlockSpec((nm, D_MODEL), lambda b: (b, 0)), _full((1, D_MODEL)), _full((D_MODEL, 512)),
                  _full((1, 256)), _full((256, 256))],
        out_specs=(pl.BlockSpec((None, D_MEMATT, MEM_HEADS * nm), lambda b: (b, 0, 0)),
                   pl.BlockSpec((None, MEM_HEADS * nm, D_MEMATT), lambda b: (b, 0, 0))),
        out_shape=(jax.ShapeDtypeStruct((batch, D_MEMATT, MEM_HEADS * nm), BF16),
                   jax.ShapeDtypeStruct((batch, MEM_HEADS * nm, D_MEMATT), BF16)),
        compiler_params=_params(1), name="mem_prep")(mem2, g, wkv, kg, gmat)


def _mix_memx_kernel(x_ref, yc_ref, ys_ref, ya_ref, wout_ref, g_ref, wq_ref, qg_ref, gmat_ref, kbd_ref, vbd_ref,
                     wo_ref, out_ref):
    nm = kbd_ref.shape[1] // MEM_HEADS
    y_mix = jnp.concatenate([yc_ref[...], ys_ref[...], ya_ref[...]], axis=1).astype(BF16)
    x1 = x_ref[...] + _mm(y_mix, wout_ref[...])
    hb = _rms_rows(x1, g_ref[...]).astype(BF16)
    q = _mm(hb, wq_ref[...])
    qn = (q * lax.rsqrt(_mm_f32_lhs(q * q, gmat_ref[...]) + EPS)) * qg_ref[...]
    s = _mm(qn.astype(BF16), kbd_ref[...]) * (MEM_HEAD_DIM ** -0.5)
    ps = []
    for h in range(MEM_HEADS):
        sh = s[:, h * nm:(h + 1) * nm]
        e = jnp.exp(sh - jnp.max(sh, axis=-1, keepdims=True))
        ps.append(e / jnp.sum(e, axis=-1, keepdims=True))
    o = _mm(jnp.concatenate(ps, axis=1).astype(BF16), vbd_ref[...])
    out_ref[...] = x1 + _mm(o.astype(BF16), wo_ref[...])


def _mix_memx(x2, yc, ys, ya, wout, g, wq, qg, gmat, kbd, vbd, wo, seq):
    t = x2.shape[0]
    tm = TOKEN_TILE
    per_b = seq // tm
    nm4 = kbd.shape[2]
    row = lambda w: pl.BlockSpec((tm, w), lambda i: (i, 0))
    return pl.pallas_call(
        _mix_memx_kernel, grid=(t // tm,),
        in_specs=[row(D_MODEL), row(256), row(512), row(256), _full((D_MODEL, D_MODEL)), _full((1, D_MODEL)),
                  _full((D_MODEL, 256)), _full((1, 256)), _full((256, 256)),
                  pl.BlockSpec((None, D_MEMATT, nm4), lambda i: (i // per_b, 0, 0)),
                  pl.BlockSpec((None, nm4, D_MEMATT), lambda i: (i // per_b, 0, 0)), _full((256, D_MODEL))],
        out_specs=row(D_MODEL), out_shape=jax.ShapeDtypeStruct((t, D_MODEL), F32),
        compiler_params=_params(1), name="mix_memx")(x2, yc, ys, ya, wout, g, wq, qg, gmat, kbd, vbd, wo)


def _ffn_kernel(x_ref, g_ref, wg_ref, wu_ref, wd_ref, out_ref, hb_ref, acc_ref):
    c = pl.program_id(1)

    @pl.when(c == 0)
    def _():
        hb_ref[...] = _rms_rows(x_ref[...], g_ref[...]).astype(BF16)
        acc_ref[...] = jnp.zeros(acc_ref.shape, F32)

    hb = hb_ref[...]
    a = _silu(_mm(hb, wg_ref[...])) * _mm(hb, wu_ref[...])
    acc_ref[...] += _mm(a.astype(BF16), wd_ref[...])

    @pl.when(c == pl.num_programs(1) - 1)
    def _():
        out_ref[...] = x_ref[...] + acc_ref[...]


def _ffn(x2, g, wg, wu, wd):
    t = x2.shape[0]
    tm = TOKEN_TILE
    fc = FF_CHUNK
    return pl.pallas_call(
        _ffn_kernel, grid=(t // tm, D_FF // fc),
        in_specs=[pl.BlockSpec((tm, D_MODEL), lambda i, c: (i, 0)), _full((1, D_MODEL)),
                  pl.BlockSpec((D_MODEL, fc), lambda i, c: (0, c)), pl.BlockSpec((D_MODEL, fc), lambda i, c: (0, c)),
                  pl.BlockSpec((fc, D_MODEL), lambda i, c: (c, 0))],
        out_specs=pl.BlockSpec((tm, D_MODEL), lambda i, c: (i, 0)),
        out_shape=jax.ShapeDtypeStruct((t, D_MODEL), F32),
        scratch_shapes=[pltpu.VMEM((tm, D_MODEL), BF16), pltpu.VMEM((tm, D_MODEL), F32)],
        compiler_params=_params(2), name="ffn_dense")(x2, g, wg, wu, wd)


def _router_kernel(x_ref, g_ref, wr_ref, ltri_ref, ri_ref, rf_ref, cnt_ref, carry_ref):
    tm = x_ref.shape[0]
    lane = lax.broadcasted_iota(I32, (tm, LANES), 1)

    @pl.when(pl.program_id(0) == 0)
    def _():
        carry_ref[...] = jnp.zeros(carry_ref.shape, F32)

    h = _rms_rows(x_ref[...], g_ref[...])
    logits = jnp.where(lane < N_EXPERTS, _mm_f32_f32(h, wr_ref[...]), NEG_INF)
    v1 = jnp.max(logits, axis=-1, keepdims=True)
    i1 = jnp.min(jnp.where(logits == v1, lane, LANES), axis=-1, keepdims=True)
    rest = jnp.where(lane == i1, NEG_INF, logits)
    v2 = jnp.max(rest, axis=-1, keepdims=True)
    i2 = jnp.min(jnp.where(rest == v2, lane, LANES), axis=-1, keepdims=True)
    e2 = jnp.exp(v2 - v1)
    den = 1.0 + e2
    hit = jnp.where((lane == i1) | (lane == i2), 1.0, 0.0)
    before = carry_ref[...] + _mm(ltri_ref[...], hit.astype(BF16))
    r1 = jnp.sum(jnp.where(lane == i1, before, 0.0), axis=-1, keepdims=True).astype(I32)
    r2 = jnp.sum(jnp.where(lane == i2, before, 0.0), axis=-1, keepdims=True).astype(I32)
    carry_ref[...] = carry_ref[...] + jnp.sum(hit, axis=0, keepdims=True)
    cnt_ref[...] = carry_ref[...]
    ri_ref[...] = jnp.where(lane == 0, i1, jnp.where(lane == 1, i2, jnp.where(lane == 2, r1,
                                                                             jnp.where(lane == 3, r2, 0))))
    rf_ref[...] = jnp.where(lane == 0, 1.0 / den, jnp.where(lane == 1, e2 / den, 0.0))


def _router(x2, g, wr, ltri):
    t = x2.shape[0]
    tm = TOKEN_TILE
    row = lambda w: pl.BlockSpec((tm, w), lambda i: (i, 0))
    return pl.pallas_call(
        _router_kernel, grid=(t // tm,),
        in_specs=[row(D_MODEL), _full((1, D_MODEL)), _full((D_MODEL, LANES)), _full((tm, tm))],
        out_specs=(row(LANES), row(LANES), _full((1, LANES))),
        out_shape=(jax.ShapeDtypeStruct((t, LANES), I32), jax.ShapeDtypeStruct((t, LANES), F32),
                   jax.ShapeDtypeStruct((1, LANES), F32)),
        scratch_shapes=[pltpu.VMEM((1, LANES), F32)],
        compiler_params=_params(1), name="moe_router")(x2, g, wr, ltri)


def _row_copy(src, src_row, dst, dst_row, sem):
    return pltpu.make_async_copy(src.at[pl.ds(src_row, 1), :], dst.at[pl.ds(dst_row, 1), :], sem)


def _dispatch_kernel(pend_ref, padded_ref, dest_ref, x_ref, xs_hbm, zero_ref, sem, zero_sem):
    n_tok = x_ref.shape[0]
    bm = zero_ref.shape[0]

    @pl.when(pl.program_id(0) == 0)
    def _():
        zero_ref[...] = jnp.zeros(zero_ref.shape, F32)
        n_slots = xs_hbm.shape[0]
        zero_blocks = [(padded_ref[e] > 0, pend_ref[e] - bm) for e in range(N_EXPERTS)]
        zero_blocks += [(pend_ref[N_EXPERTS - 1] + e * bm < n_slots, pend_ref[N_EXPERTS - 1] + e * bm)
                        for e in range(N_EXPERTS)]
        for cond, start in zero_blocks:
            @pl.when(cond)
            def _():
                dst = xs_hbm.at[pl.ds(pl.multiple_of(start, bm), bm), :]
                pltpu.make_async_copy(zero_ref, dst, zero_sem).start()
        for cond, _ in zero_blocks:
            @pl.when(cond)
            def _():
                pltpu.make_async_copy(zero_ref, xs_hbm.at[pl.ds(0, bm), :], zero_sem).wait()

    def issue(r, c):
        for k in range(2):
            _row_copy(x_ref, r, xs_hbm, dest_ref[2 * r + k], sem).start(priority=k)
        return c

    lax.fori_loop(0, n_tok, issue, 0, unroll=MOE_DMA_UNROLL)
    for _ in range(2):
        pltpu.make_async_copy(x_ref, xs_hbm.at[pl.ds(0, n_tok), :], sem).wait()


def _dispatch(pend, padded, dest_flat, x2, n_slots):
    t = x2.shape[0]
    td = MOE_DISPATCH_TILE
    grid_spec = pltpu.PrefetchScalarGridSpec(
        num_scalar_prefetch=2, grid=(t // td,),
        in_specs=[pl.BlockSpec((2 * td,), lambda i, pe, pa: (i,), memory_space=pltpu.SMEM),
                  pl.BlockSpec((td, D_MODEL), lambda i, pe, pa: (i, 0))],
        out_specs=pl.BlockSpec(memory_space=pl.ANY),
        scratch_shapes=[pltpu.VMEM((MOE_SLOT_BLOCK, D_MODEL), F32), pltpu.SemaphoreType.DMA(()),
                        pltpu.SemaphoreType.DMA(())])
    return pl.pallas_call(
        _dispatch_kernel, grid_spec=grid_spec, out_shape=jax.ShapeDtypeStruct((n_slots, D_MODEL), F32),
        compiler_params=_params(1), name="moe_dispatch")(pend, padded, dest_flat, x2)


def _experts_kernel(bexp_ref, nused_ref, xs_ref, g_ref, wg_ref, wu_ref, wd_ref, out_ref, hb_ref, acc_ref):
    del bexp_ref
    c = pl.program_id(1)
    last_c = pl.num_programs(1) - 1
    used = pl.program_id(0) < nused_ref[0]

    @pl.when(used & (c == 0))
    def _():
        hb_ref[...] = _rms_rows(xs_ref[...], g_ref[...]).astype(BF16)
        acc_ref[...] = jnp.zeros(acc_ref.shape, F32)

    @pl.when(used)
    def _():
        hb = hb_ref[...]
        a = _silu(_mm(hb, wg_ref[...])) * _mm(hb, wu_ref[...])
        acc_ref[...] += _mm(a.astype(BF16), wd_ref[...])

    @pl.when(used & (c == last_c))
    def _():
        out_ref[...] = acc_ref[...]

    @pl.when(jnp.logical_not(used) & (c == last_c))
    def _():
        out_ref[...] = jnp.zeros(out_ref.shape, F32)


def _experts(block_exp, n_used, xs, g, wg, wu, wd):
    n_slots = xs.shape[0]
    bm = MOE_SLOT_BLOCK
    fc = FF_CHUNK
    grid_spec = pltpu.PrefetchScalarGridSpec(
        num_scalar_prefetch=2, grid=(n_slots // bm, D_FF // fc),
        in_specs=[pl.BlockSpec((bm, D_MODEL), lambda b, c, be, nu: (b, 0)),
                  pl.BlockSpec((1, D_MODEL), lambda b, c, be, nu: (0, 0)),
                  pl.BlockSpec((None, D_MODEL, fc), lambda b, c, be, nu: (be[b], 0, c)),
                  pl.BlockSpec((None, D_MODEL, fc), lambda b, c, be, nu: (be[b], 0, c)),
                  pl.BlockSpec((None, fc, D_MODEL), lambda b, c, be, nu: (be[b], c, 0))],
        out_specs=pl.BlockSpec((bm, D_MODEL), lambda b, c, be, nu: (b, 0)),
        scratch_shapes=[pltpu.VMEM((bm, D_MODEL), BF16), pltpu.VMEM((bm, D_MODEL), F32)])
    return pl.pallas_call(
        _experts_kernel, grid_spec=grid_spec, out_shape=jax.ShapeDtypeStruct((n_slots, D_MODEL), F32),
        compiler_params=_params(2), name="moe_experts")(block_exp, n_used, xs, g, wg, wu, wd)


def _combine_kernel(dest_ref, x_ref, rf_ref, ys_hbm, out_ref, buf_ref, sem):
    n_tok = x_ref.shape[0]

    def issue(r, c):
        for k in range(2):
            _row_copy(ys_hbm, dest_ref[2 * r + k], buf_ref.at[k], r, sem).start(priority=k)
        return c

    lax.fori_loop(0, n_tok, issue, 0, unroll=MOE_DMA_UNROLL)
    for k in range(2):
        pltpu.make_async_copy(ys_hbm.at[pl.ds(0, n_tok), :], buf_ref.at[k], sem).wait()
    gates = rf_ref[...]
    out_ref[...] = x_ref[...] + (buf_ref[0] * gates[:, 0:1] + buf_ref[1] * gates[:, 1:2])


def _combine(dest_flat, x2, rf, ys):
    t = x2.shape[0]
    tc = MOE_COMBINE_TILE
    return pl.pallas_call(
        _combine_kernel, grid=(t // tc,),
        in_specs=[pl.BlockSpec((2 * tc,), lambda i: (i,), memory_space=pltpu.SMEM),
                  pl.BlockSpec((tc, D_MODEL), lambda i: (i, 0)), pl.BlockSpec((tc, LANES), lambda i: (i, 0)),
                  pl.BlockSpec(memory_space=pl.ANY)],
        out_specs=pl.BlockSpec((tc, D_MODEL), lambda i: (i, 0)),
        out_shape=jax.ShapeDtypeStruct((t, D_MODEL), F32),
        scratch_shapes=[pltpu.VMEM((2, tc, D_MODEL), F32), pltpu.SemaphoreType.DMA(())],
        compiler_params=_params(1), name="moe_combine")(dest_flat, x2, rf, ys)


def _moe(x2, g, wr, wg, wu, wd, ltri):
    t = x2.shape[0]
    bm = MOE_SLOT_BLOCK
    n_blocks = (2 * t) // bm + N_EXPERTS
    ri, rf, cnt = _router(x2, g, wr, ltri)
    counts = cnt[0, :N_EXPERTS].astype(I32)
    padded = (counts + bm - 1) // bm * bm
    pend = jnp.cumsum(padded)
    pstart = pend - padded
    dest = (pstart[ri[:, 0:2]] + ri[:, 2:4]).reshape(2 * t)
    n_used = (pend[-1:] // bm).astype(I32)
    starts = jnp.arange(n_blocks, dtype=I32) * bm
    block_exp = jnp.minimum(jnp.sum((pend[None, :] <= starts[:, None]).astype(I32), axis=1), N_EXPERTS - 1)
    xs = _dispatch(pend.astype(I32), padded.astype(I32), dest, x2, n_blocks * bm)
    ys = _experts(block_exp, n_used, xs, g, wg, wu, wd)
    return _combine(dest, x2, rf, ys)


def _const_mats():
    r = np.arange(256)
    gmat = ((r[:, None] // 64) == (r[None, :] // 64)).astype(np.float32) / 64.0
    d = r % 64
    same = (r[:, None] // 64) == (r[None, :] // 64)
    swap = ((d[None, :] < 8) & (d[:, None] == d[None, :] + 8)) | \
           ((d[None, :] >= 8) & (d[None, :] < 16) & (d[:, None] == d[None, :] - 8))
    pmat = (same & swap).astype(np.float32)
    tri = np.tril(np.ones((SSD_CHUNK, SSD_CHUNK), np.float32))
    rsel = np.zeros((LANES, D_SSD), np.float32)
    rep = np.zeros((LANES, SSD_HEADS * SSD_CHUNK), np.float32)
    sel8 = np.zeros((SSD_HEADS, LANES), np.float32)
    for h in range(SSD_HEADS):
        rsel[IDX_DIM + h, h * 64:(h + 1) * 64] = 1.0
        rep[IDX_DIM + h, h * SSD_CHUNK:(h + 1) * SSD_CHUNK] = 1.0
        sel8[h, IDX_DIM + h] = 1.0
    tril = np.tril(np.ones((KEY_TILE, KEY_TILE), np.float32), -1)
    ltri = np.tril(np.ones((TOKEN_TILE, TOKEN_TILE), np.float32), -1)
    to = lambda a: jnp.asarray(a, dtype=BF16)
    return to(gmat), to(pmat), (to(tri), to(rsel), to(rep), to(sel8)), to(tril), to(ltri)


def kernel(x, mem, positions, mix_norm_g, w_in, conv_dw_w, conv_ln_g, conv_ln_b, conv_pw_w, ssd_conv_w, ssd_conv_b, ssd_dt_bias, ssd_a_log, ssd_d, ssd_norm_g, att_q_norm_g, att_k_norm_g, w_out, memx_norm_g, mem_norm_g, memx_w_q, memx_w_kv, memx_q_norm_g, memx_k_norm_g, memx_w_o, ffn_norm_g, ffn_w_gate, ffn_w_up, ffn_w_down, moe_w_router, moe_w_gate, moe_w_up, moe_w_down):
    batch, seq, d = x.shape
    nm = mem.shape[1]
    depth = w_in.shape[0]
    t = batch * seq
    assert d == D_MODEL and seq % TOKEN_TILE == 0 and seq % SEQ_TILE == 0 and seq % KEY_TILE == 0
    assert TOKEN_TILE % KEY_TILE == 0 and KEY_TILE % Q_TILE == 0 and D_FF % FF_CHUNK == 0

    gmat, pmat, ssd_consts, tril, ltri = _const_mats()

    inv = ROPE_THETA ** (-jnp.arange(0, ROT_DIM, 2, dtype=F32) / ROT_DIM)
    ang = positions.astype(F32).reshape(1, t) * inv.reshape(-1, 1)
    ct, st = jnp.cos(ang), jnp.sin(ang)

    x2 = x.reshape(t, d)
    mem2 = mem.reshape(batch * nm, d)
    row = lambda v: v.reshape(1, -1)
    rep64 = lambda v: jnp.repeat(v, SSD_HEAD_DIM).reshape(1, -1)
    dt_lanes = lambda v: jnp.concatenate(
        [jnp.zeros((IDX_DIM,), F32), v, jnp.zeros((LANES - IDX_DIM - SSD_HEADS,), F32)]).reshape(1, LANES)
    o = _OFF

    for i in range(depth):
        w = w_in[i].astype(BF16)
        wstd = jnp.concatenate([w[:, o[0]:o[3]], w[:, o[5]:o[6]], w[:, o[8]:o[9]], w[:, o[3]:o[4]],
                                jnp.zeros((d, LANES - IDX_DIM - SSD_HEADS), BF16)], axis=1)
        wtrn = jnp.concatenate([w[:, o[7]:o[8]].T, w[:, o[4]:o[5]].T, w[:, o[6]:o[7]].T, w[:, o[9]:o[10]].T,
                                jnp.zeros((N_TRN - 1032, d), BF16)], axis=0)
        uconv, z, xbc, kn, small, kib, qit, qnt, vt, wt = _in_proj(
            x2, row(mix_norm_g[i]), wstd, wtrn, ct, st, att_q_norm_g[i].reshape(64, 1),
            jnp.tile(att_k_norm_g[i], ATT_HEADS).reshape(1, 256), gmat, pmat)

        dw = jnp.concatenate([conv_dw_w[i], jnp.zeros((32 - CONV_WIDTH, D_CONV), F32)], axis=0)
        y_conv = _conformer_conv(uconv, dw, row(conv_ln_g[i]), row(conv_ln_b[i]), conv_pw_w[i].astype(BF16),
                                 batch, seq)
        cw = jnp.concatenate([ssd_conv_w[i], jnp.zeros((8 - SSD_CONV, D_XBC), F32)], axis=0)
        y_ssd = _ssd(z, xbc, small, cw, row(ssd_conv_b[i]), dt_lanes(ssd_dt_bias[i]), dt_lanes(ssd_a_log[i]),
                     rep64(ssd_d[i]), row(ssd_norm_g[i]), ssd_consts, batch, seq)
        y_att = _dsa(qit, qnt, wt, kib, kn, vt, tril, batch, seq)

        kbd, vbd = _mem_prep(mem2, row(mem_norm_g[i]), memx_w_kv[i].astype(BF16),
                             jnp.tile(memx_k_norm_g[i], MEM_HEADS).reshape(1, 256), gmat, batch, nm)
        x2 = _mix_memx(x2, y_conv, y_ssd, y_att, w_out[i].astype(BF16), row(memx_norm_g[i]),
                       memx_w_q[i].astype(BF16), jnp.tile(memx_q_norm_g[i], MEM_HEADS).reshape(1, 256), gmat,
                       kbd, vbd, memx_w_o[i].astype(BF16), seq)

        j = i // 2
        if i % 2 == 0:
            x2 = _ffn(x2, row(ffn_norm_g[i]), ffn_w_gate[j].astype(BF16), ffn_w_up[j].astype(BF16),
                      ffn_w_down[j].astype(BF16))
        else:
            wr = jnp.concatenate([moe_w_router[j], jnp.zeros((d, LANES - N_EXPERTS), F32)], axis=1)
            x2 = _moe(x2, row(ffn_norm_g[i]), wr, moe_w_gate[j].astype(BF16), moe_w_up[j].astype(BF16),
                      moe_w_down[j].astype(BF16), ltri)
    return x2.reshape(batch, seq, d)
```

```python
import functools
import math

import numpy as np
import jax
import jax.numpy as jnp
from jax import lax
from jax.experimental import pallas as pl
from jax.experimental.pallas import tpu as pltpu

F32 = jnp.float32
BF16 = jnp.bfloat16
I32 = jnp.int32

EPS = 1e-6
D_MODEL = 1024
D_CONV = 256
CONV_WIDTH = 31
SSD_HEAD_DIM = 64
D_SSD = 512
SSD_HEADS = 8
SSD_GROUPS = 2
SSD_STATE = 128
SSD_CONV = 4
SSD_CHUNK = 128
D_XBC = D_SSD + 2 * SSD_GROUPS * SSD_STATE
ATT_HEAD_DIM = 64
D_ATT = 256
ATT_HEADS = 4
IDX_HEADS = 8
IDX_DIM = 64
TOPK_MAX = 256
ROPE_THETA = 500000.0
ROT_DIM = ATT_HEAD_DIM // 4
MEM_HEADS = 4
MEM_HEAD_DIM = 64
D_MEMATT = 256
D_FF = 2816
N_EXPERTS = 8

LANES = 128
SUBLANES = 8
VMEM_LIMIT_BYTES = 56 * 1024 * 1024

TOKEN_TILE = 512
MIX_TILE = 1024
SEQ_TILE = 512
Q_TILE = 256
KEY_TILE = 256
FF_CHUNK = 1408
MOE_SLOT_BLOCK = 512
MOE_DISPATCH_TILE = 512
MOE_COMBINE_TILE = 256
MOE_DMA_UNROLL = 8
CONV_ROWS = 64
CONV_HALO = 32
SSD_HALO = 8
INT_MIN = -2 ** 31
LOG2_E = math.log2(math.e)
NEG_INF = float("-inf")

_SZ = [2 * D_CONV, D_SSD, D_XBC, SSD_HEADS, D_ATT, D_ATT, D_ATT, IDX_HEADS * IDX_DIM, IDX_DIM, IDX_HEADS]
_OFF = np.concatenate([[0], np.cumsum(_SZ)]).tolist()
N_STD = 2 * D_CONV + D_SSD + D_XBC + D_ATT + LANES
N_TRN = IDX_HEADS * IDX_DIM + D_ATT + D_ATT + 16


def _mm(a, b):
    return jnp.dot(a, b, preferred_element_type=F32)


def _mm_nt(a, b):
    return lax.dot_general(a, b, (((1,), (1,)), ((), ())), preferred_element_type=F32)


def _split3(a):
    hi = a.astype(BF16)
    r = a - hi.astype(F32)
    mid = r.astype(BF16)
    lo = (r - mid.astype(F32)).astype(BF16)
    return hi, mid, lo


def _mm_f32_lhs(a, b_exact):
    hi, mid, lo = _split3(a)
    return (_mm(hi, b_exact) + _mm(mid, b_exact)) + _mm(lo, b_exact)


def _mm_f32_rhs(a_exact, b):
    hi, mid, lo = _split3(b)
    return (_mm(a_exact, hi) + _mm(a_exact, mid)) + _mm(a_exact, lo)


def _mm_nt_f32_rhs(a_exact, b):
    hi, mid, lo = _split3(b)
    return (_mm_nt(a_exact, hi) + _mm_nt(a_exact, mid)) + _mm_nt(a_exact, lo)


def _mm_f32_f32(a, b):
    a1, a2, a3 = _split3(a)
    b1, b2, b3 = _split3(b)
    return (_mm(a1, b1) + (_mm(a1, b2) + _mm(a2, b1))) + ((_mm(a1, b3) + _mm(a2, b2)) + _mm(a3, b1))


def _silu(x):
    return x * jax.nn.sigmoid(x)


def _rms_rows(x, g):
    ms = jnp.mean(x * x, axis=-1, keepdims=True)
    return (x * lax.rsqrt(ms + EPS)) * g


def _fold_rows(x, op):
    parts = [x[r:r + SUBLANES] for r in range(0, x.shape[0], SUBLANES)]
    while len(parts) > 1:
        nxt = [op(parts[k], parts[k + 1]) for k in range(0, len(parts) - 1, 2)]
        if len(parts) % 2:
            nxt.append(parts[-1])
        parts = nxt
    return parts[0]


def _params(n_axes):
    return pltpu.CompilerParams(dimension_semantics=("arbitrary",) * n_axes,
                                vmem_limit_bytes=VMEM_LIMIT_BYTES)


def _full(shape):
    n = len(shape)
    return pl.BlockSpec(shape, lambda *_: (0,) * n)


def _in_proj_kernel(x_ref, g_ref, wstd_ref, wtrn_ref, ct_ref, st_ref, qg_ref, kg_ref,
                    gmat_ref, pmat_ref,
                    uconv_ref, z_ref, xbc_ref, kn_ref, small_ref, kib_ref, qit_ref, qnt_ref, vt_ref, wt_ref):
    tm = x_ref.shape[0]
    hb = _rms_rows(x_ref[...], g_ref[...]).astype(BF16)
    uconv_ref[...] = _mm(hb, wstd_ref[:, 0:512])
    z_ref[...] = _mm(hb, wstd_ref[:, 512:1024])
    xbc_ref[...] = _mm(hb, wstd_ref[:, 1024:2048])

    ct = ct_ref[...]
    st = st_ref[...]
    rest = ATT_HEAD_DIM - ROT_DIM
    c128 = jnp.concatenate([ct, ct, jnp.ones((rest, tm), F32)] * 2, axis=0).T
    s128 = jnp.concatenate([-st, st, jnp.zeros((rest, tm), F32)] * 2, axis=0).T
    kraw = _mm(hb, wstd_ref[:, 2048:2304])
    kn = (kraw * lax.rsqrt(_mm_f32_lhs(kraw * kraw, gmat_ref[...]) + EPS)) * kg_ref[...]
    kn = kn * jnp.concatenate([c128, c128], axis=1) + \
        _mm_f32_lhs(kn, pmat_ref[...]) * jnp.concatenate([s128, s128], axis=1)
    kn_ref[...] = kn.astype(BF16)

    sm = _mm(hb, wstd_ref[:, 2304:2432])
    lane = lax.broadcasted_iota(I32, (tm, LANES), 1)
    is_idx = lane < IDX_DIM
    sm = sm * jnp.where(is_idx, c128, 1.0) + _mm_f32_lhs(sm, pmat_ref[0:LANES, 0:LANES]) * jnp.where(is_idx, s128, 0.0)
    small_ref[...] = sm
    kib_ref[...] = jnp.where(is_idx, sm, 0.0).astype(BF16)

    out_t = _mm_nt(wtrn_ref[...], hb)

    def rope_t(blk):
        x1 = blk[0:8]
        x2 = blk[8:16]
        return jnp.concatenate([x1 * ct - x2 * st, x2 * ct + x1 * st, blk[16:64]], axis=0)

    qi = [rope_t(out_t[h * 64:(h + 1) * 64]) for h in range(IDX_HEADS)]
    qit_ref[...] = jnp.concatenate(qi, axis=0).astype(BF16)

    qn = []
    for h in range(ATT_HEADS):
        blk = out_t[512 + h * 64:512 + (h + 1) * 64]
        ms = jnp.mean(blk * blk, axis=0, keepdims=True)
        blk = (blk * lax.rsqrt(ms + EPS)) * qg_ref[...]
        qn.append(rope_t(blk) * (ATT_HEAD_DIM ** -0.5 * LOG2_E))
    qnt_ref[...] = jnp.concatenate(qn, axis=0).astype(BF16)

    vt = out_t[768:1024].astype(BF16)
    for c in range(tm // KEY_TILE):
        vt_ref[c] = vt[:, c * KEY_TILE:(c + 1) * KEY_TILE]
    wt_ref[...] = out_t[1024:1032] * (IDX_HEADS ** -0.5 * IDX_DIM ** -0.5)


def _in_proj(x2, g, wstd, wtrn, ct, st, qg, kg, gmat, pmat):
    t = x2.shape[0]
    tm = TOKEN_TILE
    row = lambda w: pl.BlockSpec((tm, w), lambda i: (i, 0))
    col = lambda r: pl.BlockSpec((r, tm), lambda i: (0, i))
    out_shape = (
        jax.ShapeDtypeStruct((t, 512), F32), jax.ShapeDtypeStruct((t, 512), F32),
        jax.ShapeDtypeStruct((t, 1024), F32), jax.ShapeDtypeStruct((t, 256), BF16),
        jax.ShapeDtypeStruct((t, LANES), F32), jax.ShapeDtypeStruct((t, LANES), BF16),
        jax.ShapeDtypeStruct((512, t), BF16), jax.ShapeDtypeStruct((256, t), BF16),
        jax.ShapeDtypeStruct((t // KEY_TILE, 256, KEY_TILE), BF16), jax.ShapeDtypeStruct((8, t), F32))
    out_specs = (row(512), row(512), row(1024), row(256), row(LANES), row(LANES), col(512), col(256),
                 pl.BlockSpec((tm // KEY_TILE, 256, KEY_TILE), lambda i: (i, 0, 0)), col(8))
    in_specs = [row(D_MODEL), _full((1, D_MODEL)), _full(wstd.shape), _full(wtrn.shape),
                col(8), col(8), _full((64, 1)), _full((1, 256)), _full((256, 256)), _full((256, 256))]
    return pl.pallas_call(_in_proj_kernel, grid=(t // tm,), in_specs=in_specs, out_specs=out_specs,
                          out_shape=out_shape, compiler_params=_params(1), name="in_proj")(
        x2, g, wstd, wtrn, ct, st, qg, kg, gmat, pmat)


def _conv_kernel(u_ref, dw_ref, lg_ref, lb_ref, pw_ref, out_ref, hp_ref, hs_ref):
    ts = u_ref.shape[0]

    @pl.when(pl.program_id(1) == 0)
    def _():
        hp_ref[0:CONV_HALO, :] = jnp.zeros((CONV_HALO, D_CONV), F32)

    u = u_ref[...]
    hp_ref[CONV_HALO:CONV_HALO + ts, :] = u[:, :D_CONV] * jax.nn.sigmoid(u[:, D_CONV:])
    n_sh = hs_ref.shape[1]
    for s in range(1, SUBLANES):
        hs_ref[s - 1] = hp_ref[pl.ds(s, n_sh), :]
    base = CONV_HALO - (CONV_WIDTH - 1)

    def tap(r0, k):
        a, s = divmod(base + k, SUBLANES)
        if s == 0:
            return hp_ref[pl.ds(r0 + a * SUBLANES, CONV_ROWS), :]
        return hs_ref[s - 1, pl.ds(r0 + a * SUBLANES, CONV_ROWS), :]

    for r0 in range(0, ts, CONV_ROWS):
        acc = dw_ref[0:1, :] * tap(r0, 0)
        for k in range(1, CONV_WIDTH):
            acc = acc + dw_ref[k:k + 1, :] * tap(r0, k)
        mu = jnp.mean(acc, axis=-1, keepdims=True)
        xc = acc - mu
        y = xc * lax.rsqrt(jnp.mean(xc * xc, axis=-1, keepdims=True) + EPS)
        y = _silu(y * lg_ref[...] + lb_ref[...])
        out_ref[pl.ds(r0, CONV_ROWS), :] = _mm(y.astype(BF16), pw_ref[...])
    hp_ref[0:CONV_HALO, :] = hp_ref[ts:ts + CONV_HALO, :]


def _conformer_conv(uconv, dw, lg, lb, pw, batch, seq):
    ts = SEQ_TILE
    ns = seq // ts
    return pl.pallas_call(
        _conv_kernel, grid=(batch, ns),
        in_specs=[pl.BlockSpec((ts, 512), lambda b, s: (b * ns + s, 0)), _full((32, D_CONV)), _full((1, D_CONV)),
                  _full((1, D_CONV)), _full((D_CONV, D_CONV))],
        out_specs=pl.BlockSpec((ts, D_CONV), lambda b, s: (b * ns + s, 0)),
        out_shape=jax.ShapeDtypeStruct((batch * seq, D_CONV), F32),
        scratch_shapes=[pltpu.VMEM((CONV_HALO + ts, D_CONV), F32),
                        pltpu.VMEM((SUBLANES - 1, CONV_HALO + ts - SUBLANES, D_CONV), F32)],
        compiler_params=_params(2), name="conformer_conv")(uconv, dw, lg, lb, pw)


def _ssd_kernel(z_ref, xbc_ref, small_ref, cw_ref, cb_ref, dtb_ref, alog_ref, dskip_ref, ng_ref,
                tri_ref, rsel_ref, rep_ref, sel8_ref, out_ref, xp_ref, state_ref):
    ts = z_ref.shape[0]
    L = SSD_CHUNK

    @pl.when(pl.program_id(1) == 0)
    def _():
        xp_ref[0:SSD_HALO, :] = jnp.zeros((SSD_HALO, D_XBC), F32)
        state_ref[...] = jnp.zeros(state_ref.shape, F32)

    xp_ref[SSD_HALO:SSD_HALO + ts, :] = xbc_ref[...]
    lane1 = lax.broadcasted_iota(I32, (1, LANES), 1)
    is_dt = (lane1 >= IDX_DIM) & (lane1 < IDX_DIM + SSD_HEADS)
    a_neg = jnp.where(is_dt, -jnp.exp(alog_ref[...]), 0.0)
    ii = lax.broadcasted_iota(I32, (L, L), 0)
    jj = lax.broadcasted_iota(I32, (L, L), 1)
    causal = ii >= jj
    first_half = jj < SSD_HEAD_DIM
    gw = D_SSD // SSD_GROUPS

    for c in range(ts // L):
        r0 = c * L
        base = r0 + SSD_HALO - (SSD_CONV - 1)
        conv = cb_ref[...] + cw_ref[0:1, :] * xp_ref[pl.ds(base, L), :]
        for k in range(1, SSD_CONV):
            conv = conv + cw_ref[k:k + 1, :] * xp_ref[pl.ds(base + k, L), :]
        xa = _silu(conv)
        xs = xa[:, 0:D_SSD]
        bm = xa[:, D_SSD:D_SSD + 256]
        cm = xa[:, D_SSD + 256:D_SSD + 512]

        dt_arg = small_ref[pl.ds(r0, L), :] + dtb_ref[...]
        dt_s = jnp.maximum(dt_arg, 0.0) + jnp.log1p(jnp.exp(-jnp.abs(dt_arg)))
        acs_s = _mm_f32_rhs(tri_ref[...], dt_s * a_neg)
        dt = _mm_f32_lhs(dt_s, rsel_ref[...])
        acs = _mm_f32_lhs(acs_s, rsel_ref[...])
        acs_wide = _mm_f32_lhs(acs_s, rep_ref[...])
        acs_t = _mm_nt_f32_rhs(sel8_ref[...], acs_s)
        last = acs[L - 1:L, :]
        decay_end = jnp.exp(last - acs)
        decay_in = jnp.exp(acs)
        xd = xs * dt
        xdb = xd.astype(BF16)

        y_parts = []
        for g in range(SSD_GROUPS):
            bg = bm[:, g * SSD_STATE:(g + 1) * SSD_STATE]
            cgb = cm[:, g * SSD_STATE:(g + 1) * SSD_STATE].astype(BF16)
            cb = _mm_nt(cgb, bg.astype(BF16))
            pair_out = []
            for pr in range(2):
                h0 = g * 4 + pr * 2
                xpair = xdb[:, h0 * 64:(h0 + 2) * 64]
                outs = []
                for h in (h0, h0 + 1):
                    seg = acs_wide[:, h * L:(h + 1) * L] - acs_t[h:h + 1, :]
                    m_h = (cb * jnp.exp(jnp.where(causal, seg, NEG_INF))).astype(BF16)
                    outs.append(_mm(m_h, xpair))
                pair_out.append(jnp.where(first_half, outs[0], outs[1]))
            y_diag = jnp.concatenate(pair_out, axis=1)
            gs = slice(g * gw, (g + 1) * gw)
            st = state_ref[:, gs]
            y_off = _mm(cgb, st.astype(BF16)) * decay_in[:, gs]
            contrib = _mm(bg.T.astype(BF16), (xd[:, gs] * decay_end[:, gs]).astype(BF16))
            state_ref[:, gs] = jnp.exp(last[:, gs]) * st + contrib
            y_parts.append(y_diag + y_off)
        y = jnp.concatenate(y_parts, axis=1) + xs * dskip_ref[...]
        y = y * _silu(z_ref[pl.ds(r0, L), :])
        outs = []
        for g in range(SSD_GROUPS):
            gs = slice(g * gw, (g + 1) * gw)
            outs.append(_rms_rows(y[:, gs], ng_ref[:, gs]))
        out_ref[pl.ds(r0, L), :] = jnp.concatenate(outs, axis=1)
    xp_ref[0:SSD_HALO, :] = xp_ref[ts:ts + SSD_HALO, :]


def _ssd(z, xbc, small, cw, cb, dtb, alog, dskip, ng, consts, batch, seq):
    ts = SEQ_TILE
    ns = seq // ts
    tri, rsel, rep, sel8 = consts
    row = lambda w: pl.BlockSpec((ts, w), lambda b, s: (b * ns + s, 0))
    return pl.pallas_call(
        _ssd_kernel, grid=(batch, ns),
        in_specs=[row(512), row(1024), row(LANES), _full((8, D_XBC)), _full((1, D_XBC)), _full((1, LANES)),
                  _full((1, LANES)), _full((1, 512)), _full((1, 512)), _full(tri.shape), _full(rsel.shape),
                  _full(rep.shape), _full(sel8.shape)],
        out_specs=row(512),
        out_shape=jax.ShapeDtypeStruct((batch * seq, 512), F32),
        scratch_shapes=[pltpu.VMEM((SSD_HALO + ts, D_XBC), F32), pltpu.VMEM((SSD_STATE, D_SSD), F32)],
        compiler_params=_params(2), name="ssd_mixer")(z, xbc, small, cw, cb, dtb, alog, dskip, ng, tri, rsel, rep, sel8)


def _dsa_kernel(qit_ref, qnt_ref, wt_ref, kib_ref, kn_ref, vt_ref, tril_ref, out_ref, keys_ref, keysb_ref, s_ref,
                acc_ref, *, ksel):
    qb = Q_TILE
    kb = KEY_TILE
    i = pl.program_id(1)
    nsb = (i * qb + qb + kb - 1) // kb
    qpos = i * qb + lax.broadcasted_iota(I32, (kb, qb), 1)
    krow = lax.broadcasted_iota(I32, (kb, qb), 0)

    rhs_idx = jnp.concatenate([qit_ref[h * IDX_DIM:(h + 1) * IDX_DIM, :] for h in range(IDX_HEADS)], axis=1)
    rhs_idx = jnp.concatenate([rhs_idx, jnp.zeros((LANES - IDX_DIM, IDX_HEADS * qb), BF16)], axis=0)
    wt = wt_ref[...]

    def score_tile(j, carry):
        r = pl.multiple_of(j * kb, kb)
        lg = _mm(kib_ref[pl.ds(r, kb), :], rhs_idx)
        sc = wt[0:1, :] * jnp.maximum(lg[:, 0:qb], 0.0)
        for h in range(1, IDX_HEADS):
            sc = sc + wt[h:h + 1, :] * jnp.maximum(lg[:, h * qb:(h + 1) * qb], 0.0)
        sc = jnp.where(j * kb + krow <= qpos, sc, NEG_INF)
        keys_ref[pl.ds(r, kb), :] = sc
        keysb_ref[pl.ds(r, kb), :] = sc.astype(BF16)
        return carry

    lax.fori_loop(0, nsb, score_tile, 0)

    def key_to_float(k):
        return pltpu.bitcast(k ^ (lax.shift_right_arithmetic(k, 31) & 0x7FFFFFFF), F32)

    def count(cand, strict):
        def body(j, acc):
            kt = keys_ref[pl.ds(pl.multiple_of(j * kb, kb), kb), :]
            hit = (kt > cand) if strict else (kt >= cand)
            return acc + _fold_rows(jnp.where(hit, 1.0, 0.0), jnp.add)
        acc = lax.fori_loop(0, nsb, body, jnp.zeros((SUBLANES, qb), F32))
        return jnp.sum(acc, axis=0, keepdims=True)

    pack = 2 * SUBLANES
    one_b = jnp.ones((pack, qb), BF16)
    zero_b = jnp.zeros((pack, qb), BF16)

    def count_coarse(cand_b):
        def body(j, acc):
            kt = keysb_ref[pl.ds(pl.multiple_of(j * kb, kb), kb), :]
            parts = [jnp.where(kt[g * pack:(g + 1) * pack] >= cand_b, one_b, zero_b) for g in range(kb // pack)]
            while len(parts) > 1:
                parts = [parts[k] + parts[k + 1] for k in range(0, len(parts), 2)]
            return acc + parts[0]
        acc = lax.fori_loop(0, nsb, body, zero_b)
        return jnp.sum(acc.astype(F32), axis=0, keepdims=True)

    def coarse_step(it, t16):
        cand16 = t16 | lax.shift_left(jnp.int32(1), 15 - it)
        k = lax.shift_left(cand16, 16) ^ INT_MIN
        cand = pltpu.bitcast(k ^ (lax.shift_right_arithmetic(k, 31) & 0x7FFF0000), F32)
        cnt = count_coarse(jnp.concatenate([cand, cand], axis=0).astype(BF16))
        return jnp.where(cnt >= float(ksel), cand16, t16)

    t16 = lax.fori_loop(0, 16, coarse_step, jnp.zeros((SUBLANES, qb), I32))
    k1 = lax.shift_left(t16, 16) ^ INT_MIN
    c1 = pltpu.bitcast(k1 ^ (lax.shift_right_arithmetic(k1, 31) & 0x7FFF0000), F32)
    bits1 = pltpu.bitcast(c1, I32)
    key1 = bits1 ^ (lax.shift_right_arithmetic(bits1, 31) & 0x7FFFFFFF)
    def fine_step(it, carry):
        lo, hi = carry
        mid = lo + lax.shift_right_arithmetic(hi - lo + 1, 1)
        ok = count(key_to_float(mid)[0:1, :], False) >= float(ksel)
        return jnp.where(ok, mid, lo), jnp.where(ok, hi, mid - 1)

    lo, _ = lax.fori_loop(0, 17, fine_step, (key1 - 0x8000, key1 + 0x10000))
    tau = key_to_float(lo)[0:1, :]
    takes_all = qpos[0:1, :] < ksel
    tau = jnp.where(takes_all, NEG_INF, tau)
    need = jnp.where(takes_all, 0.0, float(ksel) - count(tau, True))

    zblk = jnp.zeros((ATT_HEAD_DIM, qb), BF16)
    rhs_att = jnp.concatenate(
        [jnp.concatenate([qnt_ref[h * 64:(h + 1) * 64, :] if hh == h else zblk for hh in range(ATT_HEADS)], axis=1)
         for h in range(ATT_HEADS)], axis=0)

    def pass1(j, carry):
        m, eq_before = carry
        r = pl.multiple_of(j * kb, kb)
        s = _mm(kn_ref[pl.ds(r, kb), :], rhs_att)
        kt = keys_ref[pl.ds(r, kb), :]
        eq = kt == tau
        eqf = jnp.where(eq, 1.0, 0.0)
        rank = eq_before + _mm(tril_ref[...], eqf.astype(BF16))
        sel = (kt > tau) | (eq & (rank < need))
        s = jnp.concatenate([jnp.where(sel, s[:, h * qb:(h + 1) * qb], NEG_INF) for h in range(ATT_HEADS)], axis=1)
        s_ref[pl.ds(r, kb), :] = s
        return (jnp.maximum(m, _fold_rows(s, jnp.maximum)),
                eq_before + jnp.sum(_fold_rows(eqf, jnp.add), axis=0, keepdims=True))

    m8, _ = lax.fori_loop(0, nsb, pass1,
                          (jnp.full((SUBLANES, ATT_HEADS * qb), NEG_INF, F32), jnp.zeros((1, qb), F32)))
    m = jnp.max(m8, axis=0, keepdims=True)

    acc_ref[...] = jnp.zeros(acc_ref.shape, F32)

    def pass2(j, l8):
        p = jnp.exp2(s_ref[pl.ds(pl.multiple_of(j * kb, kb), kb), :] - m)
        pb = p.astype(BF16)
        vt = vt_ref[j]
        for h in range(ATT_HEADS):
            hs = slice(h * ATT_HEAD_DIM, (h + 1) * ATT_HEAD_DIM)
            acc_ref[hs, :] += _mm(vt[hs, :], pb[:, h * qb:(h + 1) * qb])
        return l8 + _fold_rows(p, jnp.add)

    l = jnp.sum(lax.fori_loop(0, nsb, pass2, jnp.zeros((SUBLANES, ATT_HEADS * qb), F32)), axis=0, keepdims=True)
    o_t = jnp.concatenate(
        [acc_ref[h * 64:(h + 1) * 64, :] / l[:, h * qb:(h + 1) * qb] for h in range(ATT_HEADS)],
        axis=0)
    out_ref[...] = o_t.T


def _dsa(qit, qnt, wt, kib, kn, vt, tril, batch, seq):
    qb = Q_TILE
    nq = seq // qb
    nkb = seq // KEY_TILE
    ksel = min(TOPK_MAX, seq // 4)
    colq = lambda r: pl.BlockSpec((r, qb), lambda b, i: (0, b * nq + i))
    return pl.pallas_call(
        functools.partial(_dsa_kernel, ksel=ksel), grid=(batch, nq),
        in_specs=[colq(512), colq(256), colq(8),
                  pl.BlockSpec((seq, LANES), lambda b, i: (b, 0)), pl.BlockSpec((seq, 256), lambda b, i: (b, 0)),
                  pl.BlockSpec((nkb, 256, KEY_TILE), lambda b, i: (b, 0, 0)), _full((KEY_TILE, KEY_TILE))],
        out_specs=pl.BlockSpec((qb, D_ATT), lambda b, i: (b * nq + i, 0)),
        out_shape=jax.ShapeDtypeStruct((batch * seq, D_ATT), F32),
        scratch_shapes=[pltpu.VMEM((seq, qb), F32), pltpu.VMEM((seq, qb), BF16),
                        pltpu.VMEM((seq, ATT_HEADS * qb), F32),
                        pltpu.VMEM((D_ATT, qb), F32)],
        compiler_params=_params(2), name="dsa_attention")(qit, qnt, wt, kib, kn, vt, tril)


def _mem_prep_kernel(mem_ref, g_ref, wkv_ref, kg_ref, gmat_ref, kbd_ref, vbd_ref):
    nm = mem_ref.shape[0]
    mb = _rms_rows(mem_ref[...], g_ref[...]).astype(BF16)
    kv = _mm(mb, wkv_ref[...])
    k = kv[:, 0:D_MEMATT]
    kn = (k * lax.rsqrt(_mm_f32_lhs(k * k, gmat_ref[...]) + EPS)) * kg_ref[...]
    k_t = kn.T.astype(BF16)
    zero = jnp.zeros((MEM_HEAD_DIM, nm), BF16)
    kbd_ref[...] = jnp.concatenate(
        [jnp.concatenate([k_t[h * 64:(h + 1) * 64, :] if hh == h else zero for hh in range(MEM_HEADS)], axis=1)
         for h in range(MEM_HEADS)], axis=0)
    v = kv[:, D_MEMATT:]
    lane = lax.broadcasted_iota(I32, (nm, D_MEMATT), 1)
    vbd_ref[...] = jnp.concatenate(
        [jnp.where((lane >= h * 64) & (lane < (h + 1) * 64), v, 0.0) for h in range(MEM_HEADS)],
        axis=0).astype(BF16)


def _mem_prep(mem2, g, wkv, kg, gmat, batch, nm):
    return pl.pallas_call(
        _mem_prep_kernel, grid=(batch,),
        in_specs=[pl.BlockSpec((nm, D_MODEL), lambda b: (b, 0)), _full((1, D_MODEL)), _full((D_MODEL, 512)),
                  _full((1, 256)), _full((256, 256))],
        out_specs=(pl.BlockSpec((None, D_MEMATT, MEM_HEADS * nm), lambda b: (b, 0, 0)),
                   pl.BlockSpec((None, MEM_HEADS * nm, D_MEMATT), lambda b: (b, 0, 0))),
        out_shape=(jax.ShapeDtypeStruct((batch, D_MEMATT, MEM_HEADS * nm), BF16),
                   jax.ShapeDtypeStruct((batch, MEM_HEADS * nm, D_MEMATT), BF16)),
        compiler_params=_params(1), name="mem_prep")(mem2, g, wkv, kg, gmat)


def _mix_memx_kernel(x_ref, yc_ref, ys_ref, ya_ref, wout_ref, g_ref, wq_ref, qg_ref, gmat_ref, kbd_ref, vbd_ref,
                     wo_ref, out_ref):
    nm = kbd_ref.shape[1] // MEM_HEADS
    y_mix = jnp.concatenate([yc_ref[...], ys_ref[...], ya_ref[...]], axis=1).astype(BF16)
    x1 = x_ref[...] + _mm(y_mix, wout_ref[...])
    hb = _rms_rows(x1, g_ref[...]).astype(BF16)
    q = _mm(hb, wq_ref[...])
    qn = (q * lax.rsqrt(_mm_f32_lhs(q * q, gmat_ref[...]) + EPS)) * qg_ref[...]
    s = _mm(qn.astype(BF16), kbd_ref[...]) * (MEM_HEAD_DIM ** -0.5)
    ps = []
    for h in range(MEM_HEADS):
        sh = s[:, h * nm:(h + 1) * nm]
        e = jnp.exp(sh - jnp.max(sh, axis=-1, keepdims=True))
        ps.append(e / jnp.sum(e, axis=-1, keepdims=True))
    o = _mm(jnp.concatenate(ps, axis=1).astype(BF16), vbd_ref[...])
    out_ref[...] = x1 + _mm(o.astype(BF16), wo_ref[...])


def _mix_memx(x2, yc, ys, ya, wout, g, wq, qg, gmat, kbd, vbd, wo, seq):
    t = x2.shape[0]
    tm = min(MIX_TILE, seq)
    per_b = seq // tm
    nm4 = kbd.shape[2]
    row = lambda w: pl.BlockSpec((tm, w), lambda i: (i, 0))
    return pl.pallas_call(
        _mix_memx_kernel, grid=(t // tm,),
        in_specs=[row(D_MODEL), row(256), row(512), row(256), _full((D_MODEL, D_MODEL)), _full((1, D_MODEL)),
                  _full((D_MODEL, 256)), _full((1, 256)), _full((256, 256)),
                  pl.BlockSpec((None, D_MEMATT, nm4), lambda i: (i // per_b, 0, 0)),
                  pl.BlockSpec((None, nm4, D_MEMATT), lambda i: (i // per_b, 0, 0)), _full((256, D_MODEL))],
        out_specs=row(D_MODEL), out_shape=jax.ShapeDtypeStruct((t, D_MODEL), F32),
        compiler_params=_params(1), name="mix_memx")(x2, yc, ys, ya, wout, g, wq, qg, gmat, kbd, vbd, wo)


def _ffn_kernel(x_ref, g_ref, wg_ref, wu_ref, wd_ref, out_ref, acc_ref):
    c = pl.program_id(1)

    @pl.when(c == 0)
    def _():
        acc_ref[...] = jnp.zeros(acc_ref.shape, F32)

    hb = _rms_rows(x_ref[...], g_ref[...]).astype(BF16)
    a = _silu(_mm(hb, wg_ref[...])) * _mm(hb, wu_ref[...])
    acc_ref[...] += _mm(a.astype(BF16), wd_ref[...])

    @pl.when(c == pl.num_programs(1) - 1)
    def _():
        out_ref[...] = x_ref[...] + acc_ref[...]


def _ffn(x2, g, wg, wu, wd):
    t = x2.shape[0]
    tm = TOKEN_TILE
    fc = FF_CHUNK
    return pl.pallas_call(
        _ffn_kernel, grid=(t // tm, D_FF // fc),
        in_specs=[pl.BlockSpec((tm, D_MODEL), lambda i, c: (i, 0)), _full((1, D_MODEL)),
                  pl.BlockSpec((D_MODEL, fc), lambda i, c: (0, c)), pl.BlockSpec((D_MODEL, fc), lambda i, c: (0, c)),
                  pl.BlockSpec((fc, D_MODEL), lambda i, c: (c, 0))],
        out_specs=pl.BlockSpec((tm, D_MODEL), lambda i, c: (i, 0)),
        out_shape=jax.ShapeDtypeStruct((t, D_MODEL), F32),
        scratch_shapes=[pltpu.VMEM((tm, D_MODEL), F32)],
        compiler_params=_params(2), name="ffn_dense")(x2, g, wg, wu, wd)


def _router_kernel(x_ref, g_ref, wr_ref, ltri_ref, ri_ref, rf_ref, cnt_ref, carry_ref):
    tm = x_ref.shape[0]
    lane = lax.broadcasted_iota(I32, (tm, LANES), 1)

    @pl.when(pl.program_id(0) == 0)
    def _():
        carry_ref[...] = jnp.zeros(carry_ref.shape, F32)

    h = _rms_rows(x_ref[...], g_ref[...])
    logits = jnp.where(lane < N_EXPERTS, _mm_f32_f32(h, wr_ref[...]), NEG_INF)
    v1 = jnp.max(logits, axis=-1, keepdims=True)
    i1 = jnp.min(jnp.where(logits == v1, lane, LANES), axis=-1, keepdims=True)
    rest = jnp.where(lane == i1, NEG_INF, logits)
    v2 = jnp.max(rest, axis=-1, keepdims=True)
    i2 = jnp.min(jnp.where(rest == v2, lane, LANES), axis=-1, keepdims=True)
    e2 = jnp.exp(v2 - v1)
    den = 1.0 + e2
    hit = jnp.where((lane == i1) | (lane == i2), 1.0, 0.0)
    before = carry_ref[...] + _mm(ltri_ref[...], hit.astype(BF16))
    r1 = jnp.sum(jnp.where(lane == i1, before, 0.0), axis=-1, keepdims=True).astype(I32)
    r2 = jnp.sum(jnp.where(lane == i2, before, 0.0), axis=-1, keepdims=True).astype(I32)
    carry_ref[...] = carry_ref[...] + jnp.sum(hit, axis=0, keepdims=True)
    cnt_ref[...] = carry_ref[...]
    ri_ref[...] = jnp.where(lane == 0, i1, jnp.where(lane == 1, i2, jnp.where(lane == 2, r1,
                                                                             jnp.where(lane == 3, r2, 0))))
    rf_ref[...] = jnp.where(lane == 0, 1.0 / den, jnp.where(lane == 1, e2 / den, 0.0))


def _router(x2, g, wr, ltri):
    t = x2.shape[0]
    tm = TOKEN_TILE
    row = lambda w: pl.BlockSpec((tm, w), lambda i: (i, 0))
    return pl.pallas_call(
        _router_kernel, grid=(t // tm,),
        in_specs=[row(D_MODEL), _full((1, D_MODEL)), _full((D_MODEL, LANES)), _full((tm, tm))],
        out_specs=(row(LANES), row(LANES), _full((1, LANES))),
        out_shape=(jax.ShapeDtypeStruct((t, LANES), I32), jax.ShapeDtypeStruct((t, LANES), F32),
                   jax.ShapeDtypeStruct((1, LANES), F32)),
        scratch_shapes=[pltpu.VMEM((1, LANES), F32)],
        compiler_params=_params(1), name="moe_router")(x2, g, wr, ltri)


def _row_copy(src, src_row, dst, dst_row, sem):
    return pltpu.make_async_copy(src.at[pl.ds(src_row, 1), :], dst.at[pl.ds(dst_row, 1), :], sem)


def _dispatch_kernel(pend_ref, padded_ref, dest_ref, x_ref, xs_hbm, zero_ref, sem, zero_sem):
    n_tok = x_ref.shape[0]
    bm = zero_ref.shape[0]

    @pl.when(pl.program_id(0) == 0)
    def _():
        zero_ref[...] = jnp.zeros(zero_ref.shape, F32)
        n_slots = xs_hbm.shape[0]
        zero_blocks = [(padded_ref[e] > 0, pend_ref[e] - bm) for e in range(N_EXPERTS)]
        zero_blocks += [(pend_ref[N_EXPERTS - 1] + e * bm < n_slots, pend_ref[N_EXPERTS - 1] + e * bm)
                        for e in range(N_EXPERTS)]
        for cond, start in zero_blocks:
            @pl.when(cond)
            def _():
                dst = xs_hbm.at[pl.ds(pl.multiple_of(start, bm), bm), :]
                pltpu.make_async_copy(zero_ref, dst, zero_sem).start()
        for cond, _ in zero_blocks:
            @pl.when(cond)
            def _():
                pltpu.make_async_copy(zero_ref, xs_hbm.at[pl.ds(0, bm), :], zero_sem).wait()

    def issue(r, c):
        for k in range(2):
            _row_copy(x_ref, r, xs_hbm, dest_ref[2 * r + k], sem).start(priority=k)
        return c

    lax.fori_loop(0, n_tok, issue, 0, unroll=MOE_DMA_UNROLL)
    for _ in range(2):
        pltpu.make_async_copy(x_ref, xs_hbm.at[pl.ds(0, n_tok), :], sem).wait()


def _dispatch(pend, padded, dest_flat, x2, n_slots):
    t = x2.shape[0]
    td = MOE_DISPATCH_TILE
    grid_spec = pltpu.PrefetchScalarGridSpec(
        num_scalar_prefetch=2, grid=(t // td,),
        in_specs=[pl.BlockSpec((2 * td,), lambda i, pe, pa: (i,), memory_space=pltpu.SMEM),
                  pl.BlockSpec((td, D_MODEL), lambda i, pe, pa: (i, 0))],
        out_specs=pl.BlockSpec(memory_space=pl.ANY),
        scratch_shapes=[pltpu.VMEM((MOE_SLOT_BLOCK, D_MODEL), F32), pltpu.SemaphoreType.DMA(()),
                        pltpu.SemaphoreType.DMA(())])
    return pl.pallas_call(
        _dispatch_kernel, grid_spec=grid_spec, out_shape=jax.ShapeDtypeStruct((n_slots, D_MODEL), F32),
        compiler_params=_params(1), name="moe_dispatch")(pend, padded, dest_flat, x2)


def _experts_kernel(bexp_ref, nused_ref, xs_ref, g_ref, wg_ref, wu_ref, wd_ref, out_ref, acc_ref):
    del bexp_ref
    c = pl.program_id(1)
    last_c = pl.num_programs(1) - 1
    used = pl.program_id(0) < nused_ref[0]

    @pl.when(used & (c == 0))
    def _():
        acc_ref[...] = jnp.zeros(acc_ref.shape, F32)

    @pl.when(used)
    def _():
        hb = _rms_rows(xs_ref[...], g_ref[...]).astype(BF16)
        a = _silu(_mm(hb, wg_ref[...])) * _mm(hb, wu_ref[...])
        acc_ref[...] += _mm(a.astype(BF16), wd_ref[...])

    @pl.when(used & (c == last_c))
    def _():
        out_ref[...] = acc_ref[...]

    @pl.when(jnp.logical_not(used) & (c == last_c))
    def _():
        out_ref[...] = jnp.zeros(out_ref.shape, F32)


def _experts(block_exp, n_used, xs, g, wg, wu, wd):
    n_slots = xs.shape[0]
    bm = MOE_SLOT_BLOCK
    fc = FF_CHUNK
    grid_spec = pltpu.PrefetchScalarGridSpec(
        num_scalar_prefetch=2, grid=(n_slots // bm, D_FF // fc),
        in_specs=[pl.BlockSpec((bm, D_MODEL), lambda b, c, be, nu: (b, 0)),
                  pl.BlockSpec((1, D_MODEL), lambda b, c, be, nu: (0, 0)),
                  pl.BlockSpec((None, D_MODEL, fc), lambda b, c, be, nu: (be[b], 0, c)),
                  pl.BlockSpec((None, D_MODEL, fc), lambda b, c, be, nu: (be[b], 0, c)),
                  pl.BlockSpec((None, fc, D_MODEL), lambda b, c, be, nu: (be[b], c, 0))],
        out_specs=pl.BlockSpec((bm, D_MODEL), lambda b, c, be, nu: (b, 0)),
        scratch_shapes=[pltpu.VMEM((bm, D_MODEL), F32)])
    return pl.pallas_call(
        _experts_kernel, grid_spec=grid_spec, out_shape=jax.ShapeDtypeStruct((n_slots, D_MODEL), F32),
        compiler_params=_params(2), name="moe_experts")(block_exp, n_used, xs, g, wg, wu, wd)


def _combine_kernel(dest_ref, dest_next_ref, x_ref, rf_ref, ys_hbm, out_ref, buf_ref, sems):
    n_tok = x_ref.shape[0]
    i = pl.program_id(0)
    slot = i % 2

    def gather(idx_ref, s):
        def issue(r, c):
            for k in range(2):
                _row_copy(ys_hbm, idx_ref[2 * r + k], buf_ref.at[s, k], r, sems.at[s]).start(priority=k)
            return c
        lax.fori_loop(0, n_tok, issue, 0, unroll=MOE_DMA_UNROLL)

    @pl.when(i == 0)
    def _():
        gather(dest_ref, 0)

    @pl.when(i + 1 < pl.num_programs(0))
    def _():
        gather(dest_next_ref, 1 - slot)

    for k in range(2):
        pltpu.make_async_copy(ys_hbm.at[pl.ds(0, n_tok), :], buf_ref.at[slot, k], sems.at[slot]).wait()
    gates = rf_ref[...]
    out_ref[...] = x_ref[...] + (buf_ref[slot, 0] * gates[:, 0:1] + buf_ref[slot, 1] * gates[:, 1:2])


def _combine(dest_flat, x2, rf, ys):
    t = x2.shape[0]
    tc = MOE_COMBINE_TILE
    n = t // tc
    return pl.pallas_call(
        _combine_kernel, grid=(n,),
        in_specs=[pl.BlockSpec((2 * tc,), lambda i: (i,), memory_space=pltpu.SMEM),
                  pl.BlockSpec((2 * tc,), lambda i: (jnp.minimum(i + 1, n - 1),), memory_space=pltpu.SMEM),
                  pl.BlockSpec((tc, D_MODEL), lambda i: (i, 0)), pl.BlockSpec((tc, LANES), lambda i: (i, 0)),
                  pl.BlockSpec(memory_space=pl.ANY)],
        out_specs=pl.BlockSpec((tc, D_MODEL), lambda i: (i, 0)),
        out_shape=jax.ShapeDtypeStruct((t, D_MODEL), F32),
        scratch_shapes=[pltpu.VMEM((2, 2, tc, D_MODEL), F32), pltpu.SemaphoreType.DMA((2,))],
        compiler_params=_params(1), name="moe_combine")(dest_flat, dest_flat, x2, rf, ys)


def _moe(x2, g, wr, wg, wu, wd, ltri):
    t = x2.shape[0]
    bm = MOE_SLOT_BLOCK
    n_blocks = (2 * t) // bm + N_EXPERTS
    ri, rf, cnt = _router(x2, g, wr, ltri)
    counts = cnt[0, :N_EXPERTS].astype(I32)
    padded = (counts + bm - 1) // bm * bm
    pend = jnp.cumsum(padded)
    pstart = pend - padded
    dest = (pstart[ri[:, 0:2]] + ri[:, 2:4]).reshape(2 * t)
    n_used = (pend[-1:] // bm).astype(I32)
    starts = jnp.arange(n_blocks, dtype=I32) * bm
    block_exp = jnp.minimum(jnp.sum((pend[None, :] <= starts[:, None]).astype(I32), axis=1), N_EXPERTS - 1)
    xs = _dispatch(pend.astype(I32), padded.astype(I32), dest, x2, n_blocks * bm)
    ys = _experts(block_exp, n_used, xs, g, wg, wu, wd)
    return _combine(dest, x2, rf, ys)


def _const_mats():
    r = np.arange(256)
    gmat = ((r[:, None] // 64) == (r[None, :] // 64)).astype(np.float32) / 64.0
    d = r % 64
    same = (r[:, None] // 64) == (r[None, :] // 64)
    swap = ((d[None, :] < 8) & (d[:, None] == d[None, :] + 8)) | \
           ((d[None, :] >= 8) & (d[None, :] < 16) & (d[:, None] == d[None, :] - 8))
    pmat = (same & swap).astype(np.float32)
    tri = np.tril(np.ones((SSD_CHUNK, SSD_CHUNK), np.float32))
    rsel = np.zeros((LANES, D_SSD), np.float32)
    rep = np.zeros((LANES, SSD_HEADS * SSD_CHUNK), np.float32)
    sel8 = np.zeros((SSD_HEADS, LANES), np.float32)
    for h in range(SSD_HEADS):
        rsel[IDX_DIM + h, h * 64:(h + 1) * 64] = 1.0
        rep[IDX_DIM + h, h * SSD_CHUNK:(h + 1) * SSD_CHUNK] = 1.0
        sel8[h, IDX_DIM + h] = 1.0
    tril = np.tril(np.ones((KEY_TILE, KEY_TILE), np.float32), -1)
    ltri = np.tril(np.ones((TOKEN_TILE, TOKEN_TILE), np.float32), -1)
    to = lambda a: jnp.asarray(a, dtype=BF16)
    return to(gmat), to(pmat), (to(tri), to(rsel), to(rep), to(sel8)), to(tril), to(ltri)


def kernel(x, mem, positions, mix_norm_g, w_in, conv_dw_w, conv_ln_g, conv_ln_b, conv_pw_w, ssd_conv_w, ssd_conv_b, ssd_dt_bias, ssd_a_log, ssd_d, ssd_norm_g, att_q_norm_g, att_k_norm_g, w_out, memx_norm_g, mem_norm_g, memx_w_q, memx_w_kv, memx_q_norm_g, memx_k_norm_g, memx_w_o, ffn_norm_g, ffn_w_gate, ffn_w_up, ffn_w_down, moe_w_router, moe_w_gate, moe_w_up, moe_w_down):
    batch, seq, d = x.shape
    nm = mem.shape[1]
    depth = w_in.shape[0]
    t = batch * seq
    assert d == D_MODEL and seq % TOKEN_TILE == 0 and seq % SEQ_TILE == 0 and seq % KEY_TILE == 0
    assert TOKEN_TILE % KEY_TILE == 0 and KEY_TILE % Q_TILE == 0 and D_FF % FF_CHUNK == 0

    gmat, pmat, ssd_consts, tril, ltri = _const_mats()

    inv = ROPE_THETA ** (-jnp.arange(0, ROT_DIM, 2, dtype=F32) / ROT_DIM)
    ang = positions.astype(F32).reshape(1, t) * inv.reshape(-1, 1)
    ct, st = jnp.cos(ang), jnp.sin(ang)

    x2 = x.reshape(t, d)
    mem2 = mem.reshape(batch * nm, d)
    row = lambda v: v.reshape(1, -1)
    rep64 = lambda v: jnp.repeat(v, SSD_HEAD_DIM).reshape(1, -1)
    dt_lanes = lambda v: jnp.concatenate(
        [jnp.zeros((IDX_DIM,), F32), v, jnp.zeros((LANES - IDX_DIM - SSD_HEADS,), F32)]).reshape(1, LANES)
    o = _OFF

    for i in range(depth):
        w = w_in[i].astype(BF16)
        wstd = jnp.concatenate([w[:, o[0]:o[3]], w[:, o[5]:o[6]], w[:, o[8]:o[9]], w[:, o[3]:o[4]],
                                jnp.zeros((d, LANES - IDX_DIM - SSD_HEADS), BF16)], axis=1)
        wtrn = jnp.concatenate([w[:, o[7]:o[8]].T, w[:, o[4]:o[5]].T, w[:, o[6]:o[7]].T, w[:, o[9]:o[10]].T,
                                jnp.zeros((N_TRN - 1032, d), BF16)], axis=0)
        uconv, z, xbc, kn, small, kib, qit, qnt, vt, wt = _in_proj(
            x2, row(mix_norm_g[i]), wstd, wtrn, ct, st, att_q_norm_g[i].reshape(64, 1),
            jnp.tile(att_k_norm_g[i], ATT_HEADS).reshape(1, 256), gmat, pmat)

        dw = jnp.concatenate([conv_dw_w[i], jnp.zeros((32 - CONV_WIDTH, D_CONV), F32)], axis=0)
        y_conv = _conformer_conv(uconv, dw, row(conv_ln_g[i]), row(conv_ln_b[i]), conv_pw_w[i].astype(BF16),
                                 batch, seq)
        cw = jnp.concatenate([ssd_conv_w[i], jnp.zeros((8 - SSD_CONV, D_XBC), F32)], axis=0)
        y_ssd = _ssd(z, xbc, small, cw, row(ssd_conv_b[i]), dt_lanes(ssd_dt_bias[i]), dt_lanes(ssd_a_log[i]),
                     rep64(ssd_d[i]), row(ssd_norm_g[i]), ssd_consts, batch, seq)
        y_att = _dsa(qit, qnt, wt, kib, kn, vt, tril, batch, seq)

        kbd, vbd = _mem_prep(mem2, row(mem_norm_g[i]), memx_w_kv[i].astype(BF16),
                             jnp.tile(memx_k_norm_g[i], MEM_HEADS).reshape(1, 256), gmat, batch, nm)
        x2 = _mix_memx(x2, y_conv, y_ssd, y_att, w_out[i].astype(BF16), row(memx_norm_g[i]),
                       memx_w_q[i].astype(BF16), jnp.tile(memx_q_norm_g[i], MEM_HEADS).reshape(1, 256), gmat,
                       kbd, vbd, memx_w_o[i].astype(BF16), seq)

        j = i // 2
        if i % 2 == 0:
            x2 = _ffn(x2, row(ffn_norm_g[i]), ffn_w_gate[j].astype(BF16), ffn_w_up[j].astype(BF16),
                      ffn_w_down[j].astype(BF16))
        else:
            wr = jnp.concatenate([moe_w_router[j], jnp.zeros((d, LANES - N_EXPERTS), F32)], axis=1)
            x2 = _moe(x2, row(ffn_norm_g[i]), wr, moe_w_gate[j].astype(BF16), moe_w_up[j].astype(BF16),
                      moe_w_down[j].astype(BF16), ltri)
    return x2.reshape(batch, seq, d)
```

```python
import functools
import math

import numpy as np
import jax
import jax.numpy as jnp
from jax import lax
from jax.experimental import pallas as pl
from jax.experimental.pallas import tpu as pltpu

F32 = jnp.float32
BF16 = jnp.bfloat16
I32 = jnp.int32

EPS = 1e-6
D_MODEL = 1024
D_CONV = 256
CONV_WIDTH = 31
SSD_HEAD_DIM = 64
D_SSD = 512
SSD_HEADS = 8
SSD_GROUPS = 2
SSD_STATE = 128
SSD_CONV = 4
SSD_CHUNK = 128
D_XBC = D_SSD + 2 * SSD_GROUPS * SSD_STATE
ATT_HEAD_DIM = 64
D_ATT = 256
ATT_HEADS = 4
IDX_HEADS = 8
IDX_DIM = 64
TOPK_MAX = 256
ROPE_THETA = 500000.0
ROT_DIM = ATT_HEAD_DIM // 4
MEM_HEADS = 4
MEM_HEAD_DIM = 64
D_MEMATT = 256
D_FF = 2816
N_EXPERTS = 8

LANES = 128
SUBLANES = 8
VMEM_LIMIT_BYTES = 56 * 1024 * 1024

TOKEN_TILE = 512
MIX_TILE = 1024
SEQ_TILE = 512
Q_TILE = 256
KEY_TILE = 256
FF_CHUNK = 1408
MOE_SLOT_BLOCK = 512
MOE_DISPATCH_TILE = 512
MOE_COMBINE_TILE = 256
MOE_DMA_UNROLL = 8
CONV_ROWS = 64
CONV_HALO = 32
SSD_HALO = 8
INT_MIN = -2 ** 31
LOG2_E = math.log2(math.e)
NEG_INF = float("-inf")

_SZ = [2 * D_CONV, D_SSD, D_XBC, SSD_HEADS, D_ATT, D_ATT, D_ATT, IDX_HEADS * IDX_DIM, IDX_DIM, IDX_HEADS]
_OFF = np.concatenate([[0], np.cumsum(_SZ)]).tolist()
N_STD = 2 * D_CONV + D_SSD + D_XBC + D_ATT + LANES
N_TRN = IDX_HEADS * IDX_DIM + D_ATT + D_ATT + 16


def _mm(a, b):
    return jnp.dot(a, b, preferred_element_type=F32)


def _mm_nt(a, b):
    return lax.dot_general(a, b, (((1,), (1,)), ((), ())), preferred_element_type=F32)


def _split3(a):
    hi = a.astype(BF16)
    r = a - hi.astype(F32)
    mid = r.astype(BF16)
    lo = (r - mid.astype(F32)).astype(BF16)
    return hi, mid, lo


def _mm_f32_lhs(a, b_exact):
    hi, mid, lo = _split3(a)
    return (_mm(hi, b_exact) + _mm(mid, b_exact)) + _mm(lo, b_exact)


def _mm_f32_rhs(a_exact, b):
    hi, mid, lo = _split3(b)
    return (_mm(a_exact, hi) + _mm(a_exact, mid)) + _mm(a_exact, lo)


def _mm_nt_f32_rhs(a_exact, b):
    hi, mid, lo = _split3(b)
    return (_mm_nt(a_exact, hi) + _mm_nt(a_exact, mid)) + _mm_nt(a_exact, lo)


def _mm_f32_f32(a, b):
    a1, a2, a3 = _split3(a)
    b1, b2, b3 = _split3(b)
    return (_mm(a1, b1) + (_mm(a1, b2) + _mm(a2, b1))) + ((_mm(a1, b3) + _mm(a2, b2)) + _mm(a3, b1))


def _silu(x):
    return x * jax.nn.sigmoid(x)


def _rms_rows(x, g):
    ms = jnp.mean(x * x, axis=-1, keepdims=True)
    return (x * lax.rsqrt(ms + EPS)) * g


def _fold_rows(x, op):
    parts = [x[r:r + SUBLANES] for r in range(0, x.shape[0], SUBLANES)]
    while len(parts) > 1:
        nxt = [op(parts[k], parts[k + 1]) for k in range(0, len(parts) - 1, 2)]
        if len(parts) % 2:
            nxt.append(parts[-1])
        parts = nxt
    return parts[0]


def _params(n_axes):
    return pltpu.CompilerParams(dimension_semantics=("arbitrary",) * n_axes,
                                vmem_limit_bytes=VMEM_LIMIT_BYTES)


def _full(shape):
    n = len(shape)
    return pl.BlockSpec(shape, lambda *_: (0,) * n)


def _in_proj_kernel(x_ref, g_ref, wstd_ref, wtrn_ref, ct_ref, st_ref, qg_ref, kg_ref,
                    gmat_ref, pmat_ref,
                    uconv_ref, z_ref, xbc_ref, kn_ref, small_ref, kib_ref, qit_ref, qnt_ref, vt_ref, wt_ref):
    tm = x_ref.shape[0]
    hb = _rms_rows(x_ref[...], g_ref[...]).astype(BF16)
    uconv_ref[...] = _mm(hb, wstd_ref[:, 0:512])
    z_ref[...] = _mm(hb, wstd_ref[:, 512:1024])
    xbc_ref[...] = _mm(hb, wstd_ref[:, 1024:2048])

    ct = ct_ref[...]
    st = st_ref[...]
    rest = ATT_HEAD_DIM - ROT_DIM
    c128 = jnp.concatenate([ct, ct, jnp.ones((rest, tm), F32)] * 2, axis=0).T
    s128 = jnp.concatenate([-st, st, jnp.zeros((rest, tm), F32)] * 2, axis=0).T
    kraw = _mm(hb, wstd_ref[:, 2048:2304])
    kn = (kraw * lax.rsqrt(_mm_f32_lhs(kraw * kraw, gmat_ref[...]) + EPS)) * kg_ref[...]
    kn = kn * jnp.concatenate([c128, c128], axis=1) + \
        _mm_f32_lhs(kn, pmat_ref[...]) * jnp.concatenate([s128, s128], axis=1)
    kn_ref[...] = kn.astype(BF16)

    sm = _mm(hb, wstd_ref[:, 2304:2432])
    lane = lax.broadcasted_iota(I32, (tm, LANES), 1)
    is_idx = lane < IDX_DIM
    sm = sm * jnp.where(is_idx, c128, 1.0) + _mm_f32_lhs(sm, pmat_ref[0:LANES, 0:LANES]) * jnp.where(is_idx, s128, 0.0)
    small_ref[...] = sm
    kib_ref[...] = jnp.where(is_idx, sm, 0.0).astype(BF16)

    out_t = _mm_nt(wtrn_ref[...], hb)

    def rope_t(blk):
        x1 = blk[0:8]
        x2 = blk[8:16]
        return jnp.concatenate([x1 * ct - x2 * st, x2 * ct + x1 * st, blk[16:64]], axis=0)

    qi = [rope_t(out_t[h * 64:(h + 1) * 64]) for h in range(IDX_HEADS)]
    qit_ref[...] = jnp.concatenate(qi, axis=0).astype(BF16)

    qn = []
    for h in range(ATT_HEADS):
        blk = out_t[512 + h * 64:512 + (h + 1) * 64]
        ms = jnp.mean(blk * blk, axis=0, keepdims=True)
        blk = (blk * lax.rsqrt(ms + EPS)) * qg_ref[...]
        qn.append(rope_t(blk) * (ATT_HEAD_DIM ** -0.5 * LOG2_E))
    qnt_ref[...] = jnp.concatenate(qn, axis=0).astype(BF16)

    vt = out_t[768:1024].astype(BF16)
    for c in range(tm // KEY_TILE):
        vt_ref[c] = vt[:, c * KEY_TILE:(c + 1) * KEY_TILE]
    wt_ref[...] = out_t[1024:1032] * (IDX_HEADS ** -0.5 * IDX_DIM ** -0.5)


def _in_proj(x2, g, wstd, wtrn, ct, st, qg, kg, gmat, pmat):
    t = x2.shape[0]
    tm = TOKEN_TILE
    row = lambda w: pl.BlockSpec((tm, w), lambda i: (i, 0))
    col = lambda r: pl.BlockSpec((r, tm), lambda i: (0, i))
    out_shape = (
        jax.ShapeDtypeStruct((t, 512), F32), jax.ShapeDtypeStruct((t, 512), F32),
        jax.ShapeDtypeStruct((t, 1024), F32), jax.ShapeDtypeStruct((t, 256), BF16),
        jax.ShapeDtypeStruct((t, LANES), F32), jax.ShapeDtypeStruct((t, LANES), BF16),
        jax.ShapeDtypeStruct((512, t), BF16), jax.ShapeDtypeStruct((256, t), BF16),
        jax.ShapeDtypeStruct((t // KEY_TILE, 256, KEY_TILE), BF16), jax.ShapeDtypeStruct((8, t), F32))
    out_specs = (row(512), row(512), row(1024), row(256), row(LANES), row(LANES), col(512), col(256),
                 pl.BlockSpec((tm // KEY_TILE, 256, KEY_TILE), lambda i: (i, 0, 0)), col(8))
    in_specs = [row(D_MODEL), _full((1, D_MODEL)), _full(wstd.shape), _full(wtrn.shape),
                col(8), col(8), _full((64, 1)), _full((1, 256)), _full((256, 256)), _full((256, 256))]
    return pl.pallas_call(_in_proj_kernel, grid=(t // tm,), in_specs=in_specs, out_specs=out_specs,
                          out_shape=out_shape, compiler_params=_params(1), name="in_proj")(
        x2, g, wstd, wtrn, ct, st, qg, kg, gmat, pmat)


def _conv_kernel(u_ref, dw_ref, lg_ref, lb_ref, pw_ref, out_ref, hp_ref, hs_ref):
    ts = u_ref.shape[0]

    @pl.when(pl.program_id(1) == 0)
    def _():
        hp_ref[0:CONV_HALO, :] = jnp.zeros((CONV_HALO, D_CONV), F32)

    u = u_ref[...]
    hp_ref[CONV_HALO:CONV_HALO + ts, :] = u[:, :D_CONV] * jax.nn.sigmoid(u[:, D_CONV:])
    n_sh = hs_ref.shape[1]
    for s in range(1, SUBLANES):
        hs_ref[s - 1] = hp_ref[pl.ds(s, n_sh), :]
    base = CONV_HALO - (CONV_WIDTH - 1)

    def tap(r0, k):
        a, s = divmod(base + k, SUBLANES)
        if s == 0:
            return hp_ref[pl.ds(r0 + a * SUBLANES, CONV_ROWS), :]
        return hs_ref[s - 1, pl.ds(r0 + a * SUBLANES, CONV_ROWS), :]

    for r0 in range(0, ts, CONV_ROWS):
        acc = dw_ref[0:1, :] * tap(r0, 0)
        for k in range(1, CONV_WIDTH):
            acc = acc + dw_ref[k:k + 1, :] * tap(r0, k)
        mu = jnp.mean(acc, axis=-1, keepdims=True)
        xc = acc - mu
        y = xc * lax.rsqrt(jnp.mean(xc * xc, axis=-1, keepdims=True) + EPS)
        y = _silu(y * lg_ref[...] + lb_ref[...])
        out_ref[pl.ds(r0, CONV_ROWS), :] = _mm(y.astype(BF16), pw_ref[...])
    hp_ref[0:CONV_HALO, :] = hp_ref[ts:ts + CONV_HALO, :]


def _conformer_conv(uconv, dw, lg, lb, pw, batch, seq):
    ts = SEQ_TILE
    ns = seq // ts
    return pl.pallas_call(
        _conv_kernel, grid=(batch, ns),
        in_specs=[pl.BlockSpec((ts, 512), lambda b, s: (b * ns + s, 0)), _full((32, D_CONV)), _full((1, D_CONV)),
                  _full((1, D_CONV)), _full((D_CONV, D_CONV))],
        out_specs=pl.BlockSpec((ts, D_CONV), lambda b, s: (b * ns + s, 0)),
        out_shape=jax.ShapeDtypeStruct((batch * seq, D_CONV), F32),
        scratch_shapes=[pltpu.VMEM((CONV_HALO + ts, D_CONV), F32),
                        pltpu.VMEM((SUBLANES - 1, CONV_HALO + ts - SUBLANES, D_CONV), F32)],
        compiler_params=_params(2), name="conformer_conv")(uconv, dw, lg, lb, pw)


def _ssd_kernel(z_ref, xbc_ref, small_ref, cw_ref, cb_ref, dtb_ref, alog_ref, dskip_ref, ng_ref,
                tri_ref, rsel_ref, rep_ref, sel8_ref, out_ref, xp_ref, state_ref):
    ts = z_ref.shape[0]
    L = SSD_CHUNK

    @pl.when(pl.program_id(1) == 0)
    def _():
        xp_ref[0:SSD_HALO, :] = jnp.zeros((SSD_HALO, D_XBC), F32)
        state_ref[...] = jnp.zeros(state_ref.shape, F32)

    xp_ref[SSD_HALO:SSD_HALO + ts, :] = xbc_ref[...]
    lane1 = lax.broadcasted_iota(I32, (1, LANES), 1)
    is_dt = (lane1 >= IDX_DIM) & (lane1 < IDX_DIM + SSD_HEADS)
    a_neg = jnp.where(is_dt, -jnp.exp(alog_ref[...]), 0.0)
    ii = lax.broadcasted_iota(I32, (L, L), 0)
    jj = lax.broadcasted_iota(I32, (L, L), 1)
    causal = ii >= jj
    first_half = jj < SSD_HEAD_DIM
    gw = D_SSD // SSD_GROUPS

    for c in range(ts // L):
        r0 = c * L
        ext = xp_ref[pl.ds(r0, L + SSD_HALO), :]
        conv = cb_ref[...] + cw_ref[SSD_CONV - 1:SSD_CONV, :] * ext[SSD_HALO:, :]
        for k in range(SSD_CONV - 1):
            conv = conv + cw_ref[k:k + 1, :] * pltpu.roll(ext, SSD_CONV - 1 - k, axis=0)[SSD_HALO:, :]
        xa = _silu(conv)
        xs = xa[:, 0:D_SSD]
        bm = xa[:, D_SSD:D_SSD + 256]
        cm = xa[:, D_SSD + 256:D_SSD + 512]

        dt_arg = small_ref[pl.ds(r0, L), :] + dtb_ref[...]
        dt_s = jnp.maximum(dt_arg, 0.0) + jnp.log1p(jnp.exp(-jnp.abs(dt_arg)))
        acs_s = _mm_f32_rhs(tri_ref[...], dt_s * a_neg)
        dt = _mm_f32_lhs(dt_s, rsel_ref[...])
        acs = _mm_f32_lhs(acs_s, rsel_ref[...])
        acs_wide = _mm_f32_lhs(acs_s, rep_ref[...])
        acs_t = _mm_nt_f32_rhs(sel8_ref[...], acs_s)
        last = acs[L - 1:L, :]
        decay_end = jnp.exp(last - acs)
        decay_in = jnp.exp(acs)
        xd = xs * dt
        xdb = xd.astype(BF16)

        y_parts = []
        for g in range(SSD_GROUPS):
            bg = bm[:, g * SSD_STATE:(g + 1) * SSD_STATE]
            cgb = cm[:, g * SSD_STATE:(g + 1) * SSD_STATE].astype(BF16)
            cb = _mm_nt(cgb, bg.astype(BF16))
            pair_out = []
            for pr in range(2):
                h0 = g * 4 + pr * 2
                xpair = xdb[:, h0 * 64:(h0 + 2) * 64]
                outs = []
                for h in (h0, h0 + 1):
                    seg = acs_wide[:, h * L:(h + 1) * L] - acs_t[h:h + 1, :]
                    m_h = (cb * jnp.exp(jnp.where(causal, seg, NEG_INF))).astype(BF16)
                    outs.append(_mm(m_h, xpair))
                pair_out.append(jnp.where(first_half, outs[0], outs[1]))
            y_diag = jnp.concatenate(pair_out, axis=1)
            gs = slice(g * gw, (g + 1) * gw)
            st = state_ref[:, gs]
            y_off = _mm(cgb, st.astype(BF16)) * decay_in[:, gs]
            contrib = _mm(bg.T.astype(BF16), (xd[:, gs] * decay_end[:, gs]).astype(BF16))
            state_ref[:, gs] = jnp.exp(last[:, gs]) * st + contrib
            y_parts.append(y_diag + y_off)
        y = jnp.concatenate(y_parts, axis=1) + xs * dskip_ref[...]
        y = y * _silu(z_ref[pl.ds(r0, L), :])
        outs = []
        for g in range(SSD_GROUPS):
            gs = slice(g * gw, (g + 1) * gw)
            outs.append(_rms_rows(y[:, gs], ng_ref[:, gs]))
        out_ref[pl.ds(r0, L), :] = jnp.concatenate(outs, axis=1)
    xp_ref[0:SSD_HALO, :] = xp_ref[ts:ts + SSD_HALO, :]


def _ssd(z, xbc, small, cw, cb, dtb, alog, dskip, ng, consts, batch, seq):
    ts = SEQ_TILE
    ns = seq // ts
    tri, rsel, rep, sel8 = consts
    row = lambda w: pl.BlockSpec((ts, w), lambda b, s: (b * ns + s, 0))
    return pl.pallas_call(
        _ssd_kernel, grid=(batch, ns),
        in_specs=[row(512), row(1024), row(LANES), _full((8, D_XBC)), _full((1, D_XBC)), _full((1, LANES)),
                  _full((1, LANES)), _full((1, 512)), _full((1, 512)), _full(tri.shape), _full(rsel.shape),
                  _full(rep.shape), _full(sel8.shape)],
        out_specs=row(512),
        out_shape=jax.ShapeDtypeStruct((batch * seq, 512), F32),
        scratch_shapes=[pltpu.VMEM((SSD_HALO + ts, D_XBC), F32), pltpu.VMEM((SSD_STATE, D_SSD), F32)],
        compiler_params=_params(2), name="ssd_mixer")(z, xbc, small, cw, cb, dtb, alog, dskip, ng, tri, rsel, rep, sel8)


def _dsa_kernel(qit_ref, qnt_ref, wt_ref, kib_ref, kn_ref, vt_ref, tril_ref, out_ref, keys_ref, keysb_ref, s_ref,
                acc_ref, *, ksel):
    qb = Q_TILE
    kb = KEY_TILE
    i = pl.program_id(1)
    nsb = (i * qb + qb + kb - 1) // kb
    qpos = i * qb + lax.broadcasted_iota(I32, (kb, qb), 1)
    krow = lax.broadcasted_iota(I32, (kb, qb), 0)

    rhs_idx = jnp.concatenate([qit_ref[h * IDX_DIM:(h + 1) * IDX_DIM, :] for h in range(IDX_HEADS)], axis=1)
    rhs_idx = jnp.concatenate([rhs_idx, jnp.zeros((LANES - IDX_DIM, IDX_HEADS * qb), BF16)], axis=0)
    wt = wt_ref[...]

    def score_tile(j, carry):
        r = pl.multiple_of(j * kb, kb)
        lg = _mm(kib_ref[pl.ds(r, kb), :], rhs_idx)
        sc = wt[0:1, :] * jnp.maximum(lg[:, 0:qb], 0.0)
        for h in range(1, IDX_HEADS):
            sc = sc + wt[h:h + 1, :] * jnp.maximum(lg[:, h * qb:(h + 1) * qb], 0.0)
        sc = jnp.where(j * kb + krow <= qpos, sc, NEG_INF)
        keys_ref[pl.ds(r, kb), :] = sc
        keysb_ref[pl.ds(r, kb), :] = sc.astype(BF16)
        return carry

    lax.fori_loop(0, nsb, score_tile, 0)

    def key_to_float(k):
        return pltpu.bitcast(k ^ (lax.shift_right_arithmetic(k, 31) & 0x7FFFFFFF), F32)

    n_count = jnp.where(i * qb + qb <= ksel, 0, nsb)

    def count(cand, strict):
        def body(j, acc):
            kt = keys_ref[pl.ds(pl.multiple_of(j * kb, kb), kb), :]
            hit = (kt > cand) if strict else (kt >= cand)
            return acc + _fold_rows(jnp.where(hit, 1.0, 0.0), jnp.add)
        acc = lax.fori_loop(0, n_count, body, jnp.zeros((SUBLANES, qb), F32))
        return jnp.sum(acc, axis=0, keepdims=True)

    pack = 2 * SUBLANES
    one_b = jnp.ones((pack, qb), BF16)
    zero_b = jnp.zeros((pack, qb), BF16)

    def count_coarse(cand_b):
        def body(j, acc):
            kt = keysb_ref[pl.ds(pl.multiple_of(j * kb, kb), kb), :]
            parts = [jnp.where(kt[g * pack:(g + 1) * pack] >= cand_b, one_b, zero_b) for g in range(kb // pack)]
            while len(parts) > 1:
                parts = [parts[k] + parts[k + 1] for k in range(0, len(parts), 2)]
            return acc + parts[0]
        acc = lax.fori_loop(0, n_count, body, zero_b)
        return jnp.sum(acc.astype(F32), axis=0, keepdims=True)

    def coarse_step(it, t16):
        cand16 = t16 | lax.shift_left(jnp.int32(1), 15 - it)
        k = lax.shift_left(cand16, 16) ^ INT_MIN
        cand = pltpu.bitcast(k ^ (lax.shift_right_arithmetic(k, 31) & 0x7FFF0000), F32)
        cnt = count_coarse(jnp.concatenate([cand, cand], axis=0).astype(BF16))
        return jnp.where(cnt >= float(ksel), cand16, t16)

    t16 = lax.fori_loop(0, 16, coarse_step, jnp.zeros((SUBLANES, qb), I32))
    k1 = lax.shift_left(t16, 16) ^ INT_MIN
    c1 = pltpu.bitcast(k1 ^ (lax.shift_right_arithmetic(k1, 31) & 0x7FFF0000), F32)
    bits1 = pltpu.bitcast(c1, I32)
    key1 = bits1 ^ (lax.shift_right_arithmetic(bits1, 31) & 0x7FFFFFFF)
    def fine_step(it, carry):
        lo, hi = carry
        mid = lo + lax.shift_right_arithmetic(hi - lo + 1, 1)
        ok = count(key_to_float(mid)[0:1, :], False) >= float(ksel)
        return jnp.where(ok, mid, lo), jnp.where(ok, hi, mid - 1)

    lo, _ = lax.fori_loop(0, 17, fine_step, (key1 - 0x8000, key1 + 0x10000))
    tau = key_to_float(lo)[0:1, :]
    takes_all = qpos[0:1, :] < ksel
    tau = jnp.where(takes_all, NEG_INF, tau)
    need = jnp.where(takes_all, 0.0, float(ksel) - count(tau, True))

    zblk = jnp.zeros((ATT_HEAD_DIM, qb), BF16)
    rhs_att = jnp.concatenate(
        [jnp.concatenate([qnt_ref[h * 64:(h + 1) * 64, :] if hh == h else zblk for hh in range(ATT_HEADS)], axis=1)
         for h in range(ATT_HEADS)], axis=0)

    def pass1(j, carry):
        m, eq_before = carry
        r = pl.multiple_of(j * kb, kb)
        s = _mm(kn_ref[pl.ds(r, kb), :], rhs_att)
        kt = keys_ref[pl.ds(r, kb), :]
        eq = kt == tau
        eqf = jnp.where(eq, 1.0, 0.0)
        rank = eq_before + _mm(tril_ref[...], eqf.astype(BF16))
        sel = (kt > tau) | (eq & (rank < need))
        s = jnp.concatenate([jnp.where(sel, s[:, h * qb:(h + 1) * qb], NEG_INF) for h in range(ATT_HEADS)], axis=1)
        s_ref[pl.ds(r, kb), :] = s
        return (jnp.maximum(m, _fold_rows(s, jnp.maximum)),
                eq_before + jnp.sum(_fold_rows(eqf, jnp.add), axis=0, keepdims=True))

    m8, _ = lax.fori_loop(0, nsb, pass1,
                          (jnp.full((SUBLANES, ATT_HEADS * qb), NEG_INF, F32), jnp.zeros((1, qb), F32)))
    m = jnp.max(m8, axis=0, keepdims=True)

    acc_ref[...] = jnp.zeros(acc_ref.shape, F32)

    def pass2(j, l8):
        p = jnp.exp2(s_ref[pl.ds(pl.multiple_of(j * kb, kb), kb), :] - m)
        pb = p.astype(BF16)
        vt = vt_ref[j]
        for h in range(ATT_HEADS):
            hs = slice(h * ATT_HEAD_DIM, (h + 1) * ATT_HEAD_DIM)
            acc_ref[hs, :] += _mm(vt[hs, :], pb[:, h * qb:(h + 1) * qb])
        return l8 + _fold_rows(p, jnp.add)

    l = jnp.sum(lax.fori_loop(0, nsb, pass2, jnp.zeros((SUBLANES, ATT_HEADS * qb), F32)), axis=0, keepdims=True)
    o_t = jnp.concatenate(
        [acc_ref[h * 64:(h + 1) * 64, :] / l[:, h * qb:(h + 1) * qb] for h in range(ATT_HEADS)],
        axis=0)
    out_ref[...] = o_t.T


def _dsa(qit, qnt, wt, kib, kn, vt, tril, batch, seq):
    qb = Q_TILE
    nq = seq // qb
    nkb = seq // KEY_TILE
    ksel = min(TOPK_MAX, seq // 4)
    colq = lambda r: pl.BlockSpec((r, qb), lambda b, i: (0, b * nq + i))
    return pl.pallas_call(
        functools.partial(_dsa_kernel, ksel=ksel), grid=(batch, nq),
        in_specs=[colq(512), colq(256), colq(8),
                  pl.BlockSpec((seq, LANES), lambda b, i: (b, 0)), pl.BlockSpec((seq, 256), lambda b, i: (b, 0)),
                  pl.BlockSpec((nkb, 256, KEY_TILE), lambda b, i: (b, 0, 0)), _full((KEY_TILE, KEY_TILE))],
        out_specs=pl.BlockSpec((qb, D_ATT), lambda b, i: (b * nq + i, 0)),
        out_shape=jax.ShapeDtypeStruct((batch * seq, D_ATT), F32),
        scratch_shapes=[pltpu.VMEM((seq, qb), F32), pltpu.VMEM((seq, qb), BF16),
                        pltpu.VMEM((seq, ATT_HEADS * qb), F32),
                        pltpu.VMEM((D_ATT, qb), F32)],
        compiler_params=_params(2), name="dsa_attention")(qit, qnt, wt, kib, kn, vt, tril)


def _mem_prep_kernel(mem_ref, g_ref, wkv_ref, kg_ref, gmat_ref, kbd_ref, vbd_ref):
    nm = mem_ref.shape[0]
    mb = _rms_rows(mem_ref[...], g_ref[...]).astype(BF16)
    kv = _mm(mb, wkv_ref[...])
    k = kv[:, 0:D_MEMATT]
    kn = (k * lax.rsqrt(_mm_f32_lhs(k * k, gmat_ref[...]) + EPS)) * kg_ref[...]
    k_t = kn.T.astype(BF16)
    zero = jnp.zeros((MEM_HEAD_DIM, nm), BF16)
    kbd_ref[...] = jnp.concatenate(
        [jnp.concatenate([k_t[h * 64:(h + 1) * 64, :] if hh == h else zero for hh in range(MEM_HEADS)], axis=1)
         for h in range(MEM_HEADS)], axis=0)
    v = kv[:, D_MEMATT:]
    lane = lax.broadcasted_iota(I32, (nm, D_MEMATT), 1)
    vbd_ref[...] = jnp.concatenate(
        [jnp.where((lane >= h * 64) & (lane < (h + 1) * 64), v, 0.0) for h in range(MEM_HEADS)],
        axis=0).astype(BF16)


def _mem_prep(mem2, g, wkv, kg, gmat, batch, nm):
    return pl.pallas_call(
        _mem_prep_kernel, grid=(batch,),
        in_specs=[pl.BlockSpec((nm, D_MODEL), lambda b: (b, 0)), _full((1, D_MODEL)), _full((D_MODEL, 512)),
                  _full((1, 256)), _full((256, 256))],
        out_specs=(pl.BlockSpec((None, D_MEMATT, MEM_HEADS * nm), lambda b: (b, 0, 0)),
                   pl.BlockSpec((None, MEM_HEADS * nm, D_MEMATT), lambda b: (b, 0, 0))),
        out_shape=(jax.ShapeDtypeStruct((batch, D_MEMATT, MEM_HEADS * nm), BF16),
                   jax.ShapeDtypeStruct((batch, MEM_HEADS * nm, D_MEMATT), BF16)),
        compiler_params=_params(1), name="mem_prep")(mem2, g, wkv, kg, gmat)


def _mix_memx_kernel(x_ref, yc_ref, ys_ref, ya_ref, wout_ref, g_ref, wq_ref, qg_ref, gmat_ref, kbd_ref, vbd_ref,
                     wo_ref, out_ref):
    nm = kbd_ref.shape[1] // MEM_HEADS
    y_mix = jnp.concatenate([yc_ref[...], ys_ref[...], ya_ref[...]], axis=1).astype(BF16)
    x1 = x_ref[...] + _mm(y_mix, wout_ref[...])
    hb = _rms_rows(x1, g_ref[...]).astype(BF16)
    q = _mm(hb, wq_ref[...])
    qn = (q * lax.rsqrt(_mm_f32_lhs(q * q, gmat_ref[...]) + EPS)) * qg_ref[...]
    s = _mm(qn.astype(BF16), kbd_ref[...]) * (MEM_HEAD_DIM ** -0.5)
    ps = []
    for h in range(MEM_HEADS):
        sh = s[:, h * nm:(h + 1) * nm]
        e = jnp.exp(sh - jnp.max(sh, axis=-1, keepdims=True))
        ps.append(e / jnp.sum(e, axis=-1, keepdims=True))
    o = _mm(jnp.concatenate(ps, axis=1).astype(BF16), vbd_ref[...])
    out_ref[...] = x1 + _mm(o.astype(BF16), wo_ref[...])


def _mix_memx(x2, yc, ys, ya, wout, g, wq, qg, gmat, kbd, vbd, wo, seq):
    t = x2.shape[0]
    tm = min(MIX_TILE, seq)
    per_b = seq // tm
    nm4 = kbd.shape[2]
    row = lambda w: pl.BlockSpec((tm, w), lambda i: (i, 0))
    return pl.pallas_call(
        _mix_memx_kernel, grid=(t // tm,),
        in_specs=[row(D_MODEL), row(256), row(512), row(256), _full((D_MODEL, D_MODEL)), _full((1, D_MODEL)),
                  _full((D_MODEL, 256)), _full((1, 256)), _full((256, 256)),
                  pl.BlockSpec((None, D_MEMATT, nm4), lambda i: (i // per_b, 0, 0)),
                  pl.BlockSpec((None, nm4, D_MEMATT), lambda i: (i // per_b, 0, 0)), _full((256, D_MODEL))],
        out_specs=row(D_MODEL), out_shape=jax.ShapeDtypeStruct((t, D_MODEL), F32),
        compiler_params=_params(1), name="mix_memx")(x2, yc, ys, ya, wout, g, wq, qg, gmat, kbd, vbd, wo)


def _ffn_kernel(x_ref, g_ref, wg_ref, wu_ref, wd_ref, out_ref, acc_ref):
    c = pl.program_id(1)

    @pl.when(c == 0)
    def _():
        acc_ref[...] = jnp.zeros(acc_ref.shape, F32)

    hb = _rms_rows(x_ref[...], g_ref[...]).astype(BF16)
    a = _silu(_mm(hb, wg_ref[...])) * _mm(hb, wu_ref[...])
    acc_ref[...] += _mm(a.astype(BF16), wd_ref[...])

    @pl.when(c == pl.num_programs(1) - 1)
    def _():
        out_ref[...] = x_ref[...] + acc_ref[...]


def _ffn(x2, g, wg, wu, wd):
    t = x2.shape[0]
    tm = TOKEN_TILE
    fc = FF_CHUNK
    return pl.pallas_call(
        _ffn_kernel, grid=(t // tm, D_FF // fc),
        in_specs=[pl.BlockSpec((tm, D_MODEL), lambda i, c: (i, 0)), _full((1, D_MODEL)),
                  pl.BlockSpec((D_MODEL, fc), lambda i, c: (0, c)), pl.BlockSpec((D_MODEL, fc), lambda i, c: (0, c)),
                  pl.BlockSpec((fc, D_MODEL), lambda i, c: (c, 0))],
        out_specs=pl.BlockSpec((tm, D_MODEL), lambda i, c: (i, 0)),
        out_shape=jax.ShapeDtypeStruct((t, D_MODEL), F32),
        scratch_shapes=[pltpu.VMEM((tm, D_MODEL), F32)],
        compiler_params=_params(2), name="ffn_dense")(x2, g, wg, wu, wd)


def _router_kernel(x_ref, g_ref, wr_ref, ltri_ref, ri_ref, rf_ref, cnt_ref, carry_ref):
    tm = x_ref.shape[0]
    lane = lax.broadcasted_iota(I32, (tm, LANES), 1)

    @pl.when(pl.program_id(0) == 0)
    def _():
        carry_ref[...] = jnp.zeros(carry_ref.shape, F32)

    h = _rms_rows(x_ref[...], g_ref[...])
    logits = jnp.where(lane < N_EXPERTS, _mm_f32_f32(h, wr_ref[...]), NEG_INF)
    v1 = jnp.max(logits, axis=-1, keepdims=True)
    i1 = jnp.min(jnp.where(logits == v1, lane, LANES), axis=-1, keepdims=True)
    rest = jnp.where(lane == i1, NEG_INF, logits)
    v2 = jnp.max(rest, axis=-1, keepdims=True)
    i2 = jnp.min(jnp.where(rest == v2, lane, LANES), axis=-1, keepdims=True)
    e2 = jnp.exp(v2 - v1)
    den = 1.0 + e2
    hit = jnp.where((lane == i1) | (lane == i2), 1.0, 0.0)
    before = carry_ref[...] + _mm(ltri_ref[...], hit.astype(BF16))
    r1 = jnp.sum(jnp.where(lane == i1, before, 0.0), axis=-1, keepdims=True).astype(I32)
    r2 = jnp.sum(jnp.where(lane == i2, before, 0.0), axis=-1, keepdims=True).astype(I32)
    carry_ref[...] = carry_ref[...] + jnp.sum(hit, axis=0, keepdims=True)
    cnt_ref[...] = carry_ref[...]
    ri_ref[...] = jnp.where(lane == 0, i1, jnp.where(lane == 1, i2, jnp.where(lane == 2, r1,
                                                                             jnp.where(lane == 3, r2, 0))))
    rf_ref[...] = jnp.where(lane == 0, 1.0 / den, jnp.where(lane == 1, e2 / den, 0.0))


def _router(x2, g, wr, ltri):
    t = x2.shape[0]
    tm = TOKEN_TILE
    row = lambda w: pl.BlockSpec((tm, w), lambda i: (i, 0))
    return pl.pallas_call(
        _router_kernel, grid=(t // tm,),
        in_specs=[row(D_MODEL), _full((1, D_MODEL)), _full((D_MODEL, LANES)), _full((tm, tm))],
        out_specs=(row(LANES), row(LANES), _full((1, LANES))),
        out_shape=(jax.ShapeDtypeStruct((t, LANES), I32), jax.ShapeDtypeStruct((t, LANES), F32),
                   jax.ShapeDtypeStruct((1, LANES), F32)),
        scratch_shapes=[pltpu.VMEM((1, LANES), F32)],
        compiler_params=_params(1), name="moe_router")(x2, g, wr, ltri)


def _row_copy(src, src_row, dst, dst_row, sem):
    return pltpu.make_async_copy(src.at[pl.ds(src_row, 1), :], dst.at[pl.ds(dst_row, 1), :], sem)


def _dispatch_kernel(pend_ref, padded_ref, dest_ref, x_ref, xs_hbm, zero_ref, sem, zero_sem):
    n_tok = x_ref.shape[0]
    bm = zero_ref.shape[0]

    @pl.when(pl.program_id(0) == 0)
    def _():
        zero_ref[...] = jnp.zeros(zero_ref.shape, F32)
        n_slots = xs_hbm.shape[0]
        zero_blocks = [(padded_ref[e] > 0, pend_ref[e] - bm) for e in range(N_EXPERTS)]
        zero_blocks += [(pend_ref[N_EXPERTS - 1] + e * bm < n_slots, pend_ref[N_EXPERTS - 1] + e * bm)
                        for e in range(N_EXPERTS)]
        for cond, start in zero_blocks:
            @pl.when(cond)
            def _():
                dst = xs_hbm.at[pl.ds(pl.multiple_of(start, bm), bm), :]
                pltpu.make_async_copy(zero_ref, dst, zero_sem).start()
        for cond, _ in zero_blocks:
            @pl.when(cond)
            def _():
                pltpu.make_async_copy(zero_ref, xs_hbm.at[pl.ds(0, bm), :], zero_sem).wait()

    def issue(r, c):
        for k in range(2):
            _row_copy(x_ref, r, xs_hbm, dest_ref[2 * r + k], sem).start(priority=k)
        return c

    lax.fori_loop(0, n_tok, issue, 0, unroll=MOE_DMA_UNROLL)
    for _ in range(2):
        pltpu.make_async_copy(x_ref, xs_hbm.at[pl.ds(0, n_tok), :], sem).wait()


def _dispatch(pend, padded, dest_flat, x2, n_slots):
    t = x2.shape[0]
    td = MOE_DISPATCH_TILE
    grid_spec = pltpu.PrefetchScalarGridSpec(
        num_scalar_prefetch=2, grid=(t // td,),
        in_specs=[pl.BlockSpec((2 * td,), lambda i, pe, pa: (i,), memory_space=pltpu.SMEM),
                  pl.BlockSpec((td, D_MODEL), lambda i, pe, pa: (i, 0))],
        out_specs=pl.BlockSpec(memory_space=pl.ANY),
        scratch_shapes=[pltpu.VMEM((MOE_SLOT_BLOCK, D_MODEL), F32), pltpu.SemaphoreType.DMA(()),
                        pltpu.SemaphoreType.DMA(())])
    return pl.pallas_call(
        _dispatch_kernel, grid_spec=grid_spec, out_shape=jax.ShapeDtypeStruct((n_slots, D_MODEL), F32),
        compiler_params=_params(1), name="moe_dispatch")(pend, padded, dest_flat, x2)


def _experts_kernel(bexp_ref, nused_ref, xs_ref, g_ref, wg_ref, wu_ref, wd_ref, out_ref, acc_ref):
    del bexp_ref
    c = pl.program_id(1)
    last_c = pl.num_programs(1) - 1
    used = pl.program_id(0) < nused_ref[0]

    @pl.when(used & (c == 0))
    def _():
        acc_ref[...] = jnp.zeros(acc_ref.shape, F32)

    @pl.when(used)
    def _():
        hb = _rms_rows(xs_ref[...], g_ref[...]).astype(BF16)
        a = _silu(_mm(hb, wg_ref[...])) * _mm(hb, wu_ref[...])
        acc_ref[...] += _mm(a.astype(BF16), wd_ref[...])

    @pl.when(used & (c == last_c))
    def _():
        out_ref[...] = acc_ref[...]

    @pl.when(jnp.logical_not(used) & (c == last_c))
    def _():
        out_ref[...] = jnp.zeros(out_ref.shape, F32)


def _experts(block_exp, n_used, xs, g, wg, wu, wd):
    n_slots = xs.shape[0]
    bm = MOE_SLOT_BLOCK
    fc = FF_CHUNK
    grid_spec = pltpu.PrefetchScalarGridSpec(
        num_scalar_prefetch=2, grid=(n_slots // bm, D_FF // fc),
        in_specs=[pl.BlockSpec((bm, D_MODEL), lambda b, c, be, nu: (b, 0)),
                  pl.BlockSpec((1, D_MODEL), lambda b, c, be, nu: (0, 0)),
                  pl.BlockSpec((None, D_MODEL, fc), lambda b, c, be, nu: (be[b], 0, c)),
                  pl.BlockSpec((None, D_MODEL, fc), lambda b, c, be, nu: (be[b], 0, c)),
                  pl.BlockSpec((None, fc, D_MODEL), lambda b, c, be, nu: (be[b], c, 0))],
        out_specs=pl.BlockSpec((bm, D_MODEL), lambda b, c, be, nu: (b, 0)),
        scratch_shapes=[pltpu.VMEM((bm, D_MODEL), F32)])
    return pl.pallas_call(
        _experts_kernel, grid_spec=grid_spec, out_shape=jax.ShapeDtypeStruct((n_slots, D_MODEL), F32),
        compiler_params=_params(2), name="moe_experts")(block_exp, n_used, xs, g, wg, wu, wd)


def _combine_kernel(dest_ref, dest_next_ref, x_ref, rf_ref, ys_hbm, out_ref, buf_ref, sems):
    n_tok = x_ref.shape[0]
    i = pl.program_id(0)
    slot = i % 2

    def gather(idx_ref, s):
        def issue(r, c):
            for k in range(2):
                _row_copy(ys_hbm, idx_ref[2 * r + k], buf_ref.at[s, k], r, sems.at[s]).start(priority=k)
            return c
        lax.fori_loop(0, n_tok, issue, 0, unroll=MOE_DMA_UNROLL)

    @pl.when(i == 0)
    def _():
        gather(dest_ref, 0)

    @pl.when(i + 1 < pl.num_programs(0))
    def _():
        gather(dest_next_ref, 1 - slot)

    for k in range(2):
        pltpu.make_async_copy(ys_hbm.at[pl.ds(0, n_tok), :], buf_ref.at[slot, k], sems.at[slot]).wait()
    gates = rf_ref[...]
    out_ref[...] = x_ref[...] + (buf_ref[slot, 0] * gates[:, 0:1] + buf_ref[slot, 1] * gates[:, 1:2])


def _combine(dest_flat, x2, rf, ys):
    t = x2.shape[0]
    tc = MOE_COMBINE_TILE
    n = t // tc
    return pl.pallas_call(
        _combine_kernel, grid=(n,),
        in_specs=[pl.BlockSpec((2 * tc,), lambda i: (i,), memory_space=pltpu.SMEM),
                  pl.BlockSpec((2 * tc,), lambda i: (jnp.minimum(i + 1, n - 1),), memory_space=pltpu.SMEM),
                  pl.BlockSpec((tc, D_MODEL), lambda i: (i, 0)), pl.BlockSpec((tc, LANES), lambda i: (i, 0)),
                  pl.BlockSpec(memory_space=pl.ANY)],
        out_specs=pl.BlockSpec((tc, D_MODEL), lambda i: (i, 0)),
        out_shape=jax.ShapeDtypeStruct((t, D_MODEL), F32),
        scratch_shapes=[pltpu.VMEM((2, 2, tc, D_MODEL), F32), pltpu.SemaphoreType.DMA((2,))],
        compiler_params=_params(1), name="moe_combine")(dest_flat, dest_flat, x2, rf, ys)


def _moe(x2, g, wr, wg, wu, wd, ltri):
    t = x2.shape[0]
    bm = MOE_SLOT_BLOCK
    n_blocks = (2 * t) // bm + N_EXPERTS
    ri, rf, cnt = _router(x2, g, wr, ltri)
    counts = cnt[0, :N_EXPERTS].astype(I32)
    padded = (counts + bm - 1) // bm * bm
    pend = jnp.cumsum(padded)
    pstart = pend - padded
    dest = (pstart[ri[:, 0:2]] + ri[:, 2:4]).reshape(2 * t)
    n_used = (pend[-1:] // bm).astype(I32)
    starts = jnp.arange(n_blocks, dtype=I32) * bm
    block_exp = jnp.minimum(jnp.sum((pend[None, :] <= starts[:, None]).astype(I32), axis=1), N_EXPERTS - 1)
    xs = _dispatch(pend.astype(I32), padded.astype(I32), dest, x2, n_blocks * bm)
    ys = _experts(block_exp, n_used, xs, g, wg, wu, wd)
    return _combine(dest, x2, rf, ys)


def _const_mats():
    r = np.arange(256)
    gmat = ((r[:, None] // 64) == (r[None, :] // 64)).astype(np.float32) / 64.0
    d = r % 64
    same = (r[:, None] // 64) == (r[None, :] // 64)
    swap = ((d[None, :] < 8) & (d[:, None] == d[None, :] + 8)) | \
           ((d[None, :] >= 8) & (d[None, :] < 16) & (d[:, None] == d[None, :] - 8))
    pmat = (same & swap).astype(np.float32)
    tri = np.tril(np.ones((SSD_CHUNK, SSD_CHUNK), np.float32))
    rsel = np.zeros((LANES, D_SSD), np.float32)
    rep = np.zeros((LANES, SSD_HEADS * SSD_CHUNK), np.float32)
    sel8 = np.zeros((SSD_HEADS, LANES), np.float32)
    for h in range(SSD_HEADS):
        rsel[IDX_DIM + h, h * 64:(h + 1) * 64] = 1.0
        rep[IDX_DIM + h, h * SSD_CHUNK:(h + 1) * SSD_CHUNK] = 1.0
        sel8[h, IDX_DIM + h] = 1.0
    tril = np.tril(np.ones((KEY_TILE, KEY_TILE), np.float32), -1)
    ltri = np.tril(np.ones((TOKEN_TILE, TOKEN_TILE), np.float32), -1)
    to = lambda a: jnp.asarray(a, dtype=BF16)
    return to(gmat), to(pmat), (to(tri), to(rsel), to(rep), to(sel8)), to(tril), to(ltri)


def kernel(x, mem, positions, mix_norm_g, w_in, conv_dw_w, conv_ln_g, conv_ln_b, conv_pw_w, ssd_conv_w, ssd_conv_b, ssd_dt_bias, ssd_a_log, ssd_d, ssd_norm_g, att_q_norm_g, att_k_norm_g, w_out, memx_norm_g, mem_norm_g, memx_w_q, memx_w_kv, memx_q_norm_g, memx_k_norm_g, memx_w_o, ffn_norm_g, ffn_w_gate, ffn_w_up, ffn_w_down, moe_w_router, moe_w_gate, moe_w_up, moe_w_down):
    batch, seq, d = x.shape
    nm = mem.shape[1]
    depth = w_in.shape[0]
    t = batch * seq
    assert d == D_MODEL and seq % TOKEN_TILE == 0 and seq % SEQ_TILE == 0 and seq % KEY_TILE == 0
    assert TOKEN_TILE % KEY_TILE == 0 and KEY_TILE % Q_TILE == 0 and D_FF % FF_CHUNK == 0

    gmat, pmat, ssd_consts, tril, ltri = _const_mats()

    inv = ROPE_THETA ** (-jnp.arange(0, ROT_DIM, 2, dtype=F32) / ROT_DIM)
    ang = positions.astype(F32).reshape(1, t) * inv.reshape(-1, 1)
    ct, st = jnp.cos(ang), jnp.sin(ang)

    x2 = x.reshape(t, d)
    mem2 = mem.reshape(batch * nm, d)
    row = lambda v: v.reshape(1, -1)
    rep64 = lambda v: jnp.repeat(v, SSD_HEAD_DIM).reshape(1, -1)
    dt_lanes = lambda v: jnp.concatenate(
        [jnp.zeros((IDX_DIM,), F32), v, jnp.zeros((LANES - IDX_DIM - SSD_HEADS,), F32)]).reshape(1, LANES)
    o = _OFF

    for i in range(depth):
        w = w_in[i].astype(BF16)
        wstd = jnp.concatenate([w[:, o[0]:o[3]], w[:, o[5]:o[6]], w[:, o[8]:o[9]], w[:, o[3]:o[4]],
                                jnp.zeros((d, LANES - IDX_DIM - SSD_HEADS), BF16)], axis=1)
        wtrn = jnp.concatenate([w[:, o[7]:o[8]].T, w[:, o[4]:o[5]].T, w[:, o[6]:o[7]].T, w[:, o[9]:o[10]].T,
                                jnp.zeros((N_TRN - 1032, d), BF16)], axis=0)
        uconv, z, xbc, kn, small, kib, qit, qnt, vt, wt = _in_proj(
            x2, row(mix_norm_g[i]), wstd, wtrn, ct, st, att_q_norm_g[i].reshape(64, 1),
            jnp.tile(att_k_norm_g[i], ATT_HEADS).reshape(1, 256), gmat, pmat)

        dw = jnp.concatenate([conv_dw_w[i], jnp.zeros((32 - CONV_WIDTH, D_CONV), F32)], axis=0)
        y_conv = _conformer_conv(uconv, dw, row(conv_ln_g[i]), row(conv_ln_b[i]), conv_pw_w[i].astype(BF16),
                                 batch, seq)
        cw = jnp.concatenate([ssd_conv_w[i], jnp.zeros((8 - SSD_CONV, D_XBC), F32)], axis=0)
        y_ssd = _ssd(z, xbc, small, cw, row(ssd_conv_b[i]), dt_lanes(ssd_dt_bias[i]), dt_lanes(ssd_a_log[i]),
                     rep64(ssd_d[i]), row(ssd_norm_g[i]), ssd_consts, batch, seq)
        y_att = _dsa(qit, qnt, wt, kib, kn, vt, tril, batch, seq)

        kbd, vbd = _mem_prep(mem2, row(mem_norm_g[i]), memx_w_kv[i].astype(BF16),
                             jnp.tile(memx_k_norm_g[i], MEM_HEADS).reshape(1, 256), gmat, batch, nm)
        x2 = _mix_memx(x2, y_conv, y_ssd, y_att, w_out[i].astype(BF16), row(memx_norm_g[i]),
                       memx_w_q[i].astype(BF16), jnp.tile(memx_q_norm_g[i], MEM_HEADS).reshape(1, 256), gmat,
                       kbd, vbd, memx_w_o[i].astype(BF16), seq)

        j = i // 2
        if i % 2 == 0:
            x2 = _ffn(x2, row(ffn_norm_g[i]), ffn_w_gate[j].astype(BF16), ffn_w_up[j].astype(BF16),
                      ffn_w_down[j].astype(BF16))
        else:
            wr = jnp.concatenate([moe_w_router[j], jnp.zeros((d, LANES - N_EXPERTS), F32)], axis=1)
            x2 = _moe(x2, row(ffn_norm_g[i]), wr, moe_w_gate[j].astype(BF16), moe_w_up[j].astype(BF16),
                      moe_w_down[j].astype(BF16), ltri)
    return x2.reshape(batch, seq, d)
```

```python
import functools
import math

import numpy as np
import jax
import jax.numpy as jnp
from jax import lax
from jax.experimental import pallas as pl
from jax.experimental.pallas import tpu as pltpu

F32 = jnp.float32
BF16 = jnp.bfloat16
I32 = jnp.int32

EPS = 1e-6
D_MODEL = 1024
D_CONV = 256
CONV_WIDTH = 31
SSD_HEAD_DIM = 64
D_SSD = 512
SSD_HEADS = 8
SSD_GROUPS = 2
SSD_STATE = 128
SSD_CONV = 4
SSD_CHUNK = 128
D_XBC = D_SSD + 2 * SSD_GROUPS * SSD_STATE
ATT_HEAD_DIM = 64
D_ATT = 256
ATT_HEADS = 4
IDX_HEADS = 8
IDX_DIM = 64
TOPK_MAX = 256
ROPE_THETA = 500000.0
ROT_DIM = ATT_HEAD_DIM // 4
MEM_HEADS = 4
MEM_HEAD_DIM = 64
D_MEMATT = 256
D_FF = 2816
N_EXPERTS = 8

LANES = 128
SUBLANES = 8
VMEM_LIMIT_BYTES = 56 * 1024 * 1024

TOKEN_TILE = 512
MIX_TILE = 1024
SEQ_TILE = 512
Q_TILE = 256
KEY_TILE = 256
FF_CHUNK = 1408
MOE_SLOT_BLOCK = 512
MOE_DISPATCH_TILE = 512
MOE_COMBINE_TILE = 256
MOE_DMA_UNROLL = 8
CONV_ROWS = 64
CONV_HALO = 32
SSD_HALO = 8
INT_MIN = -2 ** 31
LOG2_E = math.log2(math.e)
NEG_INF = float("-inf")

_SZ = [2 * D_CONV, D_SSD, D_XBC, SSD_HEADS, D_ATT, D_ATT, D_ATT, IDX_HEADS * IDX_DIM, IDX_DIM, IDX_HEADS]
_OFF = np.concatenate([[0], np.cumsum(_SZ)]).tolist()
N_STD = 2 * D_CONV + D_SSD + D_XBC + D_ATT + LANES
N_TRN = IDX_HEADS * IDX_DIM + D_ATT + D_ATT + 16


def _mm(a, b):
    return jnp.dot(a, b, preferred_element_type=F32)


def _mm_nt(a, b):
    return lax.dot_general(a, b, (((1,), (1,)), ((), ())), preferred_element_type=F32)


def _split3(a):
    hi = a.astype(BF16)
    r = a - hi.astype(F32)
    mid = r.astype(BF16)
    lo = (r - mid.astype(F32)).astype(BF16)
    return hi, mid, lo


def _mm_f32_lhs(a, b_exact):
    hi, mid, lo = _split3(a)
    return (_mm(hi, b_exact) + _mm(mid, b_exact)) + _mm(lo, b_exact)


def _mm_f32_rhs(a_exact, b):
    hi, mid, lo = _split3(b)
    return (_mm(a_exact, hi) + _mm(a_exact, mid)) + _mm(a_exact, lo)


def _mm_nt_f32_rhs(a_exact, b):
    hi, mid, lo = _split3(b)
    return (_mm_nt(a_exact, hi) + _mm_nt(a_exact, mid)) + _mm_nt(a_exact, lo)


def _mm_f32_f32(a, b):
    a1, a2, _ = _split3(a)
    b1, b2, _ = _split3(b)
    return _mm(a1, b1) + (_mm(a1, b2) + _mm(a2, b1))


def _silu(x):
    return x * jax.nn.sigmoid(x)


def _rms_rows(x, g):
    ms = jnp.mean(x * x, axis=-1, keepdims=True)
    return (x * lax.rsqrt(ms + EPS)) * g


def _fold_rows(x, op):
    parts = [x[r:r + SUBLANES] for r in range(0, x.shape[0], SUBLANES)]
    while len(parts) > 1:
        nxt = [op(parts[k], parts[k + 1]) for k in range(0, len(parts) - 1, 2)]
        if len(parts) % 2:
            nxt.append(parts[-1])
        parts = nxt
    return parts[0]


def _params(n_axes):
    return pltpu.CompilerParams(dimension_semantics=("arbitrary",) * n_axes,
                                vmem_limit_bytes=VMEM_LIMIT_BYTES)


def _full(shape):
    n = len(shape)
    return pl.BlockSpec(shape, lambda *_: (0,) * n)


def _in_proj_kernel(x_ref, g_ref, wstd_ref, wtrn_ref, ct_ref, st_ref, qg_ref, kg_ref,
                    gmat_ref, pmat_ref,
                    uconv_ref, z_ref, xbc_ref, kn_ref, small_ref, kib_ref, qit_ref, qnt_ref, vt_ref, wt_ref):
    tm = x_ref.shape[0]
    hb = _rms_rows(x_ref[...], g_ref[...]).astype(BF16)
    uconv_ref[...] = _mm(hb, wstd_ref[:, 0:512])
    z_ref[...] = _mm(hb, wstd_ref[:, 512:1024])
    xbc_ref[...] = _mm(hb, wstd_ref[:, 1024:2048])

    ct = ct_ref[...]
    st = st_ref[...]
    rest = ATT_HEAD_DIM - ROT_DIM
    c128 = jnp.concatenate([ct, ct, jnp.ones((rest, tm), F32)] * 2, axis=0).T
    s128 = jnp.concatenate([-st, st, jnp.zeros((rest, tm), F32)] * 2, axis=0).T
    kraw = _mm(hb, wstd_ref[:, 2048:2304])
    kn = (kraw * lax.rsqrt(_mm_f32_lhs(kraw * kraw, gmat_ref[...]) + EPS)) * kg_ref[...]
    kn = kn * jnp.concatenate([c128, c128], axis=1) + \
        _mm_f32_lhs(kn, pmat_ref[...]) * jnp.concatenate([s128, s128], axis=1)
    kn_ref[...] = kn.astype(BF16)

    sm = _mm(hb, wstd_ref[:, 2304:2432])
    lane = lax.broadcasted_iota(I32, (tm, LANES), 1)
    is_idx = lane < IDX_DIM
    sm = sm * jnp.where(is_idx, c128, 1.0) + _mm_f32_lhs(sm, pmat_ref[0:LANES, 0:LANES]) * jnp.where(is_idx, s128, 0.0)
    small_ref[...] = sm
    kib_ref[...] = jnp.where(is_idx, sm, 0.0).astype(BF16)

    out_t = _mm_nt(wtrn_ref[...], hb)

    def rope_t(blk):
        x1 = blk[0:8]
        x2 = blk[8:16]
        return jnp.concatenate([x1 * ct - x2 * st, x2 * ct + x1 * st, blk[16:64]], axis=0)

    qi = [rope_t(out_t[h * 64:(h + 1) * 64]) for h in range(IDX_HEADS)]
    qit_ref[...] = jnp.concatenate(qi, axis=0).astype(BF16)

    qn = []
    for h in range(ATT_HEADS):
        blk = out_t[512 + h * 64:512 + (h + 1) * 64]
        ms = jnp.mean(blk * blk, axis=0, keepdims=True)
        blk = (blk * lax.rsqrt(ms + EPS)) * qg_ref[...]
        qn.append(rope_t(blk) * (ATT_HEAD_DIM ** -0.5 * LOG2_E))
    qnt_ref[...] = jnp.concatenate(qn, axis=0).astype(BF16)

    vt = out_t[768:1024].astype(BF16)
    for c in range(tm // KEY_TILE):
        vt_ref[c] = vt[:, c * KEY_TILE:(c + 1) * KEY_TILE]
    wt_ref[...] = out_t[1024:1032] * (IDX_HEADS ** -0.5 * IDX_DIM ** -0.5)


def _in_proj(x2, g, wstd, wtrn, ct, st, qg, kg, gmat, pmat):
    t = x2.shape[0]
    tm = TOKEN_TILE
    row = lambda w: pl.BlockSpec((tm, w), lambda i: (i, 0))
    col = lambda r: pl.BlockSpec((r, tm), lambda i: (0, i))
    out_shape = (
        jax.ShapeDtypeStruct((t, 512), F32), jax.ShapeDtypeStruct((t, 512), F32),
        jax.ShapeDtypeStruct((t, 1024), F32), jax.ShapeDtypeStruct((t, 256), BF16),
        jax.ShapeDtypeStruct((t, LANES), F32), jax.ShapeDtypeStruct((t, LANES), BF16),
        jax.ShapeDtypeStruct((512, t), BF16), jax.ShapeDtypeStruct((256, t), BF16),
        jax.ShapeDtypeStruct((t // KEY_TILE, 256, KEY_TILE), BF16), jax.ShapeDtypeStruct((8, t), F32))
    out_specs = (row(512), row(512), row(1024), row(256), row(LANES), row(LANES), col(512), col(256),
                 pl.BlockSpec((tm // KEY_TILE, 256, KEY_TILE), lambda i: (i, 0, 0)), col(8))
    in_specs = [row(D_MODEL), _full((1, D_MODEL)), _full(wstd.shape), _full(wtrn.shape),
                col(8), col(8), _full((64, 1)), _full((1, 256)), _full((256, 256)), _full((256, 256))]
    return pl.pallas_call(_in_proj_kernel, grid=(t // tm,), in_specs=in_specs, out_specs=out_specs,
                          out_shape=out_shape, compiler_params=_params(1), name="in_proj")(
        x2, g, wstd, wtrn, ct, st, qg, kg, gmat, pmat)


def _conv_kernel(u_ref, dw_ref, lg_ref, lb_ref, pw_ref, out_ref, hp_ref, hs_ref):
    ts = u_ref.shape[0]

    @pl.when(pl.program_id(1) == 0)
    def _():
        hp_ref[0:CONV_HALO, :] = jnp.zeros((CONV_HALO, D_CONV), F32)

    u = u_ref[...]
    hp_ref[CONV_HALO:CONV_HALO + ts, :] = u[:, :D_CONV] * jax.nn.sigmoid(u[:, D_CONV:])
    n_sh = hs_ref.shape[1]
    for s in range(1, SUBLANES):
        hs_ref[s - 1] = hp_ref[pl.ds(s, n_sh), :]
    base = CONV_HALO - (CONV_WIDTH - 1)

    def tap(r0, k):
        a, s = divmod(base + k, SUBLANES)
        if s == 0:
            return hp_ref[pl.ds(r0 + a * SUBLANES, CONV_ROWS), :]
        return hs_ref[s - 1, pl.ds(r0 + a * SUBLANES, CONV_ROWS), :]

    for r0 in range(0, ts, CONV_ROWS):
        acc = dw_ref[0:1, :] * tap(r0, 0)
        for k in range(1, CONV_WIDTH):
            acc = acc + dw_ref[k:k + 1, :] * tap(r0, k)
        mu = jnp.mean(acc, axis=-1, keepdims=True)
        xc = acc - mu
        y = xc * lax.rsqrt(jnp.mean(xc * xc, axis=-1, keepdims=True) + EPS)
        y = _silu(y * lg_ref[...] + lb_ref[...])
        out_ref[pl.ds(r0, CONV_ROWS), :] = _mm(y.astype(BF16), pw_ref[...])
    hp_ref[0:CONV_HALO, :] = hp_ref[ts:ts + CONV_HALO, :]


def _conformer_conv(uconv, dw, lg, lb, pw, batch, seq):
    ts = SEQ_TILE
    ns = seq // ts
    return pl.pallas_call(
        _conv_kernel, grid=(batch, ns),
        in_specs=[pl.BlockSpec((ts, 512), lambda b, s: (b * ns + s, 0)), _full((32, D_CONV)), _full((1, D_CONV)),
                  _full((1, D_CONV)), _full((D_CONV, D_CONV))],
        out_specs=pl.BlockSpec((ts, D_CONV), lambda b, s: (b * ns + s, 0)),
        out_shape=jax.ShapeDtypeStruct((batch * seq, D_CONV), F32),
        scratch_shapes=[pltpu.VMEM((CONV_HALO + ts, D_CONV), F32),
                        pltpu.VMEM((SUBLANES - 1, CONV_HALO + ts - SUBLANES, D_CONV), F32)],
        compiler_params=_params(2), name="conformer_conv")(uconv, dw, lg, lb, pw)


def _ssd_kernel(z_ref, xbc_ref, small_ref, cw_ref, cb_ref, dtb_ref, alog_ref, dskip_ref, ng_ref,
                tri_ref, rsel_ref, rep_ref, sel8_ref, out_ref, xp_ref, state_ref):
    ts = z_ref.shape[0]
    L = SSD_CHUNK

    @pl.when(pl.program_id(1) == 0)
    def _():
        xp_ref[0:SSD_HALO, :] = jnp.zeros((SSD_HALO, D_XBC), F32)
        state_ref[...] = jnp.zeros(state_ref.shape, F32)

    xp_ref[SSD_HALO:SSD_HALO + ts, :] = xbc_ref[...]
    lane1 = lax.broadcasted_iota(I32, (1, LANES), 1)
    is_dt = (lane1 >= IDX_DIM) & (lane1 < IDX_DIM + SSD_HEADS)
    a_neg = jnp.where(is_dt, -jnp.exp(alog_ref[...]), 0.0)
    ii = lax.broadcasted_iota(I32, (L, L), 0)
    jj = lax.broadcasted_iota(I32, (L, L), 1)
    causal = ii >= jj
    first_half = jj < SSD_HEAD_DIM
    gw = D_SSD // SSD_GROUPS

    for c in range(ts // L):
        r0 = c * L
        ext = xp_ref[pl.ds(r0, L + SSD_HALO), :]
        conv = cb_ref[...] + cw_ref[SSD_CONV - 1:SSD_CONV, :] * ext[SSD_HALO:, :]
        for k in range(SSD_CONV - 1):
            conv = conv + cw_ref[k:k + 1, :] * pltpu.roll(ext, SSD_CONV - 1 - k, axis=0)[SSD_HALO:, :]
        xa = _silu(conv)
        xs = xa[:, 0:D_SSD]
        bm = xa[:, D_SSD:D_SSD + 256]
        cm = xa[:, D_SSD + 256:D_SSD + 512]

        dt_arg = small_ref[pl.ds(r0, L), :] + dtb_ref[...]
        dt_s = jnp.maximum(dt_arg, 0.0) + jnp.log1p(jnp.exp(-jnp.abs(dt_arg)))
        acs_s = _mm_f32_rhs(tri_ref[...], dt_s * a_neg)
        dt = _mm_f32_lhs(dt_s, rsel_ref[...])
        acs = _mm_f32_lhs(acs_s, rsel_ref[...])
        acs_wide = _mm_f32_lhs(acs_s, rep_ref[...])
        acs_t = _mm_nt_f32_rhs(sel8_ref[...], acs_s)
        last = acs[L - 1:L, :]
        decay_end = jnp.exp(last - acs)
        decay_in = jnp.exp(acs)
        xd = xs * dt
        xdb = xd.astype(BF16)

        y_parts = []
        for g in range(SSD_GROUPS):
            bg = bm[:, g * SSD_STATE:(g + 1) * SSD_STATE]
            cgb = cm[:, g * SSD_STATE:(g + 1) * SSD_STATE].astype(BF16)
            cb = _mm_nt(cgb, bg.astype(BF16))
            pair_out = []
            for pr in range(2):
                h0 = g * 4 + pr * 2
                xpair = xdb[:, h0 * 64:(h0 + 2) * 64]
                outs = []
                for h in (h0, h0 + 1):
                    seg = acs_wide[:, h * L:(h + 1) * L] - acs_t[h:h + 1, :]
                    m_h = (cb * jnp.exp(jnp.where(causal, seg, NEG_INF))).astype(BF16)
                    outs.append(_mm(m_h, xpair))
                pair_out.append(jnp.where(first_half, outs[0], outs[1]))
            y_diag = jnp.concatenate(pair_out, axis=1)
            gs = slice(g * gw, (g + 1) * gw)
            st = state_ref[:, gs]
            y_off = _mm(cgb, st.astype(BF16)) * decay_in[:, gs]
            contrib = _mm(bg.T.astype(BF16), (xd[:, gs] * decay_end[:, gs]).astype(BF16))
            state_ref[:, gs] = jnp.exp(last[:, gs]) * st + contrib
            y_parts.append(y_diag + y_off)
        y = jnp.concatenate(y_parts, axis=1) + xs * dskip_ref[...]
        y = y * _silu(z_ref[pl.ds(r0, L), :])
        outs = []
        for g in range(SSD_GROUPS):
            gs = slice(g * gw, (g + 1) * gw)
            outs.append(_rms_rows(y[:, gs], ng_ref[:, gs]))
        out_ref[pl.ds(r0, L), :] = jnp.concatenate(outs, axis=1)
    xp_ref[0:SSD_HALO, :] = xp_ref[ts:ts + SSD_HALO, :]


def _ssd(z, xbc, small, cw, cb, dtb, alog, dskip, ng, consts, batch, seq):
    ts = SEQ_TILE
    ns = seq // ts
    tri, rsel, rep, sel8 = consts
    row = lambda w: pl.BlockSpec((ts, w), lambda b, s: (b * ns + s, 0))
    return pl.pallas_call(
        _ssd_kernel, grid=(batch, ns),
        in_specs=[row(512), row(1024), row(LANES), _full((8, D_XBC)), _full((1, D_XBC)), _full((1, LANES)),
                  _full((1, LANES)), _full((1, 512)), _full((1, 512)), _full(tri.shape), _full(rsel.shape),
                  _full(rep.shape), _full(sel8.shape)],
        out_specs=row(512),
        out_shape=jax.ShapeDtypeStruct((batch * seq, 512), F32),
        scratch_shapes=[pltpu.VMEM((SSD_HALO + ts, D_XBC), F32), pltpu.VMEM((SSD_STATE, D_SSD), F32)],
        compiler_params=_params(2), name="ssd_mixer")(z, xbc, small, cw, cb, dtb, alog, dskip, ng, tri, rsel, rep, sel8)


def _dsa_kernel(qit_ref, qnt_ref, wt_ref, kib_ref, kn_ref, vt_ref, tril_ref, out_ref, keys_ref, keysb_ref, s_ref,
                acc_ref, *, ksel):
    qb = Q_TILE
    kb = KEY_TILE
    i = pl.program_id(1)
    nsb = (i * qb + qb + kb - 1) // kb
    qpos = i * qb + lax.broadcasted_iota(I32, (kb, qb), 1)
    krow = lax.broadcasted_iota(I32, (kb, qb), 0)

    rhs_idx = jnp.concatenate([qit_ref[h * IDX_DIM:(h + 1) * IDX_DIM, :] for h in range(IDX_HEADS)], axis=1)
    rhs_idx = jnp.concatenate([rhs_idx, jnp.zeros((LANES - IDX_DIM, IDX_HEADS * qb), BF16)], axis=0)
    wt = wt_ref[...]

    def score_tile(j, carry):
        r = pl.multiple_of(j * kb, kb)
        lg = _mm(kib_ref[pl.ds(r, kb), :], rhs_idx)
        sc = wt[0:1, :] * jnp.maximum(lg[:, 0:qb], 0.0)
        for h in range(1, IDX_HEADS):
            sc = sc + wt[h:h + 1, :] * jnp.maximum(lg[:, h * qb:(h + 1) * qb], 0.0)
        sc = jnp.where(j * kb + krow <= qpos, sc, NEG_INF)
        keys_ref[pl.ds(r, kb), :] = sc
        keysb_ref[pl.ds(r, kb), :] = sc.astype(BF16)
        return carry

    lax.fori_loop(0, nsb, score_tile, 0)

    def key_to_float(k):
        return pltpu.bitcast(k ^ (lax.shift_right_arithmetic(k, 31) & 0x7FFFFFFF), F32)

    n_count = jnp.where(i * qb + qb <= ksel, 0, nsb)

    def count(cand, strict):
        def body(j, acc):
            kt = keys_ref[pl.ds(pl.multiple_of(j * kb, kb), kb), :]
            hit = (kt > cand) if strict else (kt >= cand)
            return acc + _fold_rows(jnp.where(hit, 1.0, 0.0), jnp.add)
        acc = lax.fori_loop(0, n_count, body, jnp.zeros((SUBLANES, qb), F32))
        return jnp.sum(acc, axis=0, keepdims=True)

    pack = 2 * SUBLANES
    one_b = jnp.ones((pack, qb), BF16)
    zero_b = jnp.zeros((pack, qb), BF16)

    def count_coarse(cand_b):
        def body(j, acc):
            kt = keysb_ref[pl.ds(pl.multiple_of(j * kb, kb), kb), :]
            parts = [jnp.where(kt[g * pack:(g + 1) * pack] >= cand_b, one_b, zero_b) for g in range(kb // pack)]
            while len(parts) > 1:
                parts = [parts[k] + parts[k + 1] for k in range(0, len(parts), 2)]
            return acc + parts[0]
        acc = lax.fori_loop(0, n_count, body, zero_b)
        return jnp.sum(acc.astype(F32), axis=0, keepdims=True)

    def coarse_step(it, t16):
        cand16 = t16 | lax.shift_left(jnp.int32(1), 15 - it)
        k = lax.shift_left(cand16, 16) ^ INT_MIN
        cand = pltpu.bitcast(k ^ (lax.shift_right_arithmetic(k, 31) & 0x7FFF0000), F32)
        cnt = count_coarse(jnp.concatenate([cand, cand], axis=0).astype(BF16))
        return jnp.where(cnt >= float(ksel), cand16, t16)

    t16 = lax.fori_loop(0, 16, coarse_step, jnp.zeros((SUBLANES, qb), I32))
    k1 = lax.shift_left(t16, 16) ^ INT_MIN
    c1 = pltpu.bitcast(k1 ^ (lax.shift_right_arithmetic(k1, 31) & 0x7FFF0000), F32)
    bits1 = pltpu.bitcast(c1, I32)
    key1 = bits1 ^ (lax.shift_right_arithmetic(bits1, 31) & 0x7FFFFFFF)
    def fine_step(it, carry):
        lo, hi = carry
        mid = lo + lax.shift_right_arithmetic(hi - lo + 1, 1)
        ok = count(key_to_float(mid)[0:1, :], False) >= float(ksel)
        return jnp.where(ok, mid, lo), jnp.where(ok, hi, mid - 1)

    lo, _ = lax.fori_loop(0, 17, fine_step, (key1 - 0x8000, key1 + 0x10000))
    tau = key_to_float(lo)[0:1, :]
    takes_all = qpos[0:1, :] < ksel
    tau = jnp.where(takes_all, NEG_INF, tau)
    need = jnp.where(takes_all, 0.0, float(ksel) - count(tau, True))

    zblk = jnp.zeros((ATT_HEAD_DIM, qb), BF16)
    rhs_att = jnp.concatenate(
        [jnp.concatenate([qnt_ref[h * 64:(h + 1) * 64, :] if hh == h else zblk for hh in range(ATT_HEADS)], axis=1)
         for h in range(ATT_HEADS)], axis=0)

    def pass1(j, carry):
        m, eq_before = carry
        r = pl.multiple_of(j * kb, kb)
        s = _mm(kn_ref[pl.ds(r, kb), :], rhs_att)
        kt = keys_ref[pl.ds(r, kb), :]
        eq = kt == tau
        eqf = jnp.where(eq, 1.0, 0.0)
        rank = eq_before + _mm(tril_ref[...], eqf.astype(BF16))
        sel = (kt > tau) | (eq & (rank < need))
        s = jnp.concatenate([jnp.where(sel, s[:, h * qb:(h + 1) * qb], NEG_INF) for h in range(ATT_HEADS)], axis=1)
        s_ref[pl.ds(r, kb), :] = s
        return (jnp.maximum(m, _fold_rows(s, jnp.maximum)),
                eq_before + jnp.sum(_fold_rows(eqf, jnp.add), axis=0, keepdims=True))

    m8, _ = lax.fori_loop(0, nsb, pass1,
                          (jnp.full((SUBLANES, ATT_HEADS * qb), NEG_INF, F32), jnp.zeros((1, qb), F32)))
    m = jnp.max(m8, axis=0, keepdims=True)

    acc_ref[...] = jnp.zeros(acc_ref.shape, F32)

    def pass2(j, l8):
        p = jnp.exp2(s_ref[pl.ds(pl.multiple_of(j * kb, kb), kb), :] - m)
        pb = p.astype(BF16)
        vt = vt_ref[j]
        for h in range(ATT_HEADS):
            hs = slice(h * ATT_HEAD_DIM, (h + 1) * ATT_HEAD_DIM)
            acc_ref[hs, :] += _mm(vt[hs, :], pb[:, h * qb:(h + 1) * qb])
        return l8 + _fold_rows(p, jnp.add)

    l = jnp.sum(lax.fori_loop(0, nsb, pass2, jnp.zeros((SUBLANES, ATT_HEADS * qb), F32)), axis=0, keepdims=True)
    o_t = jnp.concatenate(
        [acc_ref[h * 64:(h + 1) * 64, :] / l[:, h * qb:(h + 1) * qb] for h in range(ATT_HEADS)],
        axis=0)
    out_ref[...] = o_t.T


def _dsa(qit, qnt, wt, kib, kn, vt, tril, batch, seq):
    qb = Q_TILE
    nq = seq // qb
    nkb = seq // KEY_TILE
    ksel = min(TOPK_MAX, seq // 4)
    colq = lambda r: pl.BlockSpec((r, qb), lambda b, i: (0, b * nq + i))
    return pl.pallas_call(
        functools.partial(_dsa_kernel, ksel=ksel), grid=(batch, nq),
        in_specs=[colq(512), colq(256), colq(8),
                  pl.BlockSpec((seq, LANES), lambda b, i: (b, 0)), pl.BlockSpec((seq, 256), lambda b, i: (b, 0)),
                  pl.BlockSpec((nkb, 256, KEY_TILE), lambda b, i: (b, 0, 0)), _full((KEY_TILE, KEY_TILE))],
        out_specs=pl.BlockSpec((qb, D_ATT), lambda b, i: (b * nq + i, 0)),
        out_shape=jax.ShapeDtypeStruct((batch * seq, D_ATT), F32),
        scratch_shapes=[pltpu.VMEM((seq, qb), F32), pltpu.VMEM((seq, qb), BF16),
                        pltpu.VMEM((seq, ATT_HEADS * qb), F32),
                        pltpu.VMEM((D_ATT, qb), F32)],
        compiler_params=_params(2), name="dsa_attention")(qit, qnt, wt, kib, kn, vt, tril)


def _mem_prep_kernel(mem_ref, g_ref, wkv_ref, kg_ref, gmat_ref, kbd_ref, vbd_ref):
    nm = mem_ref.shape[0]
    mb = _rms_rows(mem_ref[...], g_ref[...]).astype(BF16)
    kv = _mm(mb, wkv_ref[...])
    k = kv[:, 0:D_MEMATT]
    kn = (k * lax.rsqrt(_mm_f32_lhs(k * k, gmat_ref[...]) + EPS)) * kg_ref[...]
    k_t = kn.T.astype(BF16)
    zero = jnp.zeros((MEM_HEAD_DIM, nm), BF16)
    kbd_ref[...] = jnp.concatenate(
        [jnp.concatenate([k_t[h * 64:(h + 1) * 64, :] if hh == h else zero for hh in range(MEM_HEADS)], axis=1)
         for h in range(MEM_HEADS)], axis=0)
    v = kv[:, D_MEMATT:]
    lane = lax.broadcasted_iota(I32, (nm, D_MEMATT), 1)
    vbd_ref[...] = jnp.concatenate(
        [jnp.where((lane >= h * 64) & (lane < (h + 1) * 64), v, 0.0) for h in range(MEM_HEADS)],
        axis=0).astype(BF16)


def _mem_prep(mem2, g, wkv, kg, gmat, batch, nm):
    return pl.pallas_call(
        _mem_prep_kernel, grid=(batch,),
        in_specs=[pl.BlockSpec((nm, D_MODEL), lambda b: (b, 0)), _full((1, D_MODEL)), _full((D_MODEL, 512)),
                  _full((1, 256)), _full((256, 256))],
        out_specs=(pl.BlockSpec((None, D_MEMATT, MEM_HEADS * nm), lambda b: (b, 0, 0)),
                   pl.BlockSpec((None, MEM_HEADS * nm, D_MEMATT), lambda b: (b, 0, 0))),
        out_shape=(jax.ShapeDtypeStruct((batch, D_MEMATT, MEM_HEADS * nm), BF16),
                   jax.ShapeDtypeStruct((batch, MEM_HEADS * nm, D_MEMATT), BF16)),
        compiler_params=_params(1), name="mem_prep")(mem2, g, wkv, kg, gmat)


def _mix_memx_kernel(x_ref, yc_ref, ys_ref, ya_ref, wout_ref, g_ref, wq_ref, qg_ref, gmat_ref, kbd_ref, vbd_ref,
                     wo_ref, out_ref):
    nm = kbd_ref.shape[1] // MEM_HEADS
    y_mix = jnp.concatenate([yc_ref[...], ys_ref[...], ya_ref[...]], axis=1).astype(BF16)
    x1 = x_ref[...] + _mm(y_mix, wout_ref[...])
    hb = _rms_rows(x1, g_ref[...]).astype(BF16)
    q = _mm(hb, wq_ref[...])
    qn = (q * lax.rsqrt(_mm_f32_lhs(q * q, gmat_ref[...]) + EPS)) * qg_ref[...]
    s = _mm(qn.astype(BF16), kbd_ref[...]) * (MEM_HEAD_DIM ** -0.5)
    ps = []
    for h in range(MEM_HEADS):
        sh = s[:, h * nm:(h + 1) * nm]
        e = jnp.exp(sh - jnp.max(sh, axis=-1, keepdims=True))
        ps.append(e / jnp.sum(e, axis=-1, keepdims=True))
    o = _mm(jnp.concatenate(ps, axis=1).astype(BF16), vbd_ref[...])
    out_ref[...] = x1 + _mm(o.astype(BF16), wo_ref[...])


def _mix_memx(x2, yc, ys, ya, wout, g, wq, qg, gmat, kbd, vbd, wo, seq):
    t = x2.shape[0]
    tm = min(MIX_TILE, seq)
    per_b = seq // tm
    nm4 = kbd.shape[2]
    row = lambda w: pl.BlockSpec((tm, w), lambda i: (i, 0))
    return pl.pallas_call(
        _mix_memx_kernel, grid=(t // tm,),
        in_specs=[row(D_MODEL), row(256), row(512), row(256), _full((D_MODEL, D_MODEL)), _full((1, D_MODEL)),
                  _full((D_MODEL, 256)), _full((1, 256)), _full((256, 256)),
                  pl.BlockSpec((None, D_MEMATT, nm4), lambda i: (i // per_b, 0, 0)),
                  pl.BlockSpec((None, nm4, D_MEMATT), lambda i: (i // per_b, 0, 0)), _full((256, D_MODEL))],
        out_specs=row(D_MODEL), out_shape=jax.ShapeDtypeStruct((t, D_MODEL), F32),
        compiler_params=_params(1), name="mix_memx")(x2, yc, ys, ya, wout, g, wq, qg, gmat, kbd, vbd, wo)


def _ffn_kernel(x_ref, g_ref, wg_ref, wu_ref, wd_ref, out_ref, acc_ref):
    c = pl.program_id(1)

    @pl.when(c == 0)
    def _():
        acc_ref[...] = jnp.zeros(acc_ref.shape, F32)

    hb = _rms_rows(x_ref[...], g_ref[...]).astype(BF16)
    a = _silu(_mm(hb, wg_ref[...])) * _mm(hb, wu_ref[...])
    acc_ref[...] += _mm(a.astype(BF16), wd_ref[...])

    @pl.when(c == pl.num_programs(1) - 1)
    def _():
        out_ref[...] = x_ref[...] + acc_ref[...]


def _ffn(x2, g, wg, wu, wd):
    t = x2.shape[0]
    tm = TOKEN_TILE
    fc = FF_CHUNK
    return pl.pallas_call(
        _ffn_kernel, grid=(t // tm, D_FF // fc),
        in_specs=[pl.BlockSpec((tm, D_MODEL), lambda i, c: (i, 0)), _full((1, D_MODEL)),
                  pl.BlockSpec((D_MODEL, fc), lambda i, c: (0, c)), pl.BlockSpec((D_MODEL, fc), lambda i, c: (0, c)),
                  pl.BlockSpec((fc, D_MODEL), lambda i, c: (c, 0))],
        out_specs=pl.BlockSpec((tm, D_MODEL), lambda i, c: (i, 0)),
        out_shape=jax.ShapeDtypeStruct((t, D_MODEL), F32),
        scratch_shapes=[pltpu.VMEM((tm, D_MODEL), F32)],
        compiler_params=_params(2), name="ffn_dense")(x2, g, wg, wu, wd)


def _router_kernel(x_ref, g_ref, wr_ref, ltri_ref, ri_ref, rf_ref, cnt_ref, carry_ref):
    tm = x_ref.shape[0]
    lane = lax.broadcasted_iota(I32, (tm, LANES), 1)

    @pl.when(pl.program_id(0) == 0)
    def _():
        carry_ref[...] = jnp.zeros(carry_ref.shape, F32)

    h = _rms_rows(x_ref[...], g_ref[...])
    logits = jnp.where(lane < N_EXPERTS, _mm_f32_f32(h, wr_ref[...]), NEG_INF)
    v1 = jnp.max(logits, axis=-1, keepdims=True)
    i1 = jnp.min(jnp.where(logits == v1, lane, LANES), axis=-1, keepdims=True)
    rest = jnp.where(lane == i1, NEG_INF, logits)
    v2 = jnp.max(rest, axis=-1, keepdims=True)
    i2 = jnp.min(jnp.where(rest == v2, lane, LANES), axis=-1, keepdims=True)
    e2 = jnp.exp(v2 - v1)
    den = 1.0 + e2
    hit = jnp.where((lane == i1) | (lane == i2), 1.0, 0.0)
    before = carry_ref[...] + _mm(ltri_ref[...], hit.astype(BF16))
    r1 = jnp.sum(jnp.where(lane == i1, before, 0.0), axis=-1, keepdims=True).astype(I32)
    r2 = jnp.sum(jnp.where(lane == i2, before, 0.0), axis=-1, keepdims=True).astype(I32)
    carry_ref[...] = carry_ref[...] + jnp.sum(hit, axis=0, keepdims=True)
    cnt_ref[...] = carry_ref[...]
    ri_ref[...] = jnp.where(lane == 0, i1, jnp.where(lane == 1, i2, jnp.where(lane == 2, r1,
                                                                             jnp.where(lane == 3, r2, 0))))
    rf_ref[...] = jnp.where(lane == 0, 1.0 / den, jnp.where(lane == 1, e2 / den, 0.0))


def _router(x2, g, wr, ltri):
    t = x2.shape[0]
    tm = TOKEN_TILE
    row = lambda w: pl.BlockSpec((tm, w), lambda i: (i, 0))
    return pl.pallas_call(
        _router_kernel, grid=(t // tm,),
        in_specs=[row(D_MODEL), _full((1, D_MODEL)), _full((D_MODEL, LANES)), _full((tm, tm))],
        out_specs=(row(LANES), row(LANES), _full((1, LANES))),
        out_shape=(jax.ShapeDtypeStruct((t, LANES), I32), jax.ShapeDtypeStruct((t, LANES), F32),
                   jax.ShapeDtypeStruct((1, LANES), F32)),
        scratch_shapes=[pltpu.VMEM((1, LANES), F32)],
        compiler_params=_params(1), name="moe_router")(x2, g, wr, ltri)


def _row_copy(src, src_row, dst, dst_row, sem):
    return pltpu.make_async_copy(src.at[pl.ds(src_row, 1), :], dst.at[pl.ds(dst_row, 1), :], sem)


def _dispatch_kernel(pend_ref, padded_ref, dest_ref, x_ref, xs_hbm, zero_ref, sem, zero_sem):
    n_tok = x_ref.shape[0]
    bm = zero_ref.shape[0]

    @pl.when(pl.program_id(0) == 0)
    def _():
        zero_ref[...] = jnp.zeros(zero_ref.shape, F32)
        n_slots = xs_hbm.shape[0]
        zero_blocks = [(padded_ref[e] > 0, pend_ref[e] - bm) for e in range(N_EXPERTS)]
        zero_blocks += [(pend_ref[N_EXPERTS - 1] + e * bm < n_slots, pend_ref[N_EXPERTS - 1] + e * bm)
                        for e in range(N_EXPERTS)]
        for cond, start in zero_blocks:
            @pl.when(cond)
            def _():
                dst = xs_hbm.at[pl.ds(pl.multiple_of(start, bm), bm), :]
                pltpu.make_async_copy(zero_ref, dst, zero_sem).start()
        for cond, _ in zero_blocks:
            @pl.when(cond)
            def _():
                pltpu.make_async_copy(zero_ref, xs_hbm.at[pl.ds(0, bm), :], zero_sem).wait()

    def issue(r, c):
        for k in range(2):
            _row_copy(x_ref, r, xs_hbm, dest_ref[2 * r + k], sem).start(priority=k)
        return c

    lax.fori_loop(0, n_tok, issue, 0, unroll=MOE_DMA_UNROLL)
    for _ in range(2):
        pltpu.make_async_copy(x_ref, xs_hbm.at[pl.ds(0, n_tok), :], sem).wait()


def _dispatch(pend, padded, dest_flat, x2, n_slots):
    t = x2.shape[0]
    td = MOE_DISPATCH_TILE
    grid_spec = pltpu.PrefetchScalarGridSpec(
        num_scalar_prefetch=2, grid=(t // td,),
        in_specs=[pl.BlockSpec((2 * td,), lambda i, pe, pa: (i,), memory_space=pltpu.SMEM),
                  pl.BlockSpec((td, D_MODEL), lambda i, pe, pa: (i, 0))],
        out_specs=pl.BlockSpec(memory_space=pl.ANY),
        scratch_shapes=[pltpu.VMEM((MOE_SLOT_BLOCK, D_MODEL), F32), pltpu.SemaphoreType.DMA(()),
                        pltpu.SemaphoreType.DMA(())])
    return pl.pallas_call(
        _dispatch_kernel, grid_spec=grid_spec, out_shape=jax.ShapeDtypeStruct((n_slots, D_MODEL), F32),
        compiler_params=_params(1), name="moe_dispatch")(pend, padded, dest_flat, x2)


def _experts_kernel(bexp_ref, nused_ref, xs_ref, g_ref, wg_ref, wu_ref, wd_ref, out_ref, acc_ref):
    del bexp_ref
    c = pl.program_id(1)
    last_c = pl.num_programs(1) - 1
    used = pl.program_id(0) < nused_ref[0]

    @pl.when(used & (c == 0))
    def _():
        acc_ref[...] = jnp.zeros(acc_ref.shape, F32)

    @pl.when(used)
    def _():
        hb = _rms_rows(xs_ref[...], g_ref[...]).astype(BF16)
        a = _silu(_mm(hb, wg_ref[...])) * _mm(hb, wu_ref[...])
        acc_ref[...] += _mm(a.astype(BF16), wd_ref[...])

    @pl.when(used & (c == last_c))
    def _():
        out_ref[...] = acc_ref[...]

    @pl.when(jnp.logical_not(used) & (c == last_c))
    def _():
        out_ref[...] = jnp.zeros(out_ref.shape, F32)


def _experts(block_exp, n_used, xs, g, wg, wu, wd):
    n_slots = xs.shape[0]
    bm = MOE_SLOT_BLOCK
    fc = FF_CHUNK
    grid_spec = pltpu.PrefetchScalarGridSpec(
        num_scalar_prefetch=2, grid=(n_slots // bm, D_FF // fc),
        in_specs=[pl.BlockSpec((bm, D_MODEL), lambda b, c, be, nu: (b, 0)),
                  pl.BlockSpec((1, D_MODEL), lambda b, c, be, nu: (0, 0)),
                  pl.BlockSpec((None, D_MODEL, fc), lambda b, c, be, nu: (be[b], 0, c)),
                  pl.BlockSpec((None, D_MODEL, fc), lambda b, c, be, nu: (be[b], 0, c)),
                  pl.BlockSpec((None, fc, D_MODEL), lambda b, c, be, nu: (be[b], c, 0))],
        out_specs=pl.BlockSpec((bm, D_MODEL), lambda b, c, be, nu: (b, 0)),
        scratch_shapes=[pltpu.VMEM((bm, D_MODEL), F32)])
    return pl.pallas_call(
        _experts_kernel, grid_spec=grid_spec, out_shape=jax.ShapeDtypeStruct((n_slots, D_MODEL), F32),
        compiler_params=_params(2), name="moe_experts")(block_exp, n_used, xs, g, wg, wu, wd)


def _combine_kernel(dest_ref, dest_next_ref, x_ref, rf_ref, ys_hbm, out_ref, buf_ref, sems):
    n_tok = x_ref.shape[0]
    i = pl.program_id(0)
    slot = i % 2

    def gather(idx_ref, s):
        def issue(r, c):
            for k in range(2):
                _row_copy(ys_hbm, idx_ref[2 * r + k], buf_ref.at[s, k], r, sems.at[s]).start(priority=k)
            return c
        lax.fori_loop(0, n_tok, issue, 0, unroll=MOE_DMA_UNROLL)

    @pl.when(i == 0)
    def _():
        gather(dest_ref, 0)

    @pl.when(i + 1 < pl.num_programs(0))
    def _():
        gather(dest_next_ref, 1 - slot)

    for k in range(2):
        pltpu.make_async_copy(ys_hbm.at[pl.ds(0, n_tok), :], buf_ref.at[slot, k], sems.at[slot]).wait()
    gates = rf_ref[...]
    out_ref[...] = x_ref[...] + (buf_ref[slot, 0] * gates[:, 0:1] + buf_ref[slot, 1] * gates[:, 1:2])


def _combine(dest_flat, x2, rf, ys):
    t = x2.shape[0]
    tc = MOE_COMBINE_TILE
    n = t // tc
    return pl.pallas_call(
        _combine_kernel, grid=(n,),
        in_specs=[pl.BlockSpec((2 * tc,), lambda i: (i,), memory_space=pltpu.SMEM),
                  pl.BlockSpec((2 * tc,), lambda i: (jnp.minimum(i + 1, n - 1),), memory_space=pltpu.SMEM),
                  pl.BlockSpec((tc, D_MODEL), lambda i: (i, 0)), pl.BlockSpec((tc, LANES), lambda i: (i, 0)),
                  pl.BlockSpec(memory_space=pl.ANY)],
        out_specs=pl.BlockSpec((tc, D_MODEL), lambda i: (i, 0)),
        out_shape=jax.ShapeDtypeStruct((t, D_MODEL), F32),
        scratch_shapes=[pltpu.VMEM((2, 2, tc, D_MODEL), F32), pltpu.SemaphoreType.DMA((2,))],
        compiler_params=_params(1), name="moe_combine")(dest_flat, dest_flat, x2, rf, ys)


def _moe(x2, g, wr, wg, wu, wd, ltri):
    t = x2.shape[0]
    bm = MOE_SLOT_BLOCK
    n_blocks = (2 * t) // bm + N_EXPERTS
    ri, rf, cnt = _router(x2, g, wr, ltri)
    counts = cnt[0, :N_EXPERTS].astype(I32)
    padded = (counts + bm - 1) // bm * bm
    pend = jnp.cumsum(padded)
    pstart = pend - padded
    dest = (pstart[ri[:, 0:2]] + ri[:, 2:4]).reshape(2 * t)
    n_used = (pend[-1:] // bm).astype(I32)
    starts = jnp.arange(n_blocks, dtype=I32) * bm
    block_exp = jnp.minimum(jnp.sum((pend[None, :] <= starts[:, None]).astype(I32), axis=1), N_EXPERTS - 1)
    xs = _dispatch(pend.astype(I32), padded.astype(I32), dest, x2, n_blocks * bm)
    ys = _experts(block_exp, n_used, xs, g, wg, wu, wd)
    return _combine(dest, x2, rf, ys)


def _const_mats():
    r = np.arange(256)
    gmat = ((r[:, None] // 64) == (r[None, :] // 64)).astype(np.float32) / 64.0
    d = r % 64
    same = (r[:, None] // 64) == (r[None, :] // 64)
    swap = ((d[None, :] < 8) & (d[:, None] == d[None, :] + 8)) | \
           ((d[None, :] >= 8) & (d[None, :] < 16) & (d[:, None] == d[None, :] - 8))
    pmat = (same & swap).astype(np.float32)
    tri = np.tril(np.ones((SSD_CHUNK, SSD_CHUNK), np.float32))
    rsel = np.zeros((LANES, D_SSD), np.float32)
    rep = np.zeros((LANES, SSD_HEADS * SSD_CHUNK), np.float32)
    sel8 = np.zeros((SSD_HEADS, LANES), np.float32)
    for h in range(SSD_HEADS):
        rsel[IDX_DIM + h, h * 64:(h + 1) * 64] = 1.0
        rep[IDX_DIM + h, h * SSD_CHUNK:(h + 1) * SSD_CHUNK] = 1.0
        sel8[h, IDX_DIM + h] = 1.0
    tril = np.tril(np.ones((KEY_TILE, KEY_TILE), np.float32), -1)
    ltri = np.tril(np.ones((TOKEN_TILE, TOKEN_TILE), np.float32), -1)
    to = lambda a: jnp.asarray(a, dtype=BF16)
    return to(gmat), to(pmat), (to(tri), to(rsel), to(rep), to(sel8)), to(tril), to(ltri)


def kernel(x, mem, positions, mix_norm_g, w_in, conv_dw_w, conv_ln_g, conv_ln_b, conv_pw_w, ssd_conv_w, ssd_conv_b, ssd_dt_bias, ssd_a_log, ssd_d, ssd_norm_g, att_q_norm_g, att_k_norm_g, w_out, memx_norm_g, mem_norm_g, memx_w_q, memx_w_kv, memx_q_norm_g, memx_k_norm_g, memx_w_o, ffn_norm_g, ffn_w_gate, ffn_w_up, ffn_w_down, moe_w_router, moe_w_gate, moe_w_up, moe_w_down):
    batch, seq, d = x.shape
    nm = mem.shape[1]
    depth = w_in.shape[0]
    t = batch * seq
    assert d == D_MODEL and seq % TOKEN_TILE == 0 and seq % SEQ_TILE == 0 and seq % KEY_TILE == 0
    assert TOKEN_TILE % KEY_TILE == 0 and KEY_TILE % Q_TILE == 0 and D_FF % FF_CHUNK == 0

    gmat, pmat, ssd_consts, tril, ltri = _const_mats()

    inv = ROPE_THETA ** (-jnp.arange(0, ROT_DIM, 2, dtype=F32) / ROT_DIM)
    ang = positions.astype(F32).reshape(1, t) * inv.reshape(-1, 1)
    ct, st = jnp.cos(ang), jnp.sin(ang)

    x2 = x.reshape(t, d)
    mem2 = mem.reshape(batch * nm, d)
    row = lambda v: v.reshape(1, -1)
    rep64 = lambda v: jnp.repeat(v, SSD_HEAD_DIM).reshape(1, -1)
    dt_lanes = lambda v: jnp.concatenate(
        [jnp.zeros((IDX_DIM,), F32), v, jnp.zeros((LANES - IDX_DIM - SSD_HEADS,), F32)]).reshape(1, LANES)
    o = _OFF

    for i in range(depth):
        w = w_in[i].astype(BF16)
        wstd = jnp.concatenate([w[:, o[0]:o[3]], w[:, o[5]:o[6]], w[:, o[8]:o[9]], w[:, o[3]:o[4]],
                                jnp.zeros((d, LANES - IDX_DIM - SSD_HEADS), BF16)], axis=1)
        wtrn = jnp.concatenate([w[:, o[7]:o[8]].T, w[:, o[4]:o[5]].T, w[:, o[6]:o[7]].T, w[:, o[9]:o[10]].T,
                                jnp.zeros((N_TRN - 1032, d), BF16)], axis=0)
        uconv, z, xbc, kn, small, kib, qit, qnt, vt, wt = _in_proj(
            x2, row(mix_norm_g[i]), wstd, wtrn, ct, st, att_q_norm_g[i].reshape(64, 1),
            jnp.tile(att_k_norm_g[i], ATT_HEADS).reshape(1, 256), gmat, pmat)

        dw = jnp.concatenate([conv_dw_w[i], jnp.zeros((32 - CONV_WIDTH, D_CONV), F32)], axis=0)
        y_conv = _conformer_conv(uconv, dw, row(conv_ln_g[i]), row(conv_ln_b[i]), conv_pw_w[i].astype(BF16),
                                 batch, seq)
        cw = jnp.concatenate([ssd_conv_w[i], jnp.zeros((8 - SSD_CONV, D_XBC), F32)], axis=0)
        y_ssd = _ssd(z, xbc, small, cw, row(ssd_conv_b[i]), dt_lanes(ssd_dt_bias[i]), dt_lanes(ssd_a_log[i]),
                     rep64(ssd_d[i]), row(ssd_norm_g[i]), ssd_consts, batch, seq)
        y_att = _dsa(qit, qnt, wt, kib, kn, vt, tril, batch, seq)

        kbd, vbd = _mem_prep(mem2, row(mem_norm_g[i]), memx_w_kv[i].astype(BF16),
                             jnp.tile(memx_k_norm_g[i], MEM_HEADS).reshape(1, 256), gmat, batch, nm)
        x2 = _mix_memx(x2, y_conv, y_ssd, y_att, w_out[i].astype(BF16), row(memx_norm_g[i]),
                       memx_w_q[i].astype(BF16), jnp.tile(memx_q_norm_g[i], MEM_HEADS).reshape(1, 256), gmat,
                       kbd, vbd, memx_w_o[i].astype(BF16), seq)

        j = i // 2
        if i % 2 == 0:
            x2 = _ffn(x2, row(ffn_norm_g[i]), ffn_w_gate[j].astype(BF16), ffn_w_up[j].astype(BF16),
                      ffn_w_down[j].astype(BF16))
        else:
            wr = jnp.concatenate([moe_w_router[j], jnp.zeros((d, LANES - N_EXPERTS), F32)], axis=1)
            x2 = _moe(x2, row(ffn_norm_g[i]), wr, moe_w_gate[j].astype(BF16), moe_w_up[j].astype(BF16),
                      moe_w_down[j].astype(BF16), ltri)
    return x2.reshape(batch, seq, d)
```
